```python
import jax, jax.numpy as jnp
from jax import lax
import numpy as np

D_MODEL = 1024
BATCH = 4
SEQ = 4096
DEPTH = 4
DEC_BATCH = 32
DEC_SEQ = 1
PAST_LEN = 8192
PAGE_SIZE = 128

N_MIXERS = 3
N_NSA = (DEPTH + 2) // 3
N_GDN = (DEPTH + 1) // 3
N_LRU = DEPTH // 3

NSA_DH = 64
NSA_HEADS = D_MODEL // NSA_DH
NSA_HPG = 4
NSA_KV = NSA_HEADS // NSA_HPG
NSA_Q = NSA_HEADS * NSA_DH
NSA_KVW = NSA_KV * NSA_DH
NSA_IN = NSA_Q + 6 * NSA_KVW + 3 * NSA_HEADS
ROT_DIM = NSA_DH // 4
ROPE_THETA = 500000.0
L_CMP = 32
L_SEL = 64
N_SEL = 16
WINDOW = 512
Q_BLOCK = 128

GDN_DK = 128
GDN_DV = 128
GDN_HEADS = D_MODEL // GDN_DV
GDN_QK = GDN_HEADS * GDN_DK
GDN_VW = GDN_HEADS * GDN_DV
GDN_CONV = 4
GDN_CONV_CH = 2 * GDN_QK + GDN_VW
GDN_IN = GDN_CONV_CH + GDN_VW + 2 * GDN_HEADS
GDN_CHUNK = 64

D_RNN = D_MODEL
LRU_BLOCKS = 8
LRU_BW = D_RNN // LRU_BLOCKS
LRU_CONV = 4
LRU_C = 8.0

D_FF = ((8 * D_MODEL // 3 + 127) // 128) * 128
FFN_CONV = 3

EPS = 1e-6
NEG = -1e30
FORCE = 1e4

kernel_name = 'nsa_gdn_rglru_hybrid_step'


def rmsnorm(x, g):
    xf = x.astype(jnp.float32)
    y = xf * lax.rsqrt(jnp.mean(xf * xf, axis=-1, keepdims=True) + EPS)
    return (y * g.astype(jnp.float32)).astype(x.dtype)


def l2norm(x):
    xf = x.astype(jnp.float32)
    return xf * lax.rsqrt(jnp.sum(xf * xf, axis=-1, keepdims=True) + EPS)


def rope(x, pos):
    half = ROT_DIM // 2
    inv = ROPE_THETA ** (-jnp.arange(half, dtype=jnp.float32) / half)
    ang = pos.astype(jnp.float32)[:, None] * inv[None, :]
    cos = jnp.cos(ang)[:, None, :]
    sin = jnp.sin(ang)[:, None, :]
    xr = x[..., :ROT_DIM].astype(jnp.float32)
    x1, x2 = xr[..., :half], xr[..., half:]
    rot = jnp.concatenate([x1 * cos - x2 * sin, x2 * cos + x1 * sin], axis=-1).astype(x.dtype)
    return jnp.concatenate([rot, x[..., ROT_DIM:]], axis=-1)


def causal_conv(x, buf, w):
    width = w.shape[0]
    T = x.shape[1]
    xp = jnp.concatenate([buf.astype(x.dtype), x], axis=1)
    y = xp[:, 0:T] * w[0]
    for j in range(1, width):
        y = y + xp[:, j:j + T] * w[j]
    return y, xp[:, T:]


def nsa_project(h, w_in, pos):
    B, T, _ = h.shape
    z = h @ w_in
    cuts = np.cumsum([NSA_Q] + [NSA_KVW] * 6).tolist()
    q, kc, vc, ks, vs, kw, vw, gl = jnp.split(z, cuts, axis=-1)
    q = q.reshape(B, T, NSA_HEADS, NSA_DH)
    kc, vc, ks, vs, kw, vw = [t.reshape(B, T, NSA_KV, NSA_DH) for t in (kc, vc, ks, vs, kw, vw)]
    gates = jax.nn.sigmoid(gl.astype(jnp.float32)).reshape(B, T, 3, NSA_HEADS)
    return q, rope(q, pos), kc, vc, rope(ks, pos), vs, rope(kw, pos), vw, gates


def nsa_compress(k, pe, w):
    B, L = k.shape[:2]
    blk = k.reshape(B, L // L_CMP, L_CMP, NSA_KV, NSA_DH) + pe[:, None, :]
    return jnp.einsum('bclgd,lde->bcge', blk, w)


def nsa_compressed_select(q, kc, vc, pe, wc, qpos):
    B, T = q.shape[:2]
    L = kc.shape[1]
    kcmp = nsa_compress(kc, pe[0], wc[0])
    vcmp = nsa_compress(vc, pe[1], wc[1])
    qg = q.astype(jnp.float32).reshape(B, T, NSA_KV, NSA_HPG, NSA_DH) * NSA_DH ** -0.5
    s = jnp.einsum('btgjd,bcgd->bgjtc', qg, kcmp.astype(jnp.float32))
    n_cmp = L // L_CMP
    c_end = (jnp.arange(n_cmp, dtype=jnp.int32) + 1) * L_CMP - 1
    valid = c_end[None, :] <= qpos[:, None]
    p = jax.nn.softmax(jnp.where(valid, s, NEG), axis=-1) * jnp.any(valid, axis=-1)[:, None].astype(jnp.float32)
    o_c = jnp.einsum('bgjtc,bcgd->btgjd', p.astype(vcmp.dtype), vcmp).reshape(B, T, NSA_HEADS, NSA_DH)
    n_sel = L // L_SEL
    imp = p.sum(axis=2).reshape(B, NSA_KV, T, n_sel, L_SEL // L_CMP).sum(-1)
    blk = jnp.arange(n_sel, dtype=jnp.int32)
    forced = (blk[None, :] == 0) | (blk[None, :] == (qpos // L_SEL)[:, None])
    avail = blk[None, :] * L_SEL <= qpos[:, None]
    imp = jnp.where(forced, FORCE, jnp.where(avail, imp, -1.0))
    _, idx = lax.top_k(imp, min(N_SEL, n_sel))
    return o_c, idx


def nsa_blocks(k):
    B, L = k.shape[:2]
    return k.reshape(B, L // L_SEL, L_SEL, NSA_KV, NSA_DH).transpose(0, 3, 1, 2, 4)


def nsa_local(qg, idx, qpos, ks_blk, vs_blk, kw, vw, kwpos):
    B, Tb = qg.shape[:2]
    qf = qg.astype(jnp.float32) * NSA_DH ** -0.5
    bi = jnp.arange(B)[:, None, None, None]
    gi = jnp.arange(NSA_KV)[None, :, None, None]
    kg = ks_blk[bi, gi, idx]
    vg = vs_blk[bi, gi, idx]
    s = jnp.einsum('btgjd,bgtkld->bgjtkl', qf, kg.astype(jnp.float32))
    kpos = idx[..., None] * L_SEL + jnp.arange(L_SEL, dtype=jnp.int32)
    m = kpos <= qpos[None, None, :, None, None]
    s = jnp.where(m[:, :, None], s, NEG)
    p = jax.nn.softmax(s.reshape(s.shape[:4] + (-1,)), axis=-1)
    o_s = jnp.einsum('bgjtn,bgtnd->btgjd', p.astype(vg.dtype), vg.reshape(B, NSA_KV, Tb, -1, NSA_DH))
    sw = jnp.einsum('btgjd,blgd->bgjtl', qf, kw.astype(jnp.float32))
    dist = qpos[:, None] - kwpos[None, :]
    mw = (dist >= 0) & (dist <= WINDOW) & (kwpos >= 0)[None, :]
    pw = jax.nn.softmax(jnp.where(mw, sw, NEG), axis=-1)
    o_w = jnp.einsum('bgjtl,blgd->btgjd', pw.astype(vw.dtype), vw)
    return o_s, o_w


def nsa_merge(o_c, o_s, o_w, gates, w_out):
    B, T = gates.shape[:2]
    o = (gates[:, :, 0, :, None] * o_c.astype(jnp.float32)
         + gates[:, :, 1, :, None] * o_s.astype(jnp.float32)
         + gates[:, :, 2, :, None] * o_w.astype(jnp.float32))
    return o.reshape(B, T, NSA_Q).astype(w_out.dtype) @ w_out


def nsa_prompt(h, w_in, pe, wc, w_out):
    B, T, _ = h.shape
    pos = jnp.arange(T, dtype=jnp.int32)
    q, q_r, kc, vc, ks, vs, kw, vw, gates = nsa_project(h, w_in, pos)
    o_c, idx = nsa_compressed_select(q, kc, vc, pe, wc, pos)
    ks_blk, vs_blk = nsa_blocks(ks), nsa_blocks(vs)
    qg = q_r.reshape(B, T, NSA_KV, NSA_HPG, NSA_DH)
    kw_pad = jnp.pad(kw, ((0, 0), (WINDOW, 0), (0, 0), (0, 0)))
    vw_pad = jnp.pad(vw, ((0, 0), (WINDOW, 0), (0, 0), (0, 0)))

    def body(i):
        q0 = i * Q_BLOCK
        qb = lax.dynamic_slice_in_dim(qg, q0, Q_BLOCK, axis=1)
        idxb = lax.dynamic_slice_in_dim(idx, q0, Q_BLOCK, axis=2)
        qposb = q0 + jnp.arange(Q_BLOCK, dtype=jnp.int32)
        kwb = lax.dynamic_slice_in_dim(kw_pad, q0, WINDOW + Q_BLOCK, axis=1)
        vwb = lax.dynamic_slice_in_dim(vw_pad, q0, WINDOW + Q_BLOCK, axis=1)
        kwposb = q0 - WINDOW + jnp.arange(WINDOW + Q_BLOCK, dtype=jnp.int32)
        return nsa_local(qb, idxb, qposb, ks_blk, vs_blk, kwb, vwb, kwposb)

    o_s, o_w = lax.map(body, jnp.arange(T // Q_BLOCK, dtype=jnp.int32))
    o_s = jnp.moveaxis(o_s, 0, 1).reshape(B, T, NSA_HEADS, NSA_DH)
    o_w = jnp.moveaxis(o_w, 0, 1).reshape(B, T, NSA_HEADS, NSA_DH)
    y = nsa_merge(o_c, o_s, o_w, gates, w_out)
    wl = min(WINDOW, T)
    new_win = jnp.stack([kw[:, T - wl:], vw[:, T - wl:]], axis=2)
    return y, jnp.stack([kc, vc], axis=2), jnp.stack([ks, vs], axis=2), new_win


def nsa_sample(h, cache_cmp, cache_sel, win_buf, page_table, j, w_in, pe, wc, w_out):
    B, T, _ = h.shape
    pos = PAST_LEN + jnp.arange(T, dtype=jnp.int32)
    q, q_r, kc, vc, ks, vs, kw, vw, gates = nsa_project(h, w_in, pos)
    pad = (-(PAST_LEN + T)) % L_SEL

    def full_seq(cache, k_new, v_new):
        rows = cache[j, page_table].reshape(B, PAST_LEN, 2, NSA_KV, NSA_DH)
        seq = jnp.concatenate([rows, jnp.stack([k_new, v_new], axis=2).astype(rows.dtype)], axis=1)
        seq = jnp.pad(seq, ((0, 0), (0, pad), (0, 0), (0, 0), (0, 0)))
        return seq[:, :, 0], seq[:, :, 1]

    kc_all, vc_all = full_seq(cache_cmp, kc, vc)
    ks_all, vs_all = full_seq(cache_sel, ks, vs)
    o_c, idx = nsa_compressed_select(q, kc_all, vc_all, pe, wc, pos)
    wb = win_buf.shape[1]
    win_all = jnp.concatenate([win_buf, jnp.stack([kw, vw], axis=2).astype(win_buf.dtype)], axis=1)
    kwpos = PAST_LEN - wb + jnp.arange(wb + T, dtype=jnp.int32)
    qg = q_r.reshape(B, T, NSA_KV, NSA_HPG, NSA_DH)
    o_s, o_w = nsa_local(qg, idx, pos, nsa_blocks(ks_all), nsa_blocks(vs_all),
                         win_all[:, :, 0], win_all[:, :, 1], kwpos)
    y = nsa_merge(o_c, o_s.reshape(B, T, NSA_HEADS, NSA_DH), o_w.reshape(B, T, NSA_HEADS, NSA_DH), gates, w_out)
    return y, jnp.stack([kc, vc], axis=2), jnp.stack([ks, vs], axis=2), win_all[:, T:]


def chunk_gated_delta(q, k, v, beta, g, S0):
    B, T, H, DK = q.shape
    DV = v.shape[-1]
    C = min(GDN_CHUNK, T)
    pad = (-T) % C
    n = (T + pad) // C

    def prep(x):
        x = jnp.pad(x.astype(jnp.float32), [(0, 0), (0, pad)] + [(0, 0)] * (x.ndim - 2))
        x = x.reshape((B, n, C) + x.shape[2:])
        return jnp.moveaxis(jnp.moveaxis(x, 3, 2), 1, 0)

    qc, kc, vc, bc, gc = [prep(t) for t in (q, k, v, beta, g)]
    gcum = jnp.cumsum(gc, axis=-1)
    ar = jnp.arange(C)
    incl = ar[:, None] >= ar[None, :]
    strict = (ar[:, None] > ar[None, :]).astype(jnp.float32)
    decay = jnp.exp(jnp.where(incl, gcum[..., :, None] - gcum[..., None, :], NEG))
    kb = kc * bc[..., None]
    lower = jnp.einsum('nbhid,nbhjd->nbhij', kb, kc) * decay * strict
    A = lower + jnp.eye(C, dtype=jnp.float32)
    rhs = jnp.concatenate([vc * bc[..., None], kb * jnp.exp(gcum)[..., None]], axis=-1)
    sol = lax.linalg.triangular_solve(A, rhs, left_side=True, lower=True, unit_diagonal=True)
    u_base, w_dec = sol[..., :DV], sol[..., DV:]
    attn = jnp.einsum('nbhid,nbhjd->nbhij', qc, kc) * decay

    def step(S, xs):
        q_i, k_i, ub, wd, at, g_i = xs
        u = ub - jnp.einsum('bhck,bhkv->bhcv', wd, S)
        o = jnp.einsum('bhck,bhkv->bhcv', q_i * jnp.exp(g_i)[..., None], S) + jnp.einsum('bhij,bhjv->bhiv', at, u)
        g_last = g_i[..., -1]
        S = S * jnp.exp(g_last)[..., None, None] + jnp.einsum(
            'bhck,bhcv->bhkv', k_i * jnp.exp(g_last[..., None] - g_i)[..., None], u)
        return S, o

    S, o = lax.scan(step, S0.astype(jnp.float32), (qc, kc, u_base, w_dec, attn, gcum))
    o = jnp.moveaxis(jnp.moveaxis(o, 0, 1), 2, 3).reshape(B, n * C, H, DV)[:, :T]
    return o, S


def gdn_mixer(h, S0, conv_buf, w_in, conv_w, A_log, dt_bias, norm_g, w_out):
    B, T, _ = h.shape
    z = h @ w_in
    qkv, gate, a_in, b_in = jnp.split(z, [GDN_CONV_CH, GDN_CONV_CH + GDN_VW, GDN_CONV_CH + GDN_VW + GDN_HEADS], axis=-1)
    qkv_c, new_buf = causal_conv(qkv, conv_buf, conv_w)
    qkv_c = jax.nn.silu(qkv_c)
    q, k, v = jnp.split(qkv_c, [GDN_QK, 2 * GDN_QK], axis=-1)
    q = l2norm(q.reshape(B, T, GDN_HEADS, GDN_DK)) * GDN_DK ** -0.5
    k = l2norm(k.reshape(B, T, GDN_HEADS, GDN_DK))
    v = v.reshape(B, T, GDN_HEADS, GDN_DV)
    beta = jax.nn.sigmoid(b_in.astype(jnp.float32))
    g = -jnp.exp(A_log.astype(jnp.float32)) * jax.nn.softplus(a_in.astype(jnp.float32) + dt_bias.astype(jnp.float32))
    o, S = chunk_gated_delta(q, k, v, beta, g, S0)
    o = rmsnorm(o.astype(h.dtype), norm_g) * jax.nn.silu(gate.reshape(B, T, GDN_HEADS, GDN_DV))
    return o.reshape(B, T, GDN_VW) @ w_out, S.astype(S0.dtype), new_buf


def linear_combine(left, right):
    a_l, b_l = left
    a_r, b_r = right
    return a_l * a_r, a_r * b_l + b_r


def lru_mixer(h, h0, conv_buf, w_in, conv_w, conv_b, w_ga, b_ga, w_gx, b_gx, lam, w_out):
    B, T, _ = h.shape
    z = h @ w_in
    gate, xb = jnp.split(z, 2, axis=-1)
    xc, new_buf = causal_conv(xb, conv_buf, conv_w)
    xc = xc + conv_b
    xblk = xc.reshape(B, T, LRU_BLOCKS, LRU_BW)
    r = jax.nn.sigmoid((jnp.einsum('btnc,ncd->btnd', xblk, w_ga).reshape(B, T, D_RNN) + b_ga).astype(jnp.float32))
    i = jax.nn.sigmoid((jnp.einsum('btnc,ncd->btnd', xblk, w_gx).reshape(B, T, D_RNN) + b_gx).astype(jnp.float32))
    log_a = -LRU_C * r * jax.nn.softplus(-lam.astype(jnp.float32))
    a = jnp.exp(log_a)
    b = jnp.sqrt(-jnp.expm1(2.0 * log_a)) * i * xc.astype(jnp.float32)
    b = b.at[:, 0].add(a[:, 0] * h0.astype(jnp.float32))
    _, hs = lax.associative_scan(linear_combine, (a, b), axis=1)
    y = hs.astype(h.dtype) * jax.nn.gelu(gate)
    return y @ w_out, hs[:, -1].astype(h0.dtype), new_buf


def conv_ffn(h, buf, w_in, conv_w, w_out):
    u, new_buf = causal_conv(h @ w_in, buf, conv_w)
    val, gt = jnp.split(u, 2, axis=-1)
    return (val * jax.nn.silu(gt)) @ w_out, new_buf


def trunk(x, nsa_call, gdn_S, gdn_conv, lru_h, lru_conv, ffn_conv,
          norm_mix_g, norm_ffn_g, norm_final_g,
          gdn_w_in, gdn_conv_w, gdn_A_log, gdn_dt_bias, gdn_norm_g, gdn_w_out,
          lru_w_in, lru_conv_w, lru_conv_b, lru_w_ga, lru_b_ga, lru_w_gx, lru_b_gx, lru_lambda, lru_w_out,
          ffn_w_in, ffn_conv_w, ffn_w_out):
    nsa_new, gdn_new, lru_new, ffn_new = [], [], [], []
    for li in range(DEPTH):
        j = li // N_MIXERS
        kind = li % N_MIXERS
        hn = rmsnorm(x, norm_mix_g[li])
        if kind == 0:
            y, s_cmp, s_sel, s_win = nsa_call(j, hn)
            nsa_new.append((s_cmp, s_sel, s_win))
        elif kind == 1:
            y, S, cb = gdn_mixer(hn, gdn_S[j], gdn_conv[j], gdn_w_in[j], gdn_conv_w[j], gdn_A_log[j],
                                 gdn_dt_bias[j], gdn_norm_g[j], gdn_w_out[j])
            gdn_new.append((S, cb))
        else:
            y, hl, cb = lru_mixer(hn, lru_h[j], lru_conv[j], lru_w_in[j], lru_conv_w[j], lru_conv_b[j],
                                  lru_w_ga[j], lru_b_ga[j], lru_w_gx[j], lru_b_gx[j], lru_lambda[j], lru_w_out[j])
            lru_new.append((hl, cb))
        x = x + y
        y, fb = conv_ffn(rmsnorm(x, norm_ffn_g[li]), ffn_conv[li], ffn_w_in[li], ffn_conv_w[li], ffn_w_out[li])
        ffn_new.append(fb)
        x = x + y
    return rmsnorm(x, norm_final_g), nsa_new, gdn_new, lru_new, ffn_new


def stack_states(entries, n):
    return [jnp.stack([e[k] for e in entries]) for k in range(n)]


def setup_inputs(seed: int = 0) -> dict:
    key = jax.random.key(seed)
    ks = iter(jax.random.split(key, 48))

    def nrm(shape, scale):
        return jax.random.normal(next(ks), shape, jnp.float32) * scale

    n_pages = PAST_LEN // PAGE_SIZE
    n_used = DEC_BATCH * n_pages
    n_phys = n_used + n_used // 4
    w_buf = min(WINDOW, PAST_LEN)
    x_prompt = nrm((BATCH, SEQ, D_MODEL), 1.0)
    x_sample = nrm((DEC_BATCH, DEC_SEQ, D_MODEL), 1.0)
    cache_nsa_cmp = nrm((N_NSA, n_phys, PAGE_SIZE, 2, NSA_KV, NSA_DH), 1.0)
    cache_nsa_sel = nrm((N_NSA, n_phys, PAGE_SIZE, 2, NSA_KV, NSA_DH), 1.0)
    state_nsa_win = nrm((N_NSA, DEC_BATCH, w_buf, 2, NSA_KV, NSA_DH), 1.0)
    state_gdn_S = nrm((N_GDN, DEC_BATCH, GDN_HEADS, GDN_DK, GDN_DV), 0.1)
    state_gdn_conv = nrm((N_GDN, DEC_BATCH, GDN_CONV - 1, GDN_CONV_CH), 1.0)
    state_lru_h = nrm((N_LRU, DEC_BATCH, D_RNN), 0.5)
    state_lru_conv = nrm((N_LRU, DEC_BATCH, LRU_CONV - 1, D_RNN), 1.0)
    state_ffn_conv = nrm((DEPTH, DEC_BATCH, FFN_CONV - 1, 2 * D_FF), 1.0)
    page_table = jax.random.permutation(next(ks), n_phys)[:n_used].reshape(DEC_BATCH, n_pages).astype(jnp.int32)
    norm_mix_g = 1.0 + nrm((DEPTH, D_MODEL), 0.1)
    norm_ffn_g = 1.0 + nrm((DEPTH, D_MODEL), 0.1)
    norm_final_g = 1.0 + nrm((D_MODEL,), 0.1)
    nsa_w_in = nrm((N_NSA, D_MODEL, NSA_IN), D_MODEL ** -0.5)
    nsa_cmp_pe = nrm((N_NSA, 2, L_CMP, NSA_DH), 0.1)
    nsa_cmp_w = nrm((N_NSA, 2, L_CMP, NSA_DH, NSA_DH), (L_CMP * NSA_DH) ** -0.5)
    nsa_w_out = nrm((N_NSA, NSA_Q, D_MODEL), NSA_Q ** -0.5)
    gdn_w_in = nrm((N_GDN, D_MODEL, GDN_IN), D_MODEL ** -0.5)
    gdn_conv_w = nrm((N_GDN, GDN_CONV, GDN_CONV_CH), GDN_CONV ** -0.5)
    gdn_A_log = jnp.log(jax.random.uniform(next(ks), (N_GDN, GDN_HEADS), jnp.float32, 1.0, 16.0))
    dt = jnp.exp(jax.random.uniform(next(ks), (N_GDN, GDN_HEADS), jnp.float32, np.log(1e-3), np.log(1e-1)))
    gdn_dt_bias = dt + jnp.log(-jnp.expm1(-dt))
    gdn_norm_g = 1.0 + nrm((N_GDN, GDN_DV), 0.1)
    gdn_w_out = nrm((N_GDN, GDN_VW, D_MODEL), GDN_VW ** -0.5)
    lru_w_in = nrm((N_LRU, D_MODEL, 2 * D_RNN), D_MODEL ** -0.5)
    lru_conv_w = nrm((N_LRU, LRU_CONV, D_RNN), LRU_CONV ** -0.5)
    lru_conv_b = nrm((N_LRU, D_RNN), 0.01)
    lru_w_ga = nrm((N_LRU, LRU_BLOCKS, LRU_BW, LRU_BW), LRU_BW ** -0.5)
    lru_b_ga = nrm((N_LRU, D_RNN), 0.01)
    lru_w_gx = nrm((N_LRU, LRU_BLOCKS, LRU_BW, LRU_BW), LRU_BW ** -0.5)
    lru_b_gx = nrm((N_LRU, D_RNN), 0.01)
    s8 = jax.random.uniform(next(ks), (N_LRU, D_RNN), jnp.float32, 0.9, 0.999) ** (1.0 / LRU_C)
    lru_lambda = jnp.log(s8) - jnp.log1p(-s8)
    lru_w_out = nrm((N_LRU, D_RNN, D_MODEL), D_RNN ** -0.5)
    ffn_w_in = nrm((DEPTH, D_MODEL, 2 * D_FF), D_MODEL ** -0.5)
    ffn_conv_w = nrm((DEPTH, FFN_CONV, 2 * D_FF), FFN_CONV ** -0.5)
    ffn_w_out = nrm((DEPTH, D_FF, D_MODEL), D_FF ** -0.5)
    return {'x_prompt': x_prompt, 'x_sample': x_sample,
            'cache_nsa_cmp': cache_nsa_cmp, 'cache_nsa_sel': cache_nsa_sel, 'state_nsa_win': state_nsa_win,
            'state_gdn_S': state_gdn_S, 'state_gdn_conv': state_gdn_conv,
            'state_lru_h': state_lru_h, 'state_lru_conv': state_lru_conv, 'state_ffn_conv': state_ffn_conv,
            'page_table': page_table,
            'norm_mix_g': norm_mix_g, 'norm_ffn_g': norm_ffn_g, 'norm_final_g': norm_final_g,
            'nsa_w_in': nsa_w_in, 'nsa_cmp_pe': nsa_cmp_pe, 'nsa_cmp_w': nsa_cmp_w, 'nsa_w_out': nsa_w_out,
            'gdn_w_in': gdn_w_in, 'gdn_conv_w': gdn_conv_w, 'gdn_A_log': gdn_A_log, 'gdn_dt_bias': gdn_dt_bias,
            'gdn_norm_g': gdn_norm_g, 'gdn_w_out': gdn_w_out,
            'lru_w_in': lru_w_in, 'lru_conv_w': lru_conv_w, 'lru_conv_b': lru_conv_b,
            'lru_w_ga': lru_w_ga, 'lru_b_ga': lru_b_ga, 'lru_w_gx': lru_w_gx, 'lru_b_gx': lru_b_gx,
            'lru_lambda': lru_lambda, 'lru_w_out': lru_w_out,
            'ffn_w_in': ffn_w_in, 'ffn_conv_w': ffn_conv_w, 'ffn_w_out': ffn_w_out}


def reference(x_prompt, x_sample, cache_nsa_cmp, cache_nsa_sel, state_nsa_win, state_gdn_S, state_gdn_conv,
              state_lru_h, state_lru_conv, state_ffn_conv, page_table,
              norm_mix_g, norm_ffn_g, norm_final_g,
              nsa_w_in, nsa_cmp_pe, nsa_cmp_w, nsa_w_out,
              gdn_w_in, gdn_conv_w, gdn_A_log, gdn_dt_bias, gdn_norm_g, gdn_w_out,
              lru_w_in, lru_conv_w, lru_conv_b, lru_w_ga, lru_b_ga, lru_w_gx, lru_b_gx, lru_lambda, lru_w_out,
              ffn_w_in, ffn_conv_w, ffn_w_out):
    shared = (norm_mix_g, norm_ffn_g, norm_final_g,
              gdn_w_in, gdn_conv_w, gdn_A_log, gdn_dt_bias, gdn_norm_g, gdn_w_out,
              lru_w_in, lru_conv_w, lru_conv_b, lru_w_ga, lru_b_ga, lru_w_gx, lru_b_gx, lru_lambda, lru_w_out,
              ffn_w_in, ffn_conv_w, ffn_w_out)

    def nsa_prompt_call(j, hn):
        return nsa_prompt(hn, nsa_w_in[j], nsa_cmp_pe[j], nsa_cmp_w[j], nsa_w_out[j])

    def nsa_sample_call(j, hn):
        return nsa_sample(hn, cache_nsa_cmp, cache_nsa_sel, state_nsa_win[j], page_table, j,
                          nsa_w_in[j], nsa_cmp_pe[j], nsa_cmp_w[j], nsa_w_out[j])

    dt = x_prompt.dtype
    B = x_prompt.shape[0]
    y_prompt, p_nsa, p_gdn, p_lru, p_ffn = trunk(
        x_prompt, nsa_prompt_call,
        jnp.zeros((N_GDN, B, GDN_HEADS, GDN_DK, GDN_DV), dt),
        jnp.zeros((N_GDN, B, GDN_CONV - 1, GDN_CONV_CH), dt),
        jnp.zeros((N_LRU, B, D_RNN), dt),
        jnp.zeros((N_LRU, B, LRU_CONV - 1, D_RNN), dt),
        jnp.zeros((DEPTH, B, FFN_CONV - 1, 2 * D_FF), dt),
        *shared)
    y_sample, s_nsa, s_gdn, s_lru, s_ffn = trunk(
        x_sample, nsa_sample_call, state_gdn_S, state_gdn_conv, state_lru_h, state_lru_conv, state_ffn_conv,
        *shared)
    p_cmp, p_sel, p_win = stack_states(p_nsa, 3)
    s_cmp, s_sel, s_win = stack_states(s_nsa, 3)
    p_gS, p_gconv = stack_states(p_gdn, 2)
    s_gS, s_gconv = stack_states(s_gdn, 2)
    p_lh, p_lconv = stack_states(p_lru, 2)
    s_lh, s_lconv = stack_states(s_lru, 2)
    p_fconv = jnp.stack(p_ffn)
    s_fconv = jnp.stack(s_ffn)
    return (y_prompt, y_sample, p_cmp, s_cmp, p_sel, s_sel, p_win, s_win,
            p_gS, s_gS, p_gconv, s_gconv, p_lh, s_lh, p_lconv, s_lconv, p_fconv, s_fconv)
```

```python
import functools

import jax
import jax.numpy as jnp
from jax import lax
from jax.experimental import pallas as pl
from jax.experimental.pallas import tpu as pltpu

F32 = jnp.float32
BF16 = jnp.bfloat16
I32 = jnp.int32

D_MODEL = 1024
DEPTH = 4
PAST_LEN = 8192
PAGE_SIZE = 128
N_MIXERS = 3

NSA_DH = 64
NSA_HEADS = 16
NSA_HPG = 4
NSA_KV = 4
NSA_Q = NSA_HEADS * NSA_DH
NSA_KVW = NSA_KV * NSA_DH
ROT_DIM = NSA_DH // 4
ROPE_THETA = 500000.0
L_CMP = 32
L_SEL = 64
N_SEL = 16
WINDOW = 512
_LOG_SEL = 6
NSA_SCALE = NSA_DH ** -0.5

GDN_DK = 128
GDN_DV = 128
GDN_HEADS = 8
GDN_QK = GDN_HEADS * GDN_DK
GDN_VW = GDN_HEADS * GDN_DV
GDN_CONV = 4
GDN_CONV_CH = 2 * GDN_QK + GDN_VW
GDN_CHUNK = 64

D_RNN = D_MODEL
LRU_BLOCKS = 8
LRU_BW = D_RNN // LRU_BLOCKS
LRU_CONV = 4
LRU_C = 8.0

D_FF = 2816
FFN_CONV = 3

EPS = 1e-6
NEG = -1e30
FORCE = 1e4

LANES = 128
SUBLANES = 8
MXU_DIM = 256
VMEM_LIMIT = 56 * 1024 * 1024


def _cparams(*sem):
    return pltpu.CompilerParams(dimension_semantics=sem, vmem_limit_bytes=VMEM_LIMIT)


def _resident(shape):
    nd = len(shape)
    return pl.BlockSpec(shape, lambda *_: (0,) * nd, pipeline_mode=pl.Buffered(1))


def _rmsnorm(x, g):
    return x * lax.rsqrt(jnp.mean(x * x, axis=-1, keepdims=True) + EPS) * g


def _dot(a, b):
    return jnp.dot(a, b, preferred_element_type=F32)


def _dot_nt(a, b):
    return lax.dot_general(a, b, (((1,), (1,)), ((), ())), preferred_element_type=F32)


def _dot_tn(a, b):
    return lax.dot_general(a, b, (((0,), (0,)), ((), ())), preferred_element_type=F32)


def _dot_f32(a, b):
    return jnp.dot(a, b, precision=lax.Precision.HIGHEST, preferred_element_type=F32)


def _iota(shape, axis):
    return lax.broadcasted_iota(I32, shape, axis)


def _rope_tables(pos):
    half = ROT_DIM // 2
    inv = ROPE_THETA ** (-jnp.arange(half, dtype=F32) / half)
    ang = pos.astype(F32)[:, None] * inv[None, :]
    lane = jnp.arange(LANES) % NSA_DH
    c = jnp.cos(ang)[:, lane % half]
    s = jnp.sin(ang)[:, lane % half]
    cos = jnp.where(lane < ROT_DIM, c, 1.0)
    sin_a = jnp.where((lane >= half) & (lane < ROT_DIM), s, 0.0)
    sin_b = jnp.where(lane < half, -s, 0.0)
    return cos.astype(F32), sin_a.astype(F32), sin_b.astype(F32)


def _rope(x, cos, sin_a, sin_b):
    half = ROT_DIM // 2
    outs = []
    for c in range(x.shape[1] // LANES):
        xc = x[:, c * LANES:(c + 1) * LANES]
        outs.append(xc * cos + pltpu.roll(xc, half, 1) * sin_a + pltpu.roll(xc, LANES - half, 1) * sin_b)
    return outs[0] if len(outs) == 1 else jnp.concatenate(outs, axis=1)


_NSA_W_COLS = NSA_Q + 6 * NSA_KVW + LANES


def _nsa_proj_kernel(x_ref, g_ref, w_ref, cos_ref, sa_ref, sb_ref,
                     q_ref, qr_ref, cmp_ref, sel_ref, win_ref, gate_ref, kvb_ref):
    xn = _rmsnorm(x_ref[...], g_ref[...]).astype(BF16)
    cos, sa, sb = cos_ref[...], sa_ref[...], sb_ref[...]

    def mm(c0, c1):
        return _dot(xn, w_ref[:, c0:c1])

    zq = mm(0, NSA_Q)
    q_ref[...] = (zq * NSA_SCALE).astype(BF16)
    qr_ref[...] = (_rope(zq, cos, sa, sb) * NSA_SCALE).astype(BF16)
    c0 = NSA_Q
    cmp_ref[...] = mm(c0, c0 + 2 * NSA_KVW)
    c0 += 2 * NSA_KVW
    for kind, o_ref in ((0, sel_ref), (1, win_ref)):
        z = mm(c0, c0 + 2 * NSA_KVW)
        kr = _rope(z[:, :NSA_KVW], cos, sa, sb)
        v = z[:, NSA_KVW:]
        o_ref[:, :NSA_KVW] = kr
        o_ref[:, NSA_KVW:] = v
        for g in range(NSA_KV):
            kvb_ref[2 * kind, g] = kr[:, g * NSA_DH:(g + 1) * NSA_DH].astype(BF16)
            kvb_ref[2 * kind + 1, g] = v[:, g * NSA_DH:(g + 1) * NSA_DH].astype(BF16)
        c0 += 2 * NSA_KVW
    gate_ref[...] = jax.nn.sigmoid(mm(c0, c0 + LANES))


def _nsa_proj(x, g, w, tables, n_batch, tm):
    n = x.shape[0]
    t_len = n // n_batch
    tpb = t_len // tm
    row = lambda i: (i, 0)
    tab = pl.BlockSpec((tm, LANES), lambda i: (i % tpb, 0))
    return pl.pallas_call(
        _nsa_proj_kernel,
        grid=(n // tm,),
        in_specs=[pl.BlockSpec((tm, D_MODEL), row), _resident((1, D_MODEL)), _resident((D_MODEL, _NSA_W_COLS)),
                  tab, tab, tab],
        out_specs=[pl.BlockSpec((tm, NSA_Q), row), pl.BlockSpec((tm, NSA_Q), row),
                   pl.BlockSpec((tm, 2 * NSA_KVW), row), pl.BlockSpec((tm, 2 * NSA_KVW), row),
                   pl.BlockSpec((tm, 2 * NSA_KVW), row), pl.BlockSpec((tm, LANES), row),
                   pl.BlockSpec((None, 4, NSA_KV, tm, NSA_DH), lambda i: (i // tpb, 0, 0, i % tpb, 0))],
        out_shape=[jax.ShapeDtypeStruct((n, NSA_Q), BF16), jax.ShapeDtypeStruct((n, NSA_Q), BF16),
                   jax.ShapeDtypeStruct((n, 2 * NSA_KVW), F32), jax.ShapeDtypeStruct((n, 2 * NSA_KVW), F32),
                   jax.ShapeDtypeStruct((n, 2 * NSA_KVW), F32), jax.ShapeDtypeStruct((n, LANES), F32),
                   jax.ShapeDtypeStruct((n_batch, 4, NSA_KV, t_len, NSA_DH), BF16)],
        compiler_params=_cparams("parallel"),
        name="nsa_proj",
    )(x, g, w, *tables)


def _nsa_w_in_padded(w_in):
    pad = _NSA_W_COLS - w_in.shape[1]
    return jnp.pad(w_in, ((0, 0), (0, pad))).astype(BF16)


def _cmp_weights(cmp_w, cmp_pe):
    z = jnp.zeros_like(cmp_w)
    bd = jnp.concatenate([jnp.concatenate([cmp_w, z], axis=-1), jnp.concatenate([z, cmp_w], axis=-1)], axis=-2)
    return bd.astype(BF16), jnp.concatenate([cmp_pe, cmp_pe], axis=-1)


def _compress_planes(load_rows, bd_ref, pe_ref, o_ref, n_rows_out):
    for plane in range(4):
        kv = plane // 2
        acc = jnp.zeros((n_rows_out, LANES), F32)
        bias = jnp.zeros((SUBLANES, LANES), F32)
        for l in range(L_CMP):
            w = bd_ref[kv, l]
            acc = acc + _dot(load_rows(plane, l).astype(BF16), w)
            pe = jnp.broadcast_to(pe_ref[kv, l:l + 1, :], (SUBLANES, LANES)).astype(BF16)
            bias = bias + _dot(pe, w)
        out = acc + bias[0:1, :]
        g0 = 2 * (plane % 2)
        o_ref[kv, g0] = out[:, :NSA_DH]
        o_ref[kv, g0 + 1] = out[:, NSA_DH:]


def _nsa_compress_prompt_kernel(x0_ref, x1_ref, x2_ref, x3_ref, bd_ref, pe_ref, o_ref, *, n_blk, n_pad):
    planes = (x0_ref, x1_ref, x2_ref, x3_ref)
    if n_pad > n_blk:
        o_ref[...] = jnp.zeros(o_ref.shape, F32)

    def load_rows(plane, l):
        return planes[plane][pl.ds(l, n_blk, stride=L_CMP), :]

    if n_pad == n_blk:
        _compress_planes(load_rows, bd_ref, pe_ref, o_ref, n_blk)
    else:
        _compress_planes(load_rows, bd_ref, pe_ref, o_ref.at[:, :, 0:n_blk, :], n_blk)


def _nsa_compress_prompt(cmp_rows, bd, pe2, n_batch):
    n = cmp_rows.shape[0]
    t_len = n // n_batch
    n_blk = t_len // L_CMP
    n_pad = max(n_blk, LANES)
    planes = [pl.BlockSpec((t_len, LANES), functools.partial(lambda b, c: (b, c), c=c)) for c in range(4)]
    return pl.pallas_call(
        functools.partial(_nsa_compress_prompt_kernel, n_blk=n_blk, n_pad=n_pad),
        grid=(n_batch,),
        in_specs=planes + [_resident(bd.shape), _resident(pe2.shape)],
        out_specs=pl.BlockSpec((None, 2, NSA_KV, n_pad, NSA_DH), lambda b: (b, 0, 0, 0, 0)),
        out_shape=jax.ShapeDtypeStruct((n_batch, 2, NSA_KV, n_pad, NSA_DH), F32),
        compiler_params=_cparams("parallel"),
        name="nsa_compress_prompt",
    )(cmp_rows, cmp_rows, cmp_rows, cmp_rows, bd, pe2)


def _stack_heads(x):
    return jnp.concatenate([x[:, j * NSA_DH:(j + 1) * NSA_DH] for j in range(NSA_HPG)], axis=0)


def _softmax_rows(s):
    m = jnp.max(s, axis=-1, keepdims=True)
    e = jnp.exp(s - m)
    return e / jnp.sum(e, axis=-1, keepdims=True)


def _select_blocks(imp_t, t0, rank_scr):
    n_sel_pad = LANES // 2
    tq = imp_t.shape[1]
    rank_scr[...] = imp_t
    pair = rank_scr[pl.ds(0, n_sel_pad, stride=2), :] + rank_scr[pl.ds(1, n_sel_pad, stride=2), :]
    blk = _iota((n_sel_pad, tq), 0)
    tpos = t0 + _iota((n_sel_pad, tq), 1)
    forced = (blk == 0) | (blk == (tpos >> _LOG_SEL))
    avail = blk * L_SEL <= tpos
    val = jnp.where(forced, FORCE, jnp.where(avail, pair, -1.0))
    rank_scr[0:n_sel_pad, :] = val
    cnt = jnp.zeros((n_sel_pad, tq), F32)
    for s in range(n_sel_pad):
        row = jnp.broadcast_to(rank_scr[pl.ds(s, 1), :], (n_sel_pad, tq))
        ahead = (row > val) | ((row == val) & (blk > s))
        cnt = cnt + jnp.where(ahead, 1.0, 0.0)
    sel = jnp.where(cnt < float(N_SEL), 1.0, 0.0)
    return jnp.concatenate([sel, jnp.zeros((LANES - n_sel_pad, tq), F32)], axis=0)


def _nsa_attn_kernel(q_ref, qr_ref, gate_ref, kvc_ref, kvb_ref, o_ref, rank_scr, *, tq, tk):
    g = pl.program_id(1)
    t0 = pl.program_id(2) * tq
    rows = NSA_HPG * tq
    q4 = _stack_heads(q_ref[...])
    qr4 = _stack_heads(qr_ref[...])

    n_cmp = kvc_ref.shape[1]
    s = _dot_nt(q4, kvc_ref[0].astype(BF16))
    tpos_c = t0 + (_iota((rows, n_cmp), 0) & (tq - 1))
    c_end = (_iota((rows, n_cmp), 1) + 1) * L_CMP - 1
    p = _softmax_rows(jnp.where(c_end <= tpos_c, s, NEG))
    p = p * jnp.where(tpos_c >= L_CMP - 1, 1.0, 0.0)
    o_c = _dot(p.astype(BF16), kvc_ref[1].astype(BF16))
    imp = p[0:tq] + p[tq:2 * tq] + p[2 * tq:3 * tq] + p[3 * tq:4 * tq]

    sel_t = _select_blocks(imp.T, t0, rank_scr)
    sel = jnp.where(sel_t.T > 0.5, 1.0, 0.0).astype(BF16)

    def sel_tile(kt, carry):
        m, l, acc = carry
        k0 = pl.multiple_of(kt * tk, tk)
        k = kvb_ref[0, pl.ds(k0, tk), :]
        v = kvb_ref[1, pl.ds(k0, tk), :]
        sc = _dot_nt(qr4, k)
        expand = jnp.where(_iota((LANES, tk), 0) == ((k0 + _iota((LANES, tk), 1)) >> _LOG_SEL), 1.0, 0.0).astype(BF16)
        picked = _dot(sel, expand)
        kpos = k0 + _iota((tq, tk), 1)
        tpos = t0 + _iota((tq, tk), 0)
        bias = jnp.where((picked > 0.5) & (kpos <= tpos), 0.0, NEG)
        sc = (sc.reshape(NSA_HPG, tq, tk) + bias[None]).reshape(rows, tk)
        m_new = jnp.maximum(m, jnp.max(sc, axis=-1, keepdims=True))
        alpha = jnp.exp(m - m_new)
        pe = jnp.exp(sc - m_new)
        l = alpha * l + jnp.sum(pe, axis=-1, keepdims=True)
        acc = alpha * acc + _dot(pe.astype(BF16), v)
        return m_new, l, acc

    n_kt = (t0 + tq + tk - 1) // tk
    init = (jnp.full((rows, 1), NEG, F32), jnp.zeros((rows, 1), F32), jnp.zeros((rows, NSA_DH), F32))
    _, l_s, acc_s = lax.fori_loop(0, n_kt, sel_tile, init)
    o_s = acc_s / l_s

    wk = WINDOW + tq
    w0 = pl.multiple_of(jnp.maximum(t0 - WINDOW, 0), tq)
    kw = kvb_ref[2, pl.ds(w0, wk), :]
    vw = kvb_ref[3, pl.ds(w0, wk), :]
    sw = _dot_nt(qr4, kw)
    dist = (t0 + (_iota((rows, wk), 0) & (tq - 1))) - (w0 + _iota((rows, wk), 1))
    pw = _softmax_rows(jnp.where((dist >= 0) & (dist <= WINDOW), sw, NEG))
    o_w = _dot(pw.astype(BF16), vw)

    gates = gate_ref[...]
    lane = _iota((tq, LANES), 1)
    outs = []
    for j in range(NSA_HPG):
        o = jnp.zeros((tq, NSA_DH), F32)
        for br, ob in enumerate((o_c, o_s, o_w)):
            col = br * NSA_HEADS + g * NSA_HPG + j
            gcol = jnp.sum(jnp.where(lane == col, gates, 0.0), axis=-1, keepdims=True)
            o = o + gcol * ob[j * tq:(j + 1) * tq]
        outs.append(o)
    o_ref[...] = jnp.concatenate(outs, axis=1).astype(BF16)


def _nsa_attn_prompt(q, qr, gates, kvc, kvb, n_batch, tq=128, tk=512):
    n = q.shape[0]
    t_len = n // n_batch
    nq = t_len // tq
    tk = min(tk, t_len)
    assert kvc.shape[3] == LANES and t_len >= WINDOW + tq and t_len % tk == 0
    qspec = pl.BlockSpec((tq, NSA_HPG * NSA_DH), lambda b, g, i: (b * nq + i, g))
    return pl.pallas_call(
        functools.partial(_nsa_attn_kernel, tq=tq, tk=tk),
        grid=(n_batch, NSA_KV, nq),
        in_specs=[qspec, qspec,
                  pl.BlockSpec((tq, LANES), lambda b, g, i: (b * nq + i, 0)),
                  pl.BlockSpec((None, 2, None, LANES, NSA_DH), lambda b, g, i: (b, 0, g, 0, 0)),
                  pl.BlockSpec((None, 4, None, t_len, NSA_DH), lambda b, g, i: (b, 0, g, 0, 0))],
        out_specs=qspec,
        out_shape=jax.ShapeDtypeStruct((n, NSA_Q), BF16),
        scratch_shapes=[pltpu.VMEM((LANES, tq), F32)],
        compiler_params=_cparams("parallel", "parallel", "arbitrary"),
        name="nsa_attn_prompt",
    )(q, qr, gates, kvc, kvb)


def _matmul_residual_kernel(a_ref, w_ref, x_ref, o_ref):
    o_ref[...] = x_ref[...] + _dot(a_ref[...].astype(BF16), w_ref[...])


def _matmul_residual(a, w, x, tm):
    n, k = a.shape
    row = lambda i: (i, 0)
    return pl.pallas_call(
        _matmul_residual_kernel,
        grid=(n // tm,),
        in_specs=[pl.BlockSpec((tm, k), row), _resident(w.shape), pl.BlockSpec((tm, D_MODEL), row)],
        out_specs=pl.BlockSpec((tm, D_MODEL), row),
        out_shape=jax.ShapeDtypeStruct((n, D_MODEL), F32),
        compiler_params=_cparams("parallel"),
        name="matmul_residual",
    )(a, w, x)


def _nsa_prompt_layer(x, n_batch, norm_g, w_in, cmp_pe, cmp_w, w_out, tables):
    tm = min(512, x.shape[0] // n_batch)
    q, qr, cmp_rows, sel_rows, win_rows, gates, kvb = _nsa_proj(
        x, norm_g.reshape(1, D_MODEL), _nsa_w_in_padded(w_in), tables, n_batch, tm)
    bd, pe2 = _cmp_weights(cmp_w, cmp_pe)
    kvc = _nsa_compress_prompt(cmp_rows, bd, pe2, n_batch)
    o = _nsa_attn_prompt(q, qr, gates, kvc, kvb, n_batch)
    y = _matmul_residual(o, w_out.astype(BF16), x, tm)
    return y, cmp_rows, sel_rows, win_rows


_FFN_CW = MXU_DIM


def _ffn_kernel(x_ref, g_ref, win_ref, cw_ref, wout_ref, buf_ref, y_ref, st_ref, carry_scr, uv_scr, ug_scr,
                *, tm, tpb):
    ti = pl.program_id(0) % tpb
    x = x_ref[...]
    xn = _rmsnorm(x, g_ref[...]).astype(BF16)

    @pl.when(ti == 0)
    def _():
        carry_scr[SUBLANES - (FFN_CONV - 1):SUBLANES, :] = buf_ref[...]

    acc = jnp.zeros((tm, D_MODEL), F32)
    for c in range(D_FF // _FFN_CW):
        conv = []
        for part, scr in ((0, uv_scr), (1, ug_scr)):
            c0 = part * D_FF + c * _FFN_CW
            u = _dot(xn, win_ref[:, c0:c0 + _FFN_CW])
            scr[0:SUBLANES, :] = carry_scr[:, c0:c0 + _FFN_CW]
            scr[SUBLANES:SUBLANES + tm, :] = u
            carry_scr[:, c0:c0 + _FFN_CW] = u[tm - SUBLANES:tm, :]
            w = cw_ref[:, c0:c0 + _FFN_CW]
            conv.append(scr[SUBLANES - 2:SUBLANES - 2 + tm, :] * w[0:1] + scr[SUBLANES - 1:SUBLANES - 1 + tm, :] * w[1:2]
                        + u * w[2:3])
        act = conv[0] * jax.nn.silu(conv[1])
        acc = acc + _dot(act.astype(BF16), wout_ref[c * _FFN_CW:(c + 1) * _FFN_CW, :])
    y_ref[...] = x + acc

    @pl.when(ti == tpb - 1)
    def _():
        st_ref[...] = carry_scr[SUBLANES - (FFN_CONV - 1):SUBLANES, :]


def _ffn_prompt(x, n_batch, norm_g, w_in, conv_w, w_out, buf, tm=512):
    n = x.shape[0]
    tpb = n // n_batch // tm
    row = lambda i: (i, 0)
    st = pl.BlockSpec((None, FFN_CONV - 1, 2 * D_FF), lambda i: (i // tpb, 0, 0))
    return pl.pallas_call(
        functools.partial(_ffn_kernel, tm=tm, tpb=tpb),
        grid=(n // tm,),
        in_specs=[pl.BlockSpec((tm, D_MODEL), row), _resident((1, D_MODEL)), _resident(w_in.shape),
                  _resident(conv_w.shape), _resident(w_out.shape), st],
        out_specs=[pl.BlockSpec((tm, D_MODEL), row), st],
        out_shape=[jax.ShapeDtypeStruct((n, D_MODEL), F32),
                   jax.ShapeDtypeStruct((n_batch, FFN_CONV - 1, 2 * D_FF), F32)],
        scratch_shapes=[pltpu.VMEM((SUBLANES, 2 * D_FF), F32), pltpu.VMEM((SUBLANES + tm, _FFN_CW), F32),
                        pltpu.VMEM((SUBLANES + tm, _FFN_CW), F32)],
        compiler_params=_cparams("arbitrary"),
        name="ffn_prompt",
    )(x, norm_g.reshape(1, D_MODEL), w_in, conv_w, w_out, buf)


_GDN_W_COLS = GDN_CONV_CH + GDN_VW + LANES
_GDN_CW = MXU_DIM


def _softplus(x):
    return jnp.maximum(x, 0.0) + jnp.log1p(jnp.exp(-jnp.abs(x)))


def _gdn_w_in_padded(w_in):
    return jnp.pad(w_in, ((0, 0), (0, _GDN_W_COLS - w_in.shape[1]))).astype(BF16)


def _lane_row(v):
    return jnp.pad(v.astype(F32), (0, LANES - v.shape[0])).reshape(1, LANES)


def _gdn_post_conv(y, c0):
    y = jax.nn.silu(y)
    kind = c0 // GDN_QK
    if kind == 2:
        return y
    outs = []
    for hh in range(_GDN_CW // GDN_DK):
        seg = y[:, hh * GDN_DK:(hh + 1) * GDN_DK]
        seg = seg * lax.rsqrt(jnp.sum(seg * seg, axis=-1, keepdims=True) + EPS)
        outs.append(seg * GDN_DK ** -0.5 if kind == 0 else seg)
    return jnp.concatenate(outs, axis=1)


def _gdn_gates(ab, alog, dtb):
    lane = _iota(ab.shape, 1)
    gval = -jnp.exp(alog) * _softplus(ab + dtb)
    return jnp.where(lane < GDN_HEADS, gval, jax.nn.sigmoid(ab))


def _gdn_proj_kernel(x_ref, g_ref, w_ref, cw_ref, alog_ref, dtb_ref, buf_ref,
                     q_ref, k_ref, v_ref, gate_ref, gb_ref, st_ref, carry_scr, u_scr, *, tm, tpb):
    ti = pl.program_id(0) % tpb
    xn = _rmsnorm(x_ref[...], g_ref[...]).astype(BF16)
    keep = GDN_CONV - 1

    @pl.when(ti == 0)
    def _():
        carry_scr[SUBLANES - keep:SUBLANES, :] = buf_ref[...]

    outs = (q_ref, k_ref, v_ref)
    for c in range(GDN_CONV_CH // _GDN_CW):
        c0 = c * _GDN_CW
        u = _dot(xn, w_ref[:, c0:c0 + _GDN_CW])
        u_scr[0:SUBLANES, :] = carry_scr[:, c0:c0 + _GDN_CW]
        u_scr[SUBLANES:SUBLANES + tm, :] = u
        carry_scr[:, c0:c0 + _GDN_CW] = u[tm - SUBLANES:tm, :]
        w = cw_ref[:, c0:c0 + _GDN_CW]
        y = u_scr[SUBLANES - 3:SUBLANES - 3 + tm, :] * w[0:1]
        y = y + u_scr[SUBLANES - 2:SUBLANES - 2 + tm, :] * w[1:2]
        y = y + u_scr[SUBLANES - 1:SUBLANES - 1 + tm, :] * w[2:3]
        y = y + u * w[3:4]
        o_ref = outs[c0 // GDN_QK]
        o_ref[:, c0 % GDN_QK:c0 % GDN_QK + _GDN_CW] = _gdn_post_conv(y, c0)
    gate_ref[...] = _dot(xn, w_ref[:, GDN_CONV_CH:GDN_CONV_CH + GDN_VW])
    ab = _dot(xn, w_ref[:, GDN_CONV_CH + GDN_VW:_GDN_W_COLS])
    gb_ref[...] = _gdn_gates(ab, alog_ref[...], dtb_ref[...])

    @pl.when(ti == tpb - 1)
    def _():
        st_ref[...] = carry_scr[SUBLANES - keep:SUBLANES, :]


def _gdn_proj(x, n_batch, norm_g, w, conv_w, alog, dtb, buf, tm=256):
    n = x.shape[0]
    tpb = n // n_batch // tm
    row = lambda i: (i, 0)
    wide = pl.BlockSpec((tm, GDN_QK), row)
    st = pl.BlockSpec((None, GDN_CONV - 1, GDN_CONV_CH), lambda i: (i // tpb, 0, 0))
    f = jax.ShapeDtypeStruct((n, GDN_QK), F32)
    return pl.pallas_call(
        functools.partial(_gdn_proj_kernel, tm=tm, tpb=tpb),
        grid=(n // tm,),
        in_specs=[pl.BlockSpec((tm, D_MODEL), row), _resident((1, D_MODEL)), _resident(w.shape),
                  _resident(conv_w.shape), _resident((1, LANES)), _resident((1, LANES)), st],
        out_specs=[wide, wide, wide, wide, pl.BlockSpec((tm, LANES), row), st],
        out_shape=[f, f, f, f, jax.ShapeDtypeStruct((n, LANES), F32),
                   jax.ShapeDtypeStruct((n_batch, GDN_CONV - 1, GDN_CONV_CH), F32)],
        scratch_shapes=[pltpu.VMEM((SUBLANES, GDN_CONV_CH), F32), pltpu.VMEM((SUBLANES + tm, _GDN_CW), F32)],
        compiler_params=_cparams("arbitrary"),
        name="gdn_proj",
    )(x, norm_g.reshape(1, D_MODEL), w, conv_w, alog, dtb, buf)


def _gdn_head_chunk(qh, kh, vh, beta, gcum_col, gcum_row, s_prev):
    c = qh.shape[0]
    ii = _iota((c, c), 0)
    jj = _iota((c, c), 1)
    decay = jnp.exp(jnp.where(ii >= jj, gcum_col - gcum_row, NEG))
    kb = kh * beta
    kh16 = kh.astype(BF16)
    low = _dot_nt(kb.astype(BF16), kh16) * decay * jnp.where(ii > jj, 1.0, 0.0)
    attn = _dot_nt(qh.astype(BF16), kh16) * decay
    x = jnp.where(ii == jj, 1.0, 0.0) - low
    p = low
    for _ in range(c.bit_length() - 2):
        p16 = p.astype(BF16)
        p = _dot(p16, p16)
        x = x + _dot(x.astype(BF16), p.astype(BF16))
    rhs = jnp.concatenate([vh * beta, kb * jnp.exp(gcum_col)], axis=1)
    sol = _dot(x.astype(BF16), rhs.astype(BF16))
    u_base, w_dec = sol[:, :GDN_DV], sol[:, GDN_DV:]
    s16 = s_prev.astype(BF16)
    u = u_base - _dot(w_dec.astype(BF16), s16)
    u16 = u.astype(BF16)
    o = _dot((qh * jnp.exp(gcum_col)).astype(BF16), s16) + _dot(attn.astype(BF16), u16)
    g_last = gcum_col[c - 1:c, :]
    k_dec = kh * jnp.exp(g_last - gcum_col)
    s_new = s_prev * jnp.exp(g_last) + _dot_tn(k_dec.astype(BF16), u16)
    return o, s_new


def _gdn_chunk_kernel(q_ref, k_ref, v_ref, gb_ref, gate_ref, x_ref, s0_ref, ng_ref, wout_ref,
                      y_ref, s_ref, s_scr, o_scr, *, rt, tpb):
    ti = pl.program_id(1)

    @pl.when(ti == 0)
    def _():
        s_scr[...] = s0_ref[...]

    c = GDN_CHUNK
    tri = jnp.where(_iota((c, c), 0) >= _iota((c, c), 1), 1.0, 0.0)

    def chunk(ci, carry):
        r0 = pl.multiple_of(ci * c, c)
        gb = gb_ref[pl.ds(r0, c), :]
        gcum = _dot_f32(tri, gb)
        gcum_t = gcum.T
        for h in range(GDN_HEADS):
            cols = slice(h * GDN_DK, (h + 1) * GDN_DK)
            o, s_new = _gdn_head_chunk(
                q_ref[pl.ds(r0, c), cols], k_ref[pl.ds(r0, c), cols], v_ref[pl.ds(r0, c), cols],
                gb[:, GDN_HEADS + h:GDN_HEADS + h + 1], gcum[:, h:h + 1], gcum_t[h:h + 1, :], s_scr[h])
            s_scr[h] = s_new
            o_scr[pl.ds(r0, c), cols] = o
        return carry

    lax.fori_loop(0, rt // c, chunk, 0)

    ng = ng_ref[...]
    outs = []
    for h in range(GDN_HEADS):
        cols = slice(h * GDN_DV, (h + 1) * GDN_DV)
        outs.append((_rmsnorm(o_scr[:, cols], ng) * jax.nn.silu(gate_ref[:, cols])).astype(BF16))
    y_ref[...] = x_ref[...] + _dot(jnp.concatenate(outs, axis=1), wout_ref[...])

    @pl.when(ti == tpb - 1)
    def _():
        s_ref[...] = s_scr[...]


def _gdn_chunked(q, k, v, gb, gate, x, s0, norm_g, w_out, n_batch, rt=256):
    n = x.shape[0]
    tpb = n // n_batch // rt
    row = lambda b, i: (b * tpb + i, 0)
    wide = pl.BlockSpec((rt, GDN_QK), row)
    st = pl.BlockSpec((None, GDN_HEADS, GDN_DK, GDN_DV), lambda b, i: (b, 0, 0, 0))
    return pl.pallas_call(
        functools.partial(_gdn_chunk_kernel, rt=rt, tpb=tpb),
        grid=(n_batch, tpb),
        in_specs=[wide, wide, wide, pl.BlockSpec((rt, LANES), row), wide, wide, st,
                  _resident((1, GDN_DV)), _resident(w_out.shape)],
        out_specs=[wide, st],
        out_shape=[jax.ShapeDtypeStruct((n, D_MODEL), F32),
                   jax.ShapeDtypeStruct((n_batch, GDN_HEADS, GDN_DK, GDN_DV), F32)],
        scratch_shapes=[pltpu.VMEM((GDN_HEADS, GDN_DK, GDN_DV), F32), pltpu.VMEM((rt, GDN_VW), F32)],
        compiler_params=_cparams("parallel", "arbitrary"),
        name="gdn_chunked",
    )(q, k, v, gb, gate, x, s0, norm_g.reshape(1, GDN_DV), w_out)


def _gdn_prompt_layer(x, n_batch, norm_g, s0, conv_buf, w_in, conv_w, a_log, dt_bias, head_norm_g, w_out):
    q, k, v, gate, gb, new_buf = _gdn_proj(x, n_batch, norm_g, _gdn_w_in_padded(w_in), conv_w,
                                           _lane_row(a_log), _lane_row(dt_bias), conv_buf)
    y, s_new = _gdn_chunked(q, k, v, gb, gate, x, s0, head_norm_g, w_out.astype(BF16), n_batch)
    return y, s_new, new_buf


_LRU_CW = MXU_DIM


def _lru_gates(xc, wga_ref, bga, wgx_ref, bgx, lam, n0):
    r_parts, i_parts = [], []
    for j in range(xc.shape[1] // LRU_BW):
        blk = xc[:, j * LRU_BW:(j + 1) * LRU_BW].astype(BF16)
        r_parts.append(_dot(blk, wga_ref[n0 + j]))
        i_parts.append(_dot(blk, wgx_ref[n0 + j]))
    cat = lambda ps: ps[0] if len(ps) == 1 else jnp.concatenate(ps, axis=1)
    r = jax.nn.sigmoid(cat(r_parts) + bga)
    i = jax.nn.sigmoid(cat(i_parts) + bgx)
    log_a = -LRU_C * r * _softplus(-lam)
    a = jnp.exp(log_a)
    b = jnp.sqrt(-jnp.tanh(log_a) * (a * a + 1.0)) * i * xc
    return a, b


def _lru_proj_kernel(x_ref, g_ref, w_ref, cw_ref, cb_ref, wga_ref, bga_ref, wgx_ref, bgx_ref, lam_ref, buf_ref,
                     gate_ref, a_ref, b_ref, st_ref, carry_scr, u_scr, *, tm, tpb):
    ti = pl.program_id(0) % tpb
    xn = _rmsnorm(x_ref[...], g_ref[...]).astype(BF16)
    keep = LRU_CONV - 1

    @pl.when(ti == 0)
    def _():
        carry_scr[SUBLANES - keep:SUBLANES, :] = buf_ref[...]

    gate_ref[...] = _dot(xn, w_ref[:, 0:D_RNN])
    for c in range(D_RNN // _LRU_CW):
        c0 = c * _LRU_CW
        cols = slice(c0, c0 + _LRU_CW)
        u = _dot(xn, w_ref[:, D_RNN + c0:D_RNN + c0 + _LRU_CW])
        u_scr[0:SUBLANES, :] = carry_scr[:, cols]
        u_scr[SUBLANES:SUBLANES + tm, :] = u
        carry_scr[:, cols] = u[tm - SUBLANES:tm, :]
        w = cw_ref[:, cols]
        xc = u_scr[SUBLANES - 3:SUBLANES - 3 + tm, :] * w[0:1]
        xc = xc + u_scr[SUBLANES - 2:SUBLANES - 2 + tm, :] * w[1:2]
        xc = xc + u_scr[SUBLANES - 1:SUBLANES - 1 + tm, :] * w[2:3]
        xc = xc + u * w[3:4] + cb_ref[:, cols]
        a, b = _lru_gates(xc, wga_ref, bga_ref[:, cols], wgx_ref, bgx_ref[:, cols], lam_ref[:, cols],
                          c0 // LRU_BW)
        a_ref[:, cols] = a
        b_ref[:, cols] = b

    @pl.when(ti == tpb - 1)
    def _():
        st_ref[...] = carry_scr[SUBLANES - keep:SUBLANES, :]


def _lru_proj(x, n_batch, norm_g, w, conv_w, conv_b, w_ga, b_ga, w_gx, b_gx, lam, buf, tm=256):
    n = x.shape[0]
    tpb = n // n_batch // tm
    row = lambda i: (i, 0)
    wide = pl.BlockSpec((tm, D_RNN), row)
    vec = _resident((1, D_RNN))
    st = pl.BlockSpec((None, LRU_CONV - 1, D_RNN), lambda i: (i // tpb, 0, 0))
    f = jax.ShapeDtypeStruct((n, D_RNN), F32)
    r1 = lambda v: v.reshape(1, D_RNN)
    return pl.pallas_call(
        functools.partial(_lru_proj_kernel, tm=tm, tpb=tpb),
        grid=(n // tm,),
        in_specs=[pl.BlockSpec((tm, D_MODEL), row), _resident((1, D_MODEL)), _resident(w.shape),
                  _resident(conv_w.shape), vec, _resident(w_ga.shape), vec, _resident(w_gx.shape), vec, vec, st],
        out_specs=[wide, wide, wide, st],
        out_shape=[f, f, f, jax.ShapeDtypeStruct((n_batch, LRU_CONV - 1, D_RNN), F32)],
        scratch_shapes=[pltpu.VMEM((SUBLANES, D_RNN), F32), pltpu.VMEM((SUBLANES + tm, _LRU_CW), F32)],
        compiler_params=_cparams("arbitrary"),
        name="lru_proj",
    )(x, norm_g.reshape(1, D_MODEL), w, conv_w, r1(conv_b), w_ga, r1(b_ga), w_gx, r1(b_gx), r1(lam), buf)


def _lru_scan_kernel(a_ref, b_ref, gate_ref, x_ref, h0_ref, wout_ref, y_ref, hl_ref, h_scr, hs_scr, *, rt, tpb):
    ti = pl.program_id(1)

    @pl.when(ti == 0)
    def _():
        h_scr[...] = h0_ref[...]

    def step(t, h):
        h = a_ref[pl.ds(t, 1), :] * h + b_ref[pl.ds(t, 1), :]
        hs_scr[pl.ds(t, 1), :] = h
        return h

    h = lax.fori_loop(0, rt, step, h_scr[...], unroll=8)
    h_scr[...] = h
    y = (hs_scr[...] * jax.nn.gelu(gate_ref[...])).astype(BF16)
    y_ref[...] = x_ref[...] + _dot(y, wout_ref[...])

    @pl.when(ti == tpb - 1)
    def _():
        hl_ref[...] = h


def _lru_scan(a, b, gate, x, h0, w_out, n_batch, rt=256):
    n = x.shape[0]
    tpb = n // n_batch // rt
    wide = pl.BlockSpec((rt, D_RNN), lambda bb, i: (bb * tpb + i, 0))
    st = pl.BlockSpec((None, 1, D_RNN), lambda bb, i: (bb, 0, 0))
    y, hl = pl.pallas_call(
        functools.partial(_lru_scan_kernel, rt=rt, tpb=tpb),
        grid=(n_batch, tpb),
        in_specs=[wide, wide, wide, wide, st, _resident(w_out.shape)],
        out_specs=[wide, st],
        out_shape=[jax.ShapeDtypeStruct((n, D_MODEL), F32), jax.ShapeDtypeStruct((n_batch, 1, D_RNN), F32)],
        scratch_shapes=[pltpu.VMEM((1, D_RNN), F32), pltpu.VMEM((rt, D_RNN), F32)],
        compiler_params=_cparams("parallel", "arbitrary"),
        name="lru_scan",
    )(a, b, gate, x, h0.reshape(n_batch, 1, D_RNN), w_out)
    return y, hl.reshape(n_batch, D_RNN)


def _lru_prompt_layer(x, n_batch, norm_g, h0, conv_buf, w_in, conv_w, conv_b, w_ga, b_ga, w_gx, b_gx, lam, w_out):
    gate, a, b, new_buf = _lru_proj(x, n_batch, norm_g, w_in.astype(BF16), conv_w, conv_b, w_ga.astype(BF16), b_ga,
                                    w_gx.astype(BF16), b_gx, lam, conv_buf)
    y, h_last = _lru_scan(a, b, gate, x, h0, w_out.astype(BF16), n_batch)
    return y, h_last, new_buf


def _final_norm_kernel(x_ref, g_ref, o_ref):
    o_ref[...] = _rmsnorm(x_ref[...], g_ref[...])


def _final_norm(x, g, tm):
    n = x.shape[0]
    row = lambda i: (i, 0)
    return pl.pallas_call(
        _final_norm_kernel,
        grid=(n // tm,),
        in_specs=[pl.BlockSpec((tm, D_MODEL), row), _resident((1, D_MODEL))],
        out_specs=pl.BlockSpec((tm, D_MODEL), row),
        out_shape=jax.ShapeDtypeStruct((n, D_MODEL), F32),
        compiler_params=_cparams("parallel"),
        name="final_norm",
    )(x, g.reshape(1, D_MODEL))


def _norm_matmul_small_kernel(x_ref, g_ref, w_ref, o_ref):
    xn = _rmsnorm(x_ref[...], g_ref[...]).astype(BF16)
    o_ref[...] = _dot(xn, w_ref[...])


def _norm_matmul_small(x, g, w, n_col_tiles):
    n, k = x.shape[0], w.shape[1]
    tn = k // n_col_tiles
    return pl.pallas_call(
        _norm_matmul_small_kernel,
        grid=(n_col_tiles,),
        in_specs=[_resident((n, D_MODEL)), _resident((1, D_MODEL)), pl.BlockSpec((D_MODEL, tn), lambda j: (0, j))],
        out_specs=pl.BlockSpec((n, tn), lambda j: (0, j)),
        out_shape=jax.ShapeDtypeStruct((n, k), F32),
        compiler_params=_cparams("parallel"),
        name="norm_matmul_small",
    )(x, g.reshape(1, D_MODEL), w)


_CMP_PAGES_PER_STEP = 8
_CMP_BLK_PER_PAGE = PAGE_SIZE // L_CMP


def _nsa_compress_sample_kernel(pt_ref, *refs, n_pages, n_blk, n_out):
    pages = refs[:_CMP_PAGES_PER_STEP]
    new_ref, bd_ref, pe_ref, o_ref, rows_scr = refs[_CMP_PAGES_PER_STEP:]
    b = pl.program_id(0)
    pg = pl.program_id(1)
    past = n_pages * PAGE_SIZE
    n_rows = rows_scr.shape[1]
    for r in range(_CMP_PAGES_PER_STEP):
        r0 = pl.multiple_of((pg * _CMP_PAGES_PER_STEP + r) * PAGE_SIZE, PAGE_SIZE)
        for c in range(4):
            rows_scr[c, pl.ds(r0, PAGE_SIZE), :] = pages[r][:, c * LANES:(c + 1) * LANES]

    @pl.when(pg == pl.num_programs(1) - 1)
    def _():
        new = new_ref[pl.ds(b, 1), :]
        tail = _iota((n_rows - past, LANES), 0)
        for c in range(4):
            rows_scr[c, past:n_rows, :] = jnp.where(tail == 0, new[:, c * LANES:(c + 1) * LANES], 0.0)
        o_ref[...] = jnp.zeros(o_ref.shape, F32)

        def load_rows(plane, l):
            return rows_scr[plane, pl.ds(l, n_blk, stride=L_CMP), :]

        _compress_planes(load_rows, bd_ref, pe_ref, o_ref.at[:, :, 0:n_blk, :], n_blk)


def _nsa_compress_sample(cache, layer, page_table, new_rows, bd, pe2):
    n_b, n_pages = page_table.shape
    past = n_pages * PAGE_SIZE
    n_blk = -(-(past + 1) // L_SEL) * L_SEL // L_CMP
    n_blk = -(-n_blk // SUBLANES) * SUBLANES
    n_out = -(-n_blk // LANES) * LANES
    cache4 = cache.reshape(cache.shape[0], cache.shape[1], PAGE_SIZE, 2 * NSA_KVW)

    def page_spec(r):
        return pl.BlockSpec((None, None, PAGE_SIZE, 2 * NSA_KVW),
                            lambda b, pg, pt: (layer, pt[b, pg * _CMP_PAGES_PER_STEP + r], 0, 0))

    grid_spec = pltpu.PrefetchScalarGridSpec(
        num_scalar_prefetch=1,
        grid=(n_b, n_pages // _CMP_PAGES_PER_STEP),
        in_specs=[page_spec(r) for r in range(_CMP_PAGES_PER_STEP)]
        + [pl.BlockSpec(new_rows.shape, lambda b, pg, pt: (0, 0)),
           pl.BlockSpec(bd.shape, lambda b, pg, pt: (0, 0, 0, 0)),
           pl.BlockSpec(pe2.shape, lambda b, pg, pt: (0, 0, 0))],
        out_specs=pl.BlockSpec((None, 2, NSA_KV, n_out, NSA_DH), lambda b, pg, pt: (b, 0, 0, 0, 0)),
        scratch_shapes=[pltpu.VMEM((4, n_blk * L_CMP, LANES), F32)])
    return pl.pallas_call(
        functools.partial(_nsa_compress_sample_kernel, n_pages=n_pages, n_blk=n_blk, n_out=n_out),
        grid_spec=grid_spec,
        out_shape=jax.ShapeDtypeStruct((n_b, 2, NSA_KV, n_out, NSA_DH), F32),
        compiler_params=_cparams("parallel", "arbitrary"),
        name="nsa_compress_sample",
    )(page_table, *([cache4] * _CMP_PAGES_PER_STEP), new_rows, bd, pe2)


def _pad_rows8(x):
    return jnp.concatenate([x, jnp.zeros((SUBLANES - x.shape[0], x.shape[1]), x.dtype)], axis=0)


def _nsa_cmp_topk_sample_kernel(q_ref, kvc_ref, oc_ref, idx_ref, *, qpos, n_sel):
    n_cmp = kvc_ref.shape[2]
    n_lane = 2 * LANES
    c_end = (_iota((SUBLANES, n_cmp), 1) + 1) * L_CMP - 1
    valid = c_end <= qpos
    any_valid = 1.0 if qpos >= L_CMP - 1 else 0.0
    pair_mat = jnp.where((_iota((n_cmp, n_lane), 0) >> 1) == _iota((n_cmp, n_lane), 1), 1.0, 0.0)
    imps = []
    for g in range(NSA_KV):
        q8 = _pad_rows8(q_ref[g * NSA_HPG:(g + 1) * NSA_HPG, :])
        s = _dot_nt(q8, kvc_ref[0, g].astype(BF16))
        p = _softmax_rows(jnp.where(valid, s, NEG)) * any_valid
        o_c = _dot(p.astype(BF16), kvc_ref[1, g].astype(BF16))
        oc_ref[g * NSA_HPG:(g + 1) * NSA_HPG, :] = o_c[0:NSA_HPG]
        imps.append(p[0:1] + p[1:2] + p[2:3] + p[3:4])
    imp = jnp.concatenate(imps + [jnp.zeros((SUBLANES - NSA_KV, n_cmp), F32)], axis=0)
    pair = _dot_f32(imp, pair_mat)
    blk = _iota((SUBLANES, n_lane), 1)
    forced = (blk == 0) | (blk == qpos // L_SEL)
    avail = blk * L_SEL <= qpos
    val = jnp.where(forced, FORCE, jnp.where(avail, pair, -1.0))
    val = jnp.where(blk < n_sel, val, -2.0)
    out = jnp.zeros((SUBLANES, LANES), I32)
    out_lane = _iota((SUBLANES, LANES), 1)
    for r in range(N_SEL):
        m = jnp.max(val, axis=-1, keepdims=True)
        first = jnp.min(jnp.where(val == m, blk, n_lane), axis=-1, keepdims=True)
        out = jnp.where(out_lane == r, first, out)
        val = jnp.where(blk == first, -3.0, val)
    idx_ref[...] = out


def _nsa_cmp_topk_sample(q3, kvc, qpos, n_sel):
    n_b = q3.shape[0]
    n_cmp = kvc.shape[3]
    return pl.pallas_call(
        functools.partial(_nsa_cmp_topk_sample_kernel, qpos=qpos, n_sel=n_sel),
        grid=(n_b,),
        in_specs=[pl.BlockSpec((None, NSA_HEADS, NSA_DH), lambda b: (b, 0, 0)),
                  pl.BlockSpec((None, 2, NSA_KV, n_cmp, NSA_DH), lambda b: (b, 0, 0, 0, 0))],
        out_specs=[pl.BlockSpec((None, NSA_HEADS, NSA_DH), lambda b: (b, 0, 0)),
                   pl.BlockSpec((None, SUBLANES, LANES), lambda b: (b, 0, 0))],
        out_shape=[jax.ShapeDtypeStruct((n_b, NSA_HEADS, NSA_DH), F32),
                   jax.ShapeDtypeStruct((n_b, SUBLANES, LANES), I32)],
        compiler_params=_cparams("parallel"),
        name="nsa_cmp_topk_sample",
    )(q3, kvc)


def _nsa_gather_sel_kernel(pt_ref, idx_ref, *refs):
    blocks = refs[:NSA_KV * N_SEL]
    o_ref = refs[NSA_KV * N_SEL]
    for g in range(NSA_KV):
        for r in range(N_SEL):
            blk = blocks[g * N_SEL + r][...]
            kv = jnp.concatenate([blk[:, g * NSA_DH:(g + 1) * NSA_DH],
                                  blk[:, NSA_KVW + g * NSA_DH:NSA_KVW + (g + 1) * NSA_DH]], axis=1)
            o_ref[g, r * L_SEL:(r + 1) * L_SEL, :] = kv.astype(BF16)


def _nsa_gather_sel(cache, layer, page_table, idx):
    n_b, n_pages = page_table.shape
    past = n_pages * PAGE_SIZE
    halves = PAGE_SIZE // L_SEL
    cache4 = cache.reshape(cache.shape[0], cache.shape[1] * halves, L_SEL, 2 * NSA_KVW)
    last_blk = past // L_SEL - 1

    def blk_spec(n):
        def index(b, pt, ix):
            s = jnp.minimum(ix[b, n], last_blk)
            return (layer, pt[b, s // halves] * halves + s % halves, 0, 0)
        return pl.BlockSpec((None, None, L_SEL, 2 * NSA_KVW), index)

    grid_spec = pltpu.PrefetchScalarGridSpec(
        num_scalar_prefetch=2,
        grid=(n_b,),
        in_specs=[blk_spec(n) for n in range(NSA_KV * N_SEL)],
        out_specs=pl.BlockSpec((None, NSA_KV, N_SEL * L_SEL, 2 * NSA_DH), lambda b, pt, ix: (b, 0, 0, 0)))
    return pl.pallas_call(
        _nsa_gather_sel_kernel,
        grid_spec=grid_spec,
        out_shape=jax.ShapeDtypeStruct((n_b, NSA_KV, N_SEL * L_SEL, 2 * NSA_DH), BF16),
        name="nsa_gather_sel",
    )(page_table, idx, *([cache4] * (NSA_KV * N_SEL)))


def _nsa_sel_win_sample_kernel(idx_ref, qr_ref, oc_ref, gate_ref, selnew_ref, winnew_ref, win_ref, kv_ref, o_ref,
                               *, qpos, past):
    gates = gate_ref[...]
    glane = _iota((1, LANES), 1)
    n_key = N_SEL * L_SEL
    n_all = n_key + LANES
    lane_k = _iota((SUBLANES, n_all), 1)
    row_new = _iota((LANES, NSA_DH), 0)
    wb = win_ref.shape[0]
    wlane = _iota((SUBLANES, wb + LANES), 1)
    kwpos = past - wb + wlane
    dist = qpos - kwpos
    wmask = (dist >= 0) & (dist <= WINDOW) & (kwpos >= 0) & (wlane <= wb)
    expand = jnp.where(_iota((LANES, n_all), 0) == (_iota((LANES, n_all), 1) >> _LOG_SEL), 1.0, 0.0)
    idx_f = idx_ref[...].astype(F32)
    blk_of_lane = _dot_f32(jnp.where(_iota((SUBLANES, LANES), 1) < N_SEL, idx_f, 0.0), expand)
    sel_new = selnew_ref[...]
    win_new = winnew_ref[...]
    win = win_ref[...]
    for g in range(NSA_KV):
        kc = slice(g * NSA_DH, (g + 1) * NSA_DH)
        vc = slice(NSA_KVW + g * NSA_DH, NSA_KVW + (g + 1) * NSA_DH)
        q8 = _pad_rows8(qr_ref[g * NSA_HPG:(g + 1) * NSA_HPG, :])
        kv = kv_ref[g]
        knew = jnp.where(row_new == 0, sel_new[:, kc], 0.0).astype(BF16)
        vnew = jnp.where(row_new == 0, sel_new[:, vc], 0.0).astype(BF16)
        kcat = jnp.concatenate([kv[:, :NSA_DH], knew], axis=0)
        vcat = jnp.concatenate([kv[:, NSA_DH:], vnew], axis=0)
        blk_g = blk_of_lane[g:g + 1, :]
        new_blk = float(qpos // L_SEL)
        new_selected = jnp.max(jnp.where(blk_g == new_blk, 1.0, 0.0), axis=-1, keepdims=True)
        kpos = blk_g * float(L_SEL) + (lane_k & (L_SEL - 1)).astype(F32)
        cached_live = jnp.where((kpos < float(past)) & (kpos <= float(qpos)), 1.0, 0.0)
        live = jnp.where(lane_k < n_key, cached_live, jnp.where(lane_k == n_key, new_selected, 0.0))
        s = _dot_nt(q8, kcat)
        p = _softmax_rows(jnp.where(live > 0.5, s, NEG))
        o_s = _dot(p.astype(BF16), vcat)
        kw = jnp.concatenate([win[:, kc], jnp.where(row_new == 0, win_new[:, kc], 0.0)], axis=0).astype(BF16)
        vw = jnp.concatenate([win[:, vc], jnp.where(row_new == 0, win_new[:, vc], 0.0)], axis=0).astype(BF16)
        pw = _softmax_rows(jnp.where(wmask, _dot_nt(q8, kw), NEG))
        o_w = _dot(pw.astype(BF16), vw)
        o_c = oc_ref[g * NSA_HPG:(g + 1) * NSA_HPG, :]
        rows = []
        for j in range(NSA_HPG):
            acc = jnp.zeros((1, NSA_DH), F32)
            for br, ob in enumerate((o_c, o_s, o_w)):
                col = br * NSA_HEADS + g * NSA_HPG + j
                gcol = jnp.sum(jnp.where(glane == col, gates, 0.0), axis=-1, keepdims=True)
                acc = acc + gcol * ob[j:j + 1, :]
            rows.append(acc)
        o_ref[g * NSA_HPG:(g + 1) * NSA_HPG, :] = jnp.concatenate(rows, axis=0)


def _nsa_sel_win_sample(kv_sel, idx8, qr3, oc3, gates, sel_new, win_new, win_state, qpos, past):
    n_b = qr3.shape[0]
    wb = win_state.shape[1]
    win3 = win_state.reshape(n_b, wb, 2 * NSA_KVW)
    b3 = lambda b: (b, 0, 0)
    return pl.pallas_call(
        functools.partial(_nsa_sel_win_sample_kernel, qpos=qpos, past=past),
        grid=(n_b,),
        in_specs=[pl.BlockSpec((None, SUBLANES, LANES), b3),
                  pl.BlockSpec((None, NSA_HEADS, NSA_DH), b3), pl.BlockSpec((None, NSA_HEADS, NSA_DH), b3),
                  pl.BlockSpec((None, 1, LANES), b3), pl.BlockSpec((None, 1, 2 * NSA_KVW), b3),
                  pl.BlockSpec((None, 1, 2 * NSA_KVW), b3), pl.BlockSpec((None, wb, 2 * NSA_KVW), b3),
                  pl.BlockSpec((None, NSA_KV, N_SEL * L_SEL, 2 * NSA_DH), lambda b: (b, 0, 0, 0))],
        out_specs=pl.BlockSpec((None, NSA_HEADS, NSA_DH), b3),
        out_shape=jax.ShapeDtypeStruct((n_b, NSA_HEADS, NSA_DH), F32),
        compiler_params=_cparams("parallel"),
        name="nsa_sel_win_sample",
    )(idx8, qr3, oc3, gates.reshape(n_b, 1, LANES), sel_new.reshape(n_b, 1, -1),
      win_new.reshape(n_b, 1, -1), win3, kv_sel)


def _nsa_sample_layer(x, layer, norm_g, w_in, cmp_pe, cmp_w, w_out, cache_cmp, cache_sel, win_state, page_table,
                      tables):
    n_b = x.shape[0]
    past = page_table.shape[1] * PAGE_SIZE
    qpos = past
    q, qr, cmp_rows, sel_rows, win_rows, gates, _ = _nsa_proj(
        x, norm_g.reshape(1, D_MODEL), _nsa_w_in_padded(w_in), tables, 1, n_b)
    bd, pe2 = _cmp_weights(cmp_w, cmp_pe)
    kvc = _nsa_compress_sample(cache_cmp, layer, page_table, cmp_rows, bd, pe2)
    n_sel = -(-(past + 1) // L_SEL)
    oc3, idx8 = _nsa_cmp_topk_sample(q.reshape(n_b, NSA_HEADS, NSA_DH), kvc, qpos, n_sel)
    idx = idx8[:, :NSA_KV, :N_SEL].reshape(n_b, NSA_KV * N_SEL)
    kv_sel = _nsa_gather_sel(cache_sel, layer, page_table, idx)
    o3 = _nsa_sel_win_sample(kv_sel, idx8, qr.reshape(n_b, NSA_HEADS, NSA_DH), oc3, gates,
                             sel_rows, win_rows, win_state, qpos, past)
    y = _matmul_residual(o3.reshape(n_b, NSA_Q), w_out.astype(BF16), x, n_b)
    return y, cmp_rows, sel_rows, win_rows


def _gdn_sample_pre_kernel(z_ref, b0_ref, b1_ref, b2_ref, cw_ref, alog_ref, dtb_ref, q_ref, k_ref, v_ref, gb_ref):
    outs = (q_ref, k_ref, v_ref)
    for c in range(GDN_CONV_CH // _GDN_CW):
        c0 = c * _GDN_CW
        cols = slice(c0, c0 + _GDN_CW)
        w = cw_ref[:, cols]
        y = b0_ref[:, cols] * w[0:1] + b1_ref[:, cols] * w[1:2] + b2_ref[:, cols] * w[2:3] + z_ref[:, cols] * w[3:4]
        o_ref = outs[c0 // GDN_QK]
        o_ref[:, c0 % GDN_QK:c0 % GDN_QK + _GDN_CW] = _gdn_post_conv(y, c0)
    gb_ref[...] = _gdn_gates(z_ref[:, GDN_CONV_CH + GDN_VW:_GDN_W_COLS], alog_ref[...], dtb_ref[...])


def _gdn_sample_pre(z, bufs, conv_w, alog, dtb):
    n = z.shape[0]
    f = jax.ShapeDtypeStruct((n, GDN_QK), F32)
    return pl.pallas_call(
        _gdn_sample_pre_kernel,
        out_shape=[f, f, f, jax.ShapeDtypeStruct((n, LANES), F32)],
        compiler_params=pltpu.CompilerParams(vmem_limit_bytes=VMEM_LIMIT),
        name="gdn_sample_pre",
    )(z, *bufs, conv_w, alog, dtb)


def _gdn_sample_state_kernel(s0_ref, qt_ref, kt_ref, v_ref, gb_ref, gate_ref, ng_ref, s_ref, o_ref):
    gb = gb_ref[...]
    lane = _iota((1, LANES), 1)
    ng = ng_ref[...]
    for h in range(GDN_HEADS):
        cols = slice(h * GDN_DV, (h + 1) * GDN_DV)
        g_h = jnp.sum(jnp.where(lane == h, gb, 0.0), axis=-1, keepdims=True)
        beta = jnp.sum(jnp.where(lane == GDN_HEADS + h, gb, 0.0), axis=-1, keepdims=True)
        eg = jnp.exp(g_h)
        s_prev = s0_ref[h]
        k_col = kt_ref[:, h:h + 1]
        q_col = qt_ref[:, h:h + 1]
        v_row = v_ref[:, cols]
        u = v_row * beta - jnp.sum(s_prev * (k_col * beta * eg), axis=0, keepdims=True)
        attn = jnp.sum(q_col * k_col, axis=0, keepdims=True)
        o = jnp.sum(s_prev * (q_col * eg), axis=0, keepdims=True) + attn * u
        s_ref[h] = s_prev * eg + k_col * u
        o_ref[:, cols] = (_rmsnorm(o, ng) * jax.nn.silu(gate_ref[:, cols])).astype(BF16)


def _gdn_sample_state(s0, q_t, k_t, v, gb, gate, head_norm_g):
    n_b = s0.shape[0]
    st = pl.BlockSpec((None, GDN_HEADS, GDN_DK, GDN_DV), lambda b: (b, 0, 0, 0))
    col = pl.BlockSpec((None, GDN_DK, GDN_HEADS), lambda b: (b, 0, 0))
    wide = pl.BlockSpec((None, 1, GDN_VW), lambda b: (b, 0, 0))
    return pl.pallas_call(
        _gdn_sample_state_kernel,
        grid=(n_b,),
        in_specs=[st, col, col, wide, pl.BlockSpec((None, 1, LANES), lambda b: (b, 0, 0)), wide,
                  _resident((1, GDN_DV))],
        out_specs=[st, wide],
        out_shape=[jax.ShapeDtypeStruct(s0.shape, F32), jax.ShapeDtypeStruct((n_b, 1, GDN_VW), BF16)],
        compiler_params=_cparams("parallel"),
        name="gdn_sample_state",
    )(s0, q_t, k_t, v.reshape(n_b, 1, GDN_VW), gb.reshape(n_b, 1, LANES), gate.reshape(n_b, 1, GDN_VW),
      head_norm_g.reshape(1, GDN_DV))


def _gdn_sample_layer(x, norm_g, s0, conv_buf, w_in, conv_w, a_log, dt_bias, head_norm_g, w_out):
    n_b = x.shape[0]
    z = _norm_matmul_small(x, norm_g, _gdn_w_in_padded(w_in), 3)
    bufs = [conv_buf[:, j, :] for j in range(GDN_CONV - 1)]
    q, k, v, gb = _gdn_sample_pre(z, bufs, conv_w, _lane_row(a_log), _lane_row(dt_bias))
    to_cols = lambda t: t.reshape(n_b, GDN_HEADS, GDN_DK).transpose(0, 2, 1)
    gate = z[:, GDN_CONV_CH:GDN_CONV_CH + GDN_VW]
    s_new, o = _gdn_sample_state(s0, to_cols(q), to_cols(k), v, gb, gate, head_norm_g)
    y = _matmul_residual(o.reshape(n_b, GDN_VW), w_out.astype(BF16), x, n_b)
    new_buf = jnp.concatenate([conv_buf[:, 1:, :], z[:, None, :GDN_CONV_CH]], axis=1)
    return y, s_new, new_buf


def _lru_sample_kernel(z_ref, b0_ref, b1_ref, b2_ref, cw_ref, cb_ref, wga_ref, bga_ref, wgx_ref, bgx_ref, lam_ref,
                       h0_ref, h_ref, y_ref):
    for c in range(D_RNN // _LRU_CW):
        c0 = c * _LRU_CW
        cols = slice(c0, c0 + _LRU_CW)
        w = cw_ref[:, cols]
        xc = (b0_ref[:, cols] * w[0:1] + b1_ref[:, cols] * w[1:2] + b2_ref[:, cols] * w[2:3]
              + z_ref[:, D_RNN + c0:D_RNN + c0 + _LRU_CW] * w[3:4] + cb_ref[:, cols])
        a, b = _lru_gates(xc, wga_ref, bga_ref[:, cols], wgx_ref, bgx_ref[:, cols], lam_ref[:, cols], c0 // LRU_BW)
        h = a * h0_ref[:, cols] + b
        h_ref[:, cols] = h
        y_ref[:, cols] = (h * jax.nn.gelu(z_ref[:, cols])).astype(BF16)


def _lru_sample_layer(x, norm_g, h0, conv_buf, w_in, conv_w, conv_b, w_ga, b_ga, w_gx, b_gx, lam, w_out):
    n_b = x.shape[0]
    z = _norm_matmul_small(x, norm_g, w_in.astype(BF16), 2)
    r1 = lambda v: v.reshape(1, D_RNN)
    bufs = [conv_buf[:, j, :] for j in range(LRU_CONV - 1)]
    h, y = pl.pallas_call(
        _lru_sample_kernel,
        out_shape=[jax.ShapeDtypeStruct((n_b, D_RNN), F32), jax.ShapeDtypeStruct((n_b, D_RNN), BF16)],
        compiler_params=pltpu.CompilerParams(vmem_limit_bytes=VMEM_LIMIT),
        name="lru_sample",
    )(z, *bufs, conv_w, r1(conv_b), w_ga.astype(BF16), r1(b_ga), w_gx.astype(BF16), r1(b_gx), r1(lam), h0)
    y = _matmul_residual(y, w_out.astype(BF16), x, n_b)
    new_buf = jnp.concatenate([conv_buf[:, 1:, :], z[:, None, D_RNN:]], axis=1)
    return y, h, new_buf


def _ffn_sample_kernel(x_ref, g_ref, wv_ref, wg_ref, cwv_ref, cwg_ref, b0v_ref, b0g_ref, b1v_ref, b1g_ref, wout_ref,
                       y_ref, uv_ref, ug_ref, xn_scr):
    c = pl.program_id(0)

    @pl.when(c == 0)
    def _():
        x = x_ref[...]
        xn_scr[...] = _rmsnorm(x, g_ref[...]).astype(BF16)
        y_ref[...] = x

    xn = xn_scr[...]
    uv = _dot(xn, wv_ref[...])
    ug = _dot(xn, wg_ref[...])
    uv_ref[...] = uv
    ug_ref[...] = ug
    cwv, cwg = cwv_ref[...], cwg_ref[...]
    val = b0v_ref[...] * cwv[0:1] + b1v_ref[...] * cwv[1:2] + uv * cwv[2:3]
    gt = b0g_ref[...] * cwg[0:1] + b1g_ref[...] * cwg[1:2] + ug * cwg[2:3]
    y_ref[...] += _dot((val * jax.nn.silu(gt)).astype(BF16), wout_ref[...])


def _ffn_sample(x, norm_g, w_in, conv_w, w_out, buf):
    n_b = x.shape[0]
    n_c = D_FF // _FFN_CW
    b0, b1 = buf[:, 0, :], buf[:, 1, :]
    val = lambda shape0: pl.BlockSpec((shape0, _FFN_CW), lambda c: (0, c))
    gat = lambda shape0: pl.BlockSpec((shape0, _FFN_CW), lambda c: (0, n_c + c))
    y, uv, ug = pl.pallas_call(
        _ffn_sample_kernel,
        grid=(n_c,),
        in_specs=[_resident((n_b, D_MODEL)), _resident((1, D_MODEL)), val(D_MODEL), gat(D_MODEL),
                  val(FFN_CONV), gat(FFN_CONV), val(n_b), gat(n_b), val(n_b), gat(n_b),
                  pl.BlockSpec((_FFN_CW, D_MODEL), lambda c: (c, 0))],
        out_specs=[pl.BlockSpec((n_b, D_MODEL), lambda c: (0, 0)), val(n_b), val(n_b)],
        out_shape=[jax.ShapeDtypeStruct((n_b, D_MODEL), F32), jax.ShapeDtypeStruct((n_b, D_FF), F32),
                   jax.ShapeDtypeStruct((n_b, D_FF), F32)],
        scratch_shapes=[pltpu.VMEM((n_b, D_MODEL), BF16)],
        compiler_params=_cparams("arbitrary"),
        name="ffn_sample",
    )(x, norm_g.reshape(1, D_MODEL), w_in, w_in, conv_w, conv_w, b0, b0, b1, b1, w_out)
    new_buf = jnp.stack([b1, jnp.concatenate([uv, ug], axis=-1)], axis=1)
    return y, new_buf


def kernel(x_prompt, x_sample, cache_nsa_cmp, cache_nsa_sel, state_nsa_win, state_gdn_S, state_gdn_conv,
           state_lru_h, state_lru_conv, state_ffn_conv, page_table,
           norm_mix_g, norm_ffn_g, norm_final_g,
           nsa_w_in, nsa_cmp_pe, nsa_cmp_w, nsa_w_out,
           gdn_w_in, gdn_conv_w, gdn_A_log, gdn_dt_bias, gdn_norm_g, gdn_w_out,
           lru_w_in, lru_conv_w, lru_conv_b, lru_w_ga, lru_b_ga, lru_w_gx, lru_b_gx, lru_lambda, lru_w_out,
           ffn_w_in, ffn_conv_w, ffn_w_out):
    n_p, t_len, _ = x_prompt.shape
    n_s = x_sample.shape[0]
    past = page_table.shape[1] * PAGE_SIZE
    wl = min(WINDOW, t_len)
    xp = x_prompt.reshape(n_p * t_len, D_MODEL)
    xs = x_sample.reshape(n_s, D_MODEL)
    tab_p = _rope_tables(jnp.arange(t_len, dtype=I32))
    tab_s = _rope_tables(jnp.full((n_s,), past, I32))
    ffn_w_in16 = ffn_w_in.astype(BF16)
    ffn_w_out16 = ffn_w_out.astype(BF16)
    zeros = lambda *shape: jnp.zeros(shape, F32)
    kv5 = lambda rows, nb: rows.reshape(nb, -1, 2, NSA_KV, NSA_DH)

    p_nsa, s_nsa, p_gdn, s_gdn, p_lru, s_lru, p_ffn, s_ffn = [], [], [], [], [], [], [], []
    for li in range(DEPTH):
        j = li // N_MIXERS
        kind = li % N_MIXERS
        if kind == 0:
            xp, c, s, w = _nsa_prompt_layer(xp, n_p, norm_mix_g[li], nsa_w_in[j], nsa_cmp_pe[j], nsa_cmp_w[j],
                                            nsa_w_out[j], tab_p)
            p_nsa.append((kv5(c, n_p), kv5(s, n_p), kv5(w, n_p)[:, t_len - wl:]))
            xs, c, s, w = _nsa_sample_layer(xs, j, norm_mix_g[li], nsa_w_in[j], nsa_cmp_pe[j], nsa_cmp_w[j],
                                            nsa_w_out[j], cache_nsa_cmp, cache_nsa_sel, state_nsa_win[j],
                                            page_table, tab_s)
            win_all = jnp.concatenate([state_nsa_win[j], kv5(w, n_s)], axis=1)
            s_nsa.append((kv5(c, n_s), kv5(s, n_s), win_all[:, 1:]))
        elif kind == 1:
            args = (gdn_w_in[j], gdn_conv_w[j], gdn_A_log[j], gdn_dt_bias[j], gdn_norm_g[j], gdn_w_out[j])
            xp, s_new, buf = _gdn_prompt_layer(xp, n_p, norm_mix_g[li], zeros(n_p, GDN_HEADS, GDN_DK, GDN_DV),
                                               zeros(n_p, GDN_CONV - 1, GDN_CONV_CH), *args)
            p_gdn.append((s_new, buf))
            xs, s_new, buf = _gdn_sample_layer(xs, norm_mix_g[li], state_gdn_S[j], state_gdn_conv[j], *args)
            s_gdn.append((s_new, buf))
        else:
            args = (lru_w_in[j], lru_conv_w[j], lru_conv_b[j], lru_w_ga[j], lru_b_ga[j], lru_w_gx[j], lru_b_gx[j],
                    lru_lambda[j], lru_w_out[j])
            xp, h, buf = _lru_prompt_layer(xp, n_p, norm_mix_g[li], zeros(n_p, D_RNN), zeros(n_p, LRU_CONV - 1, D_RNN),
                                           *args)
            p_lru.append((h, buf))
            xs, h, buf = _lru_sample_layer(xs, norm_mix_g[li], state_lru_h[j], state_lru_conv[j], *args)
            s_lru.append((h, buf))
        xp, buf = _ffn_prompt(xp, n_p, norm_ffn_g[li], ffn_w_in16[li], ffn_conv_w[li], ffn_w_out16[li],
                              zeros(n_p, FFN_CONV - 1, 2 * D_FF))
        p_ffn.append(buf)
        xs, buf = _ffn_sample(xs, norm_ffn_g[li], ffn_w_in16[li], ffn_conv_w[li], ffn_w_out16[li], state_ffn_conv[li])
        s_ffn.append(buf)

    y_prompt = _final_norm(xp, norm_final_g, 512).reshape(n_p, t_len, D_MODEL)
    y_sample = _final_norm(xs, norm_final_g, n_s).reshape(n_s, 1, D_MODEL)
    stack = lambda entries, k: jnp.stack([e[k] for e in entries])
    return (y_prompt, y_sample,
            stack(p_nsa, 0), stack(s_nsa, 0), stack(p_nsa, 1), stack(s_nsa, 1), stack(p_nsa, 2), stack(s_nsa, 2),
            stack(p_gdn, 0), stack(s_gdn, 0), stack(p_gdn, 1), stack(s_gdn, 1),
            stack(p_lru, 0), stack(s_lru, 0), stack(p_lru, 1), stack(s_lru, 1),
            jnp.stack(p_ffn), jnp.stack(s_ffn))
```

```python
import functools

import jax
import jax.numpy as jnp
from jax import lax
from jax.experimental import pallas as pl
from jax.experimental.pallas import tpu as pltpu

F32 = jnp.float32
BF16 = jnp.bfloat16
I32 = jnp.int32

D_MODEL = 1024
DEPTH = 4
PAST_LEN = 8192
PAGE_SIZE = 128
N_MIXERS = 3

NSA_DH = 64
NSA_HEADS = 16
NSA_HPG = 4
NSA_KV = 4
NSA_Q = NSA_HEADS * NSA_DH
NSA_KVW = NSA_KV * NSA_DH
ROT_DIM = NSA_DH // 4
ROPE_THETA = 500000.0
L_CMP = 32
L_SEL = 64
N_SEL = 16
WINDOW = 512
_LOG_SEL = 6
NSA_SCALE = NSA_DH ** -0.5

GDN_DK = 128
GDN_DV = 128
GDN_HEADS = 8
GDN_QK = GDN_HEADS * GDN_DK
GDN_VW = GDN_HEADS * GDN_DV
GDN_CONV = 4
GDN_CONV_CH = 2 * GDN_QK + GDN_VW
GDN_CHUNK = 64

D_RNN = D_MODEL
LRU_BLOCKS = 8
LRU_BW = D_RNN // LRU_BLOCKS
LRU_CONV = 4
LRU_C = 8.0

D_FF = 2816
FFN_CONV = 3

EPS = 1e-6
NEG = -1e30
FORCE = 1e4

LANES = 128
SUBLANES = 8
MXU_DIM = 256
VMEM_LIMIT = 56 * 1024 * 1024


def _cparams(*sem):
    return pltpu.CompilerParams(dimension_semantics=sem, vmem_limit_bytes=VMEM_LIMIT)


def _resident(shape):
    nd = len(shape)
    return pl.BlockSpec(shape, lambda *_: (0,) * nd, pipeline_mode=pl.Buffered(1))


def _rmsnorm(x, g):
    return x * lax.rsqrt(jnp.mean(x * x, axis=-1, keepdims=True) + EPS) * g


def _dot(a, b):
    return jnp.dot(a, b, preferred_element_type=F32)


def _dot_nt(a, b):
    return lax.dot_general(a, b, (((1,), (1,)), ((), ())), preferred_element_type=F32)


def _dot_tn(a, b):
    return lax.dot_general(a, b, (((0,), (0,)), ((), ())), preferred_element_type=F32)


def _dot_f32(a, b):
    return jnp.dot(a, b, precision=lax.Precision.HIGHEST, preferred_element_type=F32)


def _iota(shape, axis):
    return lax.broadcasted_iota(I32, shape, axis)


def _rope_tables(pos):
    half = ROT_DIM // 2
    inv = ROPE_THETA ** (-jnp.arange(half, dtype=F32) / half)
    ang = pos.astype(F32)[:, None] * inv[None, :]
    lane = jnp.arange(LANES) % NSA_DH
    c = jnp.cos(ang)[:, lane % half]
    s = jnp.sin(ang)[:, lane % half]
    cos = jnp.where(lane < ROT_DIM, c, 1.0)
    sin_a = jnp.where((lane >= half) & (lane < ROT_DIM), s, 0.0)
    sin_b = jnp.where(lane < half, -s, 0.0)
    return cos.astype(F32), sin_a.astype(F32), sin_b.astype(F32)


def _rope(x, cos, sin_a, sin_b):
    half = ROT_DIM // 2
    outs = []
    for c in range(x.shape[1] // LANES):
        xc = x[:, c * LANES:(c + 1) * LANES]
        outs.append(xc * cos + pltpu.roll(xc, half, 1) * sin_a + pltpu.roll(xc, LANES - half, 1) * sin_b)
    return outs[0] if len(outs) == 1 else jnp.concatenate(outs, axis=1)


_NSA_W_COLS = NSA_Q + 6 * NSA_KVW + LANES


def _nsa_proj_kernel(x_ref, g_ref, w_ref, cos_ref, sa_ref, sb_ref,
                     q_ref, qr_ref, cmp_ref, sel_ref, win_ref, gate_ref, kvb_ref):
    xn = _rmsnorm(x_ref[...], g_ref[...]).astype(BF16)
    cos, sa, sb = cos_ref[...], sa_ref[...], sb_ref[...]

    def mm(c0, c1):
        return _dot(xn, w_ref[:, c0:c1])

    zq = mm(0, NSA_Q)
    q_ref[...] = (zq * NSA_SCALE).astype(BF16)
    qr_ref[...] = (_rope(zq, cos, sa, sb) * NSA_SCALE).astype(BF16)
    c0 = NSA_Q
    cmp_ref[...] = mm(c0, c0 + 2 * NSA_KVW)
    c0 += 2 * NSA_KVW
    for kind, o_ref in ((0, sel_ref), (1, win_ref)):
        z = mm(c0, c0 + 2 * NSA_KVW)
        kr = _rope(z[:, :NSA_KVW], cos, sa, sb)
        v = z[:, NSA_KVW:]
        o_ref[:, :NSA_KVW] = kr
        o_ref[:, NSA_KVW:] = v
        for g in range(NSA_KV):
            kvb_ref[2 * kind, g] = kr[:, g * NSA_DH:(g + 1) * NSA_DH].astype(BF16)
            kvb_ref[2 * kind + 1, g] = v[:, g * NSA_DH:(g + 1) * NSA_DH].astype(BF16)
        c0 += 2 * NSA_KVW
    gate_ref[...] = jax.nn.sigmoid(mm(c0, c0 + LANES))


def _nsa_proj(x, g, w, tables, n_batch, tm):
    n = x.shape[0]
    t_len = n // n_batch
    tpb = t_len // tm
    row = lambda i: (i, 0)
    tab = pl.BlockSpec((tm, LANES), lambda i: (i % tpb, 0))
    return pl.pallas_call(
        _nsa_proj_kernel,
        grid=(n // tm,),
        in_specs=[pl.BlockSpec((tm, D_MODEL), row), _resident((1, D_MODEL)), _resident((D_MODEL, _NSA_W_COLS)),
                  tab, tab, tab],
        out_specs=[pl.BlockSpec((tm, NSA_Q), row), pl.BlockSpec((tm, NSA_Q), row),
                   pl.BlockSpec((tm, 2 * NSA_KVW), row), pl.BlockSpec((tm, 2 * NSA_KVW), row),
                   pl.BlockSpec((tm, 2 * NSA_KVW), row), pl.BlockSpec((tm, LANES), row),
                   pl.BlockSpec((None, 4, NSA_KV, tm, NSA_DH), lambda i: (i // tpb, 0, 0, i % tpb, 0))],
        out_shape=[jax.ShapeDtypeStruct((n, NSA_Q), BF16), jax.ShapeDtypeStruct((n, NSA_Q), BF16),
                   jax.ShapeDtypeStruct((n, 2 * NSA_KVW), F32), jax.ShapeDtypeStruct((n, 2 * NSA_KVW), F32),
                   jax.ShapeDtypeStruct((n, 2 * NSA_KVW), F32), jax.ShapeDtypeStruct((n, LANES), F32),
                   jax.ShapeDtypeStruct((n_batch, 4, NSA_KV, t_len, NSA_DH), BF16)],
        compiler_params=_cparams("parallel"),
        name="nsa_proj",
    )(x, g, w, *tables)


def _nsa_w_in_padded(w_in):
    pad = _NSA_W_COLS - w_in.shape[1]
    return jnp.pad(w_in, ((0, 0), (0, pad))).astype(BF16)


def _cmp_weights(cmp_w, cmp_pe):
    z = jnp.zeros_like(cmp_w)
    bd = jnp.concatenate([jnp.concatenate([cmp_w, z], axis=-1), jnp.concatenate([z, cmp_w], axis=-1)], axis=-2)
    return bd.astype(BF16), jnp.concatenate([cmp_pe, cmp_pe], axis=-1)


def _compress_planes(load_rows, bd_ref, pe_ref, o_ref, n_rows_out):
    for plane in range(4):
        kv = plane // 2
        acc = jnp.zeros((n_rows_out, LANES), F32)
        bias = jnp.zeros((SUBLANES, LANES), F32)
        for l in range(L_CMP):
            w = bd_ref[kv, l]
            acc = acc + _dot(load_rows(plane, l).astype(BF16), w)
            pe = jnp.broadcast_to(pe_ref[kv, l:l + 1, :], (SUBLANES, LANES)).astype(BF16)
            bias = bias + _dot(pe, w)
        out = acc + bias[0:1, :]
        g0 = 2 * (plane % 2)
        o_ref[kv, g0] = out[:, :NSA_DH]
        o_ref[kv, g0 + 1] = out[:, NSA_DH:]


def _nsa_compress_prompt_kernel(x0_ref, x1_ref, x2_ref, x3_ref, bd_ref, pe_ref, o_ref, *, n_blk, n_pad):
    planes = (x0_ref, x1_ref, x2_ref, x3_ref)
    if n_pad > n_blk:
        o_ref[...] = jnp.zeros(o_ref.shape, F32)

    def load_rows(plane, l):
        return planes[plane][pl.ds(l, n_blk, stride=L_CMP), :]

    if n_pad == n_blk:
        _compress_planes(load_rows, bd_ref, pe_ref, o_ref, n_blk)
    else:
        _compress_planes(load_rows, bd_ref, pe_ref, o_ref.at[:, :, 0:n_blk, :], n_blk)


def _nsa_compress_prompt(cmp_rows, bd, pe2, n_batch):
    n = cmp_rows.shape[0]
    t_len = n // n_batch
    n_blk = t_len // L_CMP
    n_pad = max(n_blk, LANES)
    planes = [pl.BlockSpec((t_len, LANES), functools.partial(lambda b, c: (b, c), c=c)) for c in range(4)]
    return pl.pallas_call(
        functools.partial(_nsa_compress_prompt_kernel, n_blk=n_blk, n_pad=n_pad),
        grid=(n_batch,),
        in_specs=planes + [_resident(bd.shape), _resident(pe2.shape)],
        out_specs=pl.BlockSpec((None, 2, NSA_KV, n_pad, NSA_DH), lambda b: (b, 0, 0, 0, 0)),
        out_shape=jax.ShapeDtypeStruct((n_batch, 2, NSA_KV, n_pad, NSA_DH), F32),
        compiler_params=_cparams("parallel"),
        name="nsa_compress_prompt",
    )(cmp_rows, cmp_rows, cmp_rows, cmp_rows, bd, pe2)


def _stack_heads(x):
    return jnp.concatenate([x[:, j * NSA_DH:(j + 1) * NSA_DH] for j in range(NSA_HPG)], axis=0)


def _softmax_rows(s):
    m = jnp.max(s, axis=-1, keepdims=True)
    e = jnp.exp(s - m)
    return e / jnp.sum(e, axis=-1, keepdims=True)


def _select_blocks(imp_t, t0, rank_scr):
    n_sel_pad = LANES // 2
    tq = imp_t.shape[1]
    rank_scr[...] = imp_t
    pair = rank_scr[pl.ds(0, n_sel_pad, stride=2), :] + rank_scr[pl.ds(1, n_sel_pad, stride=2), :]
    blk = _iota((n_sel_pad, tq), 0)
    tpos = t0 + _iota((n_sel_pad, tq), 1)
    forced = (blk == 0) | (blk == (tpos >> _LOG_SEL))
    avail = blk * L_SEL <= tpos
    val = jnp.where(forced, FORCE, jnp.where(avail, pair, -1.0))
    rank_scr[0:n_sel_pad, :] = val
    cnt = jnp.zeros((n_sel_pad, tq), F32)
    for s in range(n_sel_pad):
        row = jnp.broadcast_to(rank_scr[pl.ds(s, 1), :], (n_sel_pad, tq))
        ahead = (row > val) | ((row == val) & (blk > s))
        cnt = cnt + jnp.where(ahead, 1.0, 0.0)
    sel = jnp.where(cnt < float(N_SEL), 1.0, 0.0)
    return jnp.concatenate([sel, jnp.zeros((LANES - n_sel_pad, tq), F32)], axis=0)


def _nsa_attn_kernel(q_ref, qr_ref, gate_ref, kvc_ref, kvb_ref, o_ref, rank_scr, *, tq, tk):
    g = pl.program_id(1)
    t0 = pl.program_id(2) * tq
    rows = NSA_HPG * tq
    q4 = _stack_heads(q_ref[...])
    qr4 = _stack_heads(qr_ref[...])

    n_cmp = kvc_ref.shape[1]
    s = _dot_nt(q4, kvc_ref[0].astype(BF16))
    tpos_c = t0 + (_iota((rows, n_cmp), 0) & (tq - 1))
    c_end = (_iota((rows, n_cmp), 1) + 1) * L_CMP - 1
    p = _softmax_rows(jnp.where(c_end <= tpos_c, s, NEG))
    p = p * jnp.where(tpos_c >= L_CMP - 1, 1.0, 0.0)
    o_c = _dot(p.astype(BF16), kvc_ref[1].astype(BF16))
    imp = p[0:tq] + p[tq:2 * tq] + p[2 * tq:3 * tq] + p[3 * tq:4 * tq]

    sel_t = _select_blocks(imp.T, t0, rank_scr)
    sel = jnp.where(sel_t.T > 0.5, 1.0, 0.0).astype(BF16)

    def sel_tile(kt, carry):
        m, l, acc = carry
        k0 = pl.multiple_of(kt * tk, tk)
        k = kvb_ref[0, pl.ds(k0, tk), :]
        v = kvb_ref[1, pl.ds(k0, tk), :]
        sc = _dot_nt(qr4, k)
        expand = jnp.where(_iota((LANES, tk), 0) == ((k0 + _iota((LANES, tk), 1)) >> _LOG_SEL), 1.0, 0.0).astype(BF16)
        picked = _dot(sel, expand)
        kpos = k0 + _iota((tq, tk), 1)
        tpos = t0 + _iota((tq, tk), 0)
        bias = jnp.where((picked > 0.5) & (kpos <= tpos), 0.0, NEG)
        sc = (sc.reshape(NSA_HPG, tq, tk) + bias[None]).reshape(rows, tk)
        m_new = jnp.maximum(m, jnp.max(sc, axis=-1, keepdims=True))
        alpha = jnp.exp(m - m_new)
        pe = jnp.exp(sc - m_new)
        l = alpha * l + jnp.sum(pe, axis=-1, keepdims=True)
        acc = alpha * acc + _dot(pe.astype(BF16), v)
        return m_new, l, acc

    n_kt = (t0 + tq + tk - 1) // tk
    init = (jnp.full((rows, 1), NEG, F32), jnp.zeros((rows, 1), F32), jnp.zeros((rows, NSA_DH), F32))
    _, l_s, acc_s = lax.fori_loop(0, n_kt, sel_tile, init)
    o_s = acc_s / l_s

    wk = WINDOW + tq
    w0 = pl.multiple_of(jnp.maximum(t0 - WINDOW, 0), tq)
    kw = kvb_ref[2, pl.ds(w0, wk), :]
    vw = kvb_ref[3, pl.ds(w0, wk), :]
    sw = _dot_nt(qr4, kw)
    dist = (t0 + (_iota((rows, wk), 0) & (tq - 1))) - (w0 + _iota((rows, wk), 1))
    pw = _softmax_rows(jnp.where((dist >= 0) & (dist <= WINDOW), sw, NEG))
    o_w = _dot(pw.astype(BF16), vw)

    gates = gate_ref[...]
    lane = _iota((tq, LANES), 1)
    outs = []
    for j in range(NSA_HPG):
        o = jnp.zeros((tq, NSA_DH), F32)
        for br, ob in enumerate((o_c, o_s, o_w)):
            col = br * NSA_HEADS + g * NSA_HPG + j
            gcol = jnp.sum(jnp.where(lane == col, gates, 0.0), axis=-1, keepdims=True)
            o = o + gcol * ob[j * tq:(j + 1) * tq]
        outs.append(o)
    o_ref[...] = jnp.concatenate(outs, axis=1).astype(BF16)


def _nsa_attn_prompt(q, qr, gates, kvc, kvb, n_batch, tq=128, tk=512):
    n = q.shape[0]
    t_len = n // n_batch
    nq = t_len // tq
    tk = min(tk, t_len)
    assert kvc.shape[3] == LANES and t_len >= WINDOW + tq and t_len % tk == 0
    qspec = pl.BlockSpec((tq, NSA_HPG * NSA_DH), lambda b, g, i: (b * nq + i, g))
    return pl.pallas_call(
        functools.partial(_nsa_attn_kernel, tq=tq, tk=tk),
        grid=(n_batch, NSA_KV, nq),
        in_specs=[qspec, qspec,
                  pl.BlockSpec((tq, LANES), lambda b, g, i: (b * nq + i, 0)),
                  pl.BlockSpec((None, 2, None, LANES, NSA_DH), lambda b, g, i: (b, 0, g, 0, 0)),
                  pl.BlockSpec((None, 4, None, t_len, NSA_DH), lambda b, g, i: (b, 0, g, 0, 0))],
        out_specs=qspec,
        out_shape=jax.ShapeDtypeStruct((n, NSA_Q), BF16),
        scratch_shapes=[pltpu.VMEM((LANES, tq), F32)],
        compiler_params=_cparams("parallel", "parallel", "arbitrary"),
        name="nsa_attn_prompt",
    )(q, qr, gates, kvc, kvb)


def _matmul_residual_kernel(a_ref, w_ref, x_ref, o_ref):
    o_ref[...] = x_ref[...] + _dot(a_ref[...].astype(BF16), w_ref[...])


def _matmul_residual(a, w, x, tm):
    n, k = a.shape
    row = lambda i: (i, 0)
    return pl.pallas_call(
        _matmul_residual_kernel,
        grid=(n // tm,),
        in_specs=[pl.BlockSpec((tm, k), row), _resident(w.shape), pl.BlockSpec((tm, D_MODEL), row)],
        out_specs=pl.BlockSpec((tm, D_MODEL), row),
        out_shape=jax.ShapeDtypeStruct((n, D_MODEL), F32),
        compiler_params=_cparams("parallel"),
        name="matmul_residual",
    )(a, w, x)


def _nsa_prompt_layer(x, n_batch, norm_g, w_in, cmp_pe, cmp_w, w_out, tables):
    tm = min(512, x.shape[0] // n_batch)
    q, qr, cmp_rows, sel_rows, win_rows, gates, kvb = _nsa_proj(
        x, norm_g.reshape(1, D_MODEL), _nsa_w_in_padded(w_in), tables, n_batch, tm)
    bd, pe2 = _cmp_weights(cmp_w, cmp_pe)
    kvc = _nsa_compress_prompt(cmp_rows, bd, pe2, n_batch)
    o = _nsa_attn_prompt(q, qr, gates, kvc, kvb, n_batch)
    y = _matmul_residual(o, w_out.astype(BF16), x, tm)
    return y, cmp_rows, sel_rows, win_rows


_FFN_CW = MXU_DIM


def _ffn_kernel(x_ref, g_ref, win_ref, cw_ref, wout_ref, buf_ref, y_ref, st_ref, carry_scr, uv_scr, ug_scr,
                *, tm, tpb):
    ti = pl.program_id(0) % tpb
    x = x_ref[...]
    xn = _rmsnorm(x, g_ref[...]).astype(BF16)

    @pl.when(ti == 0)
    def _():
        carry_scr[SUBLANES - (FFN_CONV - 1):SUBLANES, :] = buf_ref[...]

    acc = jnp.zeros((tm, D_MODEL), F32)
    for c in range(D_FF // _FFN_CW):
        conv = []
        for part, scr in ((0, uv_scr), (1, ug_scr)):
            c0 = part * D_FF + c * _FFN_CW
            u = _dot(xn, win_ref[:, c0:c0 + _FFN_CW])
            scr[0:SUBLANES, :] = carry_scr[:, c0:c0 + _FFN_CW]
            scr[SUBLANES:SUBLANES + tm, :] = u
            carry_scr[:, c0:c0 + _FFN_CW] = u[tm - SUBLANES:tm, :]
            w = cw_ref[:, c0:c0 + _FFN_CW]
            conv.append(scr[SUBLANES - 2:SUBLANES - 2 + tm, :] * w[0:1] + scr[SUBLANES - 1:SUBLANES - 1 + tm, :] * w[1:2]
                        + u * w[2:3])
        act = conv[0] * jax.nn.silu(conv[1])
        acc = acc + _dot(act.astype(BF16), wout_ref[c * _FFN_CW:(c + 1) * _FFN_CW, :])
    y_ref[...] = x + acc

    @pl.when(ti == tpb - 1)
    def _():
        st_ref[...] = carry_scr[SUBLANES - (FFN_CONV - 1):SUBLANES, :]


def _ffn_prompt(x, n_batch, norm_g, w_in, conv_w, w_out, buf, tm=512):
    n = x.shape[0]
    tpb = n // n_batch // tm
    row = lambda i: (i, 0)
    st = pl.BlockSpec((None, FFN_CONV - 1, 2 * D_FF), lambda i: (i // tpb, 0, 0))
    return pl.pallas_call(
        functools.partial(_ffn_kernel, tm=tm, tpb=tpb),
        grid=(n // tm,),
        in_specs=[pl.BlockSpec((tm, D_MODEL), row), _resident((1, D_MODEL)), _resident(w_in.shape),
                  _resident(conv_w.shape), _resident(w_out.shape), st],
        out_specs=[pl.BlockSpec((tm, D_MODEL), row), st],
        out_shape=[jax.ShapeDtypeStruct((n, D_MODEL), F32),
                   jax.ShapeDtypeStruct((n_batch, FFN_CONV - 1, 2 * D_FF), F32)],
        scratch_shapes=[pltpu.VMEM((SUBLANES, 2 * D_FF), F32), pltpu.VMEM((SUBLANES + tm, _FFN_CW), F32),
                        pltpu.VMEM((SUBLANES + tm, _FFN_CW), F32)],
        compiler_params=_cparams("arbitrary"),
        name="ffn_prompt",
    )(x, norm_g.reshape(1, D_MODEL), w_in, conv_w, w_out, buf)


_GDN_W_COLS = GDN_CONV_CH + GDN_VW + LANES
_GDN_CW = MXU_DIM


def _softplus(x):
    return jnp.maximum(x, 0.0) + jnp.log1p(jnp.exp(-jnp.abs(x)))


def _gdn_w_in_padded(w_in):
    return jnp.pad(w_in, ((0, 0), (0, _GDN_W_COLS - w_in.shape[1]))).astype(BF16)


def _lane_row(v):
    return jnp.pad(v.astype(F32), (0, LANES - v.shape[0])).reshape(1, LANES)


def _gdn_post_conv(y, c0):
    y = jax.nn.silu(y)
    kind = c0 // GDN_QK
    if kind == 2:
        return y
    outs = []
    for hh in range(_GDN_CW // GDN_DK):
        seg = y[:, hh * GDN_DK:(hh + 1) * GDN_DK]
        seg = seg * lax.rsqrt(jnp.sum(seg * seg, axis=-1, keepdims=True) + EPS)
        outs.append(seg * GDN_DK ** -0.5 if kind == 0 else seg)
    return jnp.concatenate(outs, axis=1)


def _gdn_gates(ab, alog, dtb):
    lane = _iota(ab.shape, 1)
    gval = -jnp.exp(alog) * _softplus(ab + dtb)
    return jnp.where(lane < GDN_HEADS, gval, jax.nn.sigmoid(ab))


def _gdn_proj_kernel(x_ref, g_ref, w_ref, cw_ref, alog_ref, dtb_ref, buf_ref,
                     q_ref, k_ref, v_ref, gate_ref, gb_ref, st_ref, carry_scr, u_scr, *, tm, tpb):
    ti = pl.program_id(0) % tpb
    xn = _rmsnorm(x_ref[...], g_ref[...]).astype(BF16)
    keep = GDN_CONV - 1

    @pl.when(ti == 0)
    def _():
        carry_scr[SUBLANES - keep:SUBLANES, :] = buf_ref[...]

    outs = (q_ref, k_ref, v_ref)
    for c in range(GDN_CONV_CH // _GDN_CW):
        c0 = c * _GDN_CW
        u = _dot(xn, w_ref[:, c0:c0 + _GDN_CW])
        u_scr[0:SUBLANES, :] = carry_scr[:, c0:c0 + _GDN_CW]
        u_scr[SUBLANES:SUBLANES + tm, :] = u
        carry_scr[:, c0:c0 + _GDN_CW] = u[tm - SUBLANES:tm, :]
        w = cw_ref[:, c0:c0 + _GDN_CW]
        y = u_scr[SUBLANES - 3:SUBLANES - 3 + tm, :] * w[0:1]
        y = y + u_scr[SUBLANES - 2:SUBLANES - 2 + tm, :] * w[1:2]
        y = y + u_scr[SUBLANES - 1:SUBLANES - 1 + tm, :] * w[2:3]
        y = y + u * w[3:4]
        o_ref = outs[c0 // GDN_QK]
        o_ref[:, c0 % GDN_QK:c0 % GDN_QK + _GDN_CW] = _gdn_post_conv(y, c0)
    gate_ref[...] = _dot(xn, w_ref[:, GDN_CONV_CH:GDN_CONV_CH + GDN_VW])
    ab = _dot(xn, w_ref[:, GDN_CONV_CH + GDN_VW:_GDN_W_COLS])
    gb_ref[...] = _gdn_gates(ab, alog_ref[...], dtb_ref[...])

    @pl.when(ti == tpb - 1)
    def _():
        st_ref[...] = carry_scr[SUBLANES - keep:SUBLANES, :]


def _gdn_proj(x, n_batch, norm_g, w, conv_w, alog, dtb, buf, tm=256):
    n = x.shape[0]
    tpb = n // n_batch // tm
    row = lambda i: (i, 0)
    wide = pl.BlockSpec((tm, GDN_QK), row)
    st = pl.BlockSpec((None, GDN_CONV - 1, GDN_CONV_CH), lambda i: (i // tpb, 0, 0))
    f = jax.ShapeDtypeStruct((n, GDN_QK), F32)
    return pl.pallas_call(
        functools.partial(_gdn_proj_kernel, tm=tm, tpb=tpb),
        grid=(n // tm,),
        in_specs=[pl.BlockSpec((tm, D_MODEL), row), _resident((1, D_MODEL)), _resident(w.shape),
                  _resident(conv_w.shape), _resident((1, LANES)), _resident((1, LANES)), st],
        out_specs=[wide, wide, wide, wide, pl.BlockSpec((tm, LANES), row), st],
        out_shape=[f, f, f, f, jax.ShapeDtypeStruct((n, LANES), F32),
                   jax.ShapeDtypeStruct((n_batch, GDN_CONV - 1, GDN_CONV_CH), F32)],
        scratch_shapes=[pltpu.VMEM((SUBLANES, GDN_CONV_CH), F32), pltpu.VMEM((SUBLANES + tm, _GDN_CW), F32)],
        compiler_params=_cparams("arbitrary"),
        name="gdn_proj",
    )(x, norm_g.reshape(1, D_MODEL), w, conv_w, alog, dtb, buf)


_GDN_HG = MXU_DIM // GDN_CHUNK


def _gdn_block_diag(x):
    head = _iota(x.shape, 0) // GDN_CHUNK
    return jnp.concatenate([jnp.where(head == h, x, 0.0) for h in range(_GDN_HG)], axis=1)


def _gdn_group_prep(q, k, v, beta, gcum_col, gcum_row, g_last):
    n = q.shape[0]
    ii = _iota((n, n), 0)
    jj = _iota((n, n), 1)
    same = (ii // GDN_CHUNK) == (jj // GDN_CHUNK)
    decay = jnp.exp(jnp.where(same & (ii >= jj), gcum_col - gcum_row, NEG))
    kb = k * beta
    k16 = k.astype(BF16)
    low = _dot_nt(kb.astype(BF16), k16) * decay * jnp.where(ii > jj, 1.0, 0.0)
    attn = _dot_nt(q.astype(BF16), k16) * decay
    x = jnp.where(ii == jj, 1.0, 0.0) - low
    p = low
    for _ in range(GDN_CHUNK.bit_length() - 2):
        p16 = p.astype(BF16)
        p = _dot(p16, p16)
        x = x + _dot(x.astype(BF16), p.astype(BF16))
    rhs = jnp.concatenate([v * beta, kb * jnp.exp(gcum_col)], axis=1)
    sol = _dot(x.astype(BF16), rhs.astype(BF16))
    u_base, w_dec = sol[:, :GDN_DV], sol[:, GDN_DV:]
    return (u_base, _gdn_block_diag(w_dec).astype(BF16), _gdn_block_diag(q * jnp.exp(gcum_col)).astype(BF16),
            attn.astype(BF16), _gdn_block_diag(k * jnp.exp(g_last - gcum_col)).astype(BF16))


def _gdn_group_step(prep, s_prev, s_decay):
    u_base, w_dec_bd, q_bd, attn, k_dec_bd = prep
    s16 = s_prev.astype(BF16)
    u = u_base - _dot(w_dec_bd, s16)
    u16 = u.astype(BF16)
    o = _dot(q_bd, s16) + _dot(attn, u16)
    s_new = s_prev * s_decay + _dot_tn(k_dec_bd, u16)
    return o, s_new


def _gdn_chunk_kernel(q_ref, k_ref, v_ref, gb_ref, gate_ref, x_ref, s0_ref, ng_ref, wout_ref,
                      y_ref, s_ref, s_scr, o_scr, *, rt, tpb):
    ti = pl.program_id(1)
    n_groups = GDN_HEADS // _GDN_HG

    @pl.when(ti == 0)
    def _():
        s_scr[...] = s0_ref[...].reshape(n_groups, _GDN_HG * GDN_DK, GDN_DV)

    c = GDN_CHUNK
    tri = jnp.where(_iota((c, c), 0) >= _iota((c, c), 1), 1.0, 0.0)
    stack_rows = lambda ref, r0, h0: jnp.concatenate(
        [ref[r0:r0 + c, (h0 + h) * GDN_DK:(h0 + h + 1) * GDN_DK] for h in range(_GDN_HG)], axis=0)

    preps, decays = [], []
    for ci in range(rt // c):
        r0 = ci * c
        gb = gb_ref[r0:r0 + c, :]
        gcum = _dot_f32(tri, gb)
        gcum_t = gcum.T
        for gi in range(n_groups):
            h0 = gi * _GDN_HG
            heads = range(h0, h0 + _GDN_HG)
            col = lambda a: jnp.concatenate([a[:, h:h + 1] for h in heads], axis=0)
            g_last = jnp.concatenate([jnp.broadcast_to(gcum[c - 1:c, h:h + 1], (c, 1)) for h in heads], axis=0)
            gcum_row = jnp.concatenate([gcum_t[h:h + 1, :] for h in heads], axis=1)
            beta = jnp.concatenate([gb[:, GDN_HEADS + h:GDN_HEADS + h + 1] for h in heads], axis=0)
            preps.append(_gdn_group_prep(stack_rows(q_ref, r0, h0), stack_rows(k_ref, r0, h0),
                                         stack_rows(v_ref, r0, h0), beta, col(gcum), gcum_row, g_last))
            decays.append(jnp.concatenate(
                [jnp.broadcast_to(jnp.exp(gcum[c - 1:c, h:h + 1]), (GDN_DK, 1)) for h in heads], axis=0))

    for gi in range(n_groups):
        s = s_scr[gi]
        for ci in range(rt // c):
            o, s = _gdn_group_step(preps[ci * n_groups + gi], s, decays[ci * n_groups + gi])
            for h in range(_GDN_HG):
                o_scr[ci * c:(ci + 1) * c, (gi * _GDN_HG + h) * GDN_DV:(gi * _GDN_HG + h + 1) * GDN_DV] = (
                    o[h * c:(h + 1) * c])
        s_scr[gi] = s

    ng = ng_ref[...]
    outs = []
    for h in range(GDN_HEADS):
        cols = slice(h * GDN_DV, (h + 1) * GDN_DV)
        outs.append((_rmsnorm(o_scr[:, cols], ng) * jax.nn.silu(gate_ref[:, cols])).astype(BF16))
    y_ref[...] = x_ref[...] + _dot(jnp.concatenate(outs, axis=1), wout_ref[...])

    @pl.when(ti == tpb - 1)
    def _():
        s_ref[...] = s_scr[...].reshape(GDN_HEADS, GDN_DK, GDN_DV)


def _gdn_chunked(q, k, v, gb, gate, x, s0, norm_g, w_out, n_batch, rt=256):
    n = x.shape[0]
    tpb = n // n_batch // rt
    row = lambda b, i: (b * tpb + i, 0)
    wide = pl.BlockSpec((rt, GDN_QK), row)
    st = pl.BlockSpec((None, GDN_HEADS, GDN_DK, GDN_DV), lambda b, i: (b, 0, 0, 0))
    return pl.pallas_call(
        functools.partial(_gdn_chunk_kernel, rt=rt, tpb=tpb),
        grid=(n_batch, tpb),
        in_specs=[wide, wide, wide, pl.BlockSpec((rt, LANES), row), wide, wide, st,
                  _resident((1, GDN_DV)), _resident(w_out.shape)],
        out_specs=[wide, st],
        out_shape=[jax.ShapeDtypeStruct((n, D_MODEL), F32),
                   jax.ShapeDtypeStruct((n_batch, GDN_HEADS, GDN_DK, GDN_DV), F32)],
        scratch_shapes=[pltpu.VMEM((GDN_HEADS // _GDN_HG, _GDN_HG * GDN_DK, GDN_DV), F32),
                        pltpu.VMEM((rt, GDN_VW), F32)],
        compiler_params=_cparams("parallel", "arbitrary"),
        name="gdn_chunked",
    )(q, k, v, gb, gate, x, s0, norm_g.reshape(1, GDN_DV), w_out)


def _gdn_prompt_layer(x, n_batch, norm_g, s0, conv_buf, w_in, conv_w, a_log, dt_bias, head_norm_g, w_out):
    q, k, v, gate, gb, new_buf = _gdn_proj(x, n_batch, norm_g, _gdn_w_in_padded(w_in), conv_w,
                                           _lane_row(a_log), _lane_row(dt_bias), conv_buf)
    y, s_new = _gdn_chunked(q, k, v, gb, gate, x, s0, head_norm_g, w_out.astype(BF16), n_batch)
    return y, s_new, new_buf


_LRU_CW = MXU_DIM


def _lru_gates(xc, wga_ref, bga, wgx_ref, bgx, lam, n0):
    r_parts, i_parts = [], []
    for j in range(xc.shape[1] // LRU_BW):
        blk = xc[:, j * LRU_BW:(j + 1) * LRU_BW].astype(BF16)
        r_parts.append(_dot(blk, wga_ref[n0 + j]))
        i_parts.append(_dot(blk, wgx_ref[n0 + j]))
    cat = lambda ps: ps[0] if len(ps) == 1 else jnp.concatenate(ps, axis=1)
    r = jax.nn.sigmoid(cat(r_parts) + bga)
    i = jax.nn.sigmoid(cat(i_parts) + bgx)
    log_a = -LRU_C * r * _softplus(-lam)
    a = jnp.exp(log_a)
    b = jnp.sqrt(-jnp.tanh(log_a) * (a * a + 1.0)) * i * xc
    return a, b


def _lru_proj_kernel(x_ref, g_ref, w_ref, cw_ref, cb_ref, wga_ref, bga_ref, wgx_ref, bgx_ref, lam_ref, buf_ref,
                     gate_ref, a_ref, b_ref, st_ref, carry_scr, u_scr, *, tm, tpb):
    ti = pl.program_id(0) % tpb
    xn = _rmsnorm(x_ref[...], g_ref[...]).astype(BF16)
    keep = LRU_CONV - 1

    @pl.when(ti == 0)
    def _():
        carry_scr[SUBLANES - keep:SUBLANES, :] = buf_ref[...]

    gate_ref[...] = _dot(xn, w_ref[:, 0:D_RNN])
    for c in range(D_RNN // _LRU_CW):
        c0 = c * _LRU_CW
        cols = slice(c0, c0 + _LRU_CW)
        u = _dot(xn, w_ref[:, D_RNN + c0:D_RNN + c0 + _LRU_CW])
        u_scr[0:SUBLANES, :] = carry_scr[:, cols]
        u_scr[SUBLANES:SUBLANES + tm, :] = u
        carry_scr[:, cols] = u[tm - SUBLANES:tm, :]
        w = cw_ref[:, cols]
        xc = u_scr[SUBLANES - 3:SUBLANES - 3 + tm, :] * w[0:1]
        xc = xc + u_scr[SUBLANES - 2:SUBLANES - 2 + tm, :] * w[1:2]
        xc = xc + u_scr[SUBLANES - 1:SUBLANES - 1 + tm, :] * w[2:3]
        xc = xc + u * w[3:4] + cb_ref[:, cols]
        a, b = _lru_gates(xc, wga_ref, bga_ref[:, cols], wgx_ref, bgx_ref[:, cols], lam_ref[:, cols],
                          c0 // LRU_BW)
        a_ref[:, cols] = a
        b_ref[:, cols] = b

    @pl.when(ti == tpb - 1)
    def _():
        st_ref[...] = carry_scr[SUBLANES - keep:SUBLANES, :]


def _lru_proj(x, n_batch, norm_g, w, conv_w, conv_b, w_ga, b_ga, w_gx, b_gx, lam, buf, tm=256):
    n = x.shape[0]
    tpb = n // n_batch // tm
    row = lambda i: (i, 0)
    wide = pl.BlockSpec((tm, D_RNN), row)
    vec = _resident((1, D_RNN))
    st = pl.BlockSpec((None, LRU_CONV - 1, D_RNN), lambda i: (i // tpb, 0, 0))
    f = jax.ShapeDtypeStruct((n, D_RNN), F32)
    r1 = lambda v: v.reshape(1, D_RNN)
    return pl.pallas_call(
        functools.partial(_lru_proj_kernel, tm=tm, tpb=tpb),
        grid=(n // tm,),
        in_specs=[pl.BlockSpec((tm, D_MODEL), row), _resident((1, D_MODEL)), _resident(w.shape),
                  _resident(conv_w.shape), vec, _resident(w_ga.shape), vec, _resident(w_gx.shape), vec, vec, st],
        out_specs=[wide, wide, wide, st],
        out_shape=[f, f, f, jax.ShapeDtypeStruct((n_batch, LRU_CONV - 1, D_RNN), F32)],
        scratch_shapes=[pltpu.VMEM((SUBLANES, D_RNN), F32), pltpu.VMEM((SUBLANES + tm, _LRU_CW), F32)],
        compiler_params=_cparams("arbitrary"),
        name="lru_proj",
    )(x, norm_g.reshape(1, D_MODEL), w, conv_w, r1(conv_b), w_ga, r1(b_ga), w_gx, r1(b_gx), r1(lam), buf)


def _lru_scan_kernel(a_ref, b_ref, gate_ref, x_ref, h0_ref, wout_ref, y_ref, hl_ref, h_scr, hs_scr, *, rt, tpb):
    ti = pl.program_id(1)

    @pl.when(ti == 0)
    def _():
        h_scr[...] = h0_ref[...]

    def step(t, h):
        h = a_ref[pl.ds(t, 1), :] * h + b_ref[pl.ds(t, 1), :]
        hs_scr[pl.ds(t, 1), :] = h
        return h

    h = lax.fori_loop(0, rt, step, h_scr[...], unroll=8)
    h_scr[...] = h
    y = (hs_scr[...] * jax.nn.gelu(gate_ref[...])).astype(BF16)
    y_ref[...] = x_ref[...] + _dot(y, wout_ref[...])

    @pl.when(ti == tpb - 1)
    def _():
        hl_ref[...] = h


def _lru_scan(a, b, gate, x, h0, w_out, n_batch, rt=256):
    n = x.shape[0]
    tpb = n // n_batch // rt
    wide = pl.BlockSpec((rt, D_RNN), lambda bb, i: (bb * tpb + i, 0))
    st = pl.BlockSpec((None, 1, D_RNN), lambda bb, i: (bb, 0, 0))
    y, hl = pl.pallas_call(
        functools.partial(_lru_scan_kernel, rt=rt, tpb=tpb),
        grid=(n_batch, tpb),
        in_specs=[wide, wide, wide, wide, st, _resident(w_out.shape)],
        out_specs=[wide, st],
        out_shape=[jax.ShapeDtypeStruct((n, D_MODEL), F32), jax.ShapeDtypeStruct((n_batch, 1, D_RNN), F32)],
        scratch_shapes=[pltpu.VMEM((1, D_RNN), F32), pltpu.VMEM((rt, D_RNN), F32)],
        compiler_params=_cparams("parallel", "arbitrary"),
        name="lru_scan",
    )(a, b, gate, x, h0.reshape(n_batch, 1, D_RNN), w_out)
    return y, hl.reshape(n_batch, D_RNN)


def _lru_prompt_layer(x, n_batch, norm_g, h0, conv_buf, w_in, conv_w, conv_b, w_ga, b_ga, w_gx, b_gx, lam, w_out):
    gate, a, b, new_buf = _lru_proj(x, n_batch, norm_g, w_in.astype(BF16), conv_w, conv_b, w_ga.astype(BF16), b_ga,
                                    w_gx.astype(BF16), b_gx, lam, conv_buf)
    y, h_last = _lru_scan(a, b, gate, x, h0, w_out.astype(BF16), n_batch)
    return y, h_last, new_buf


def _final_norm_kernel(x_ref, g_ref, o_ref):
    o_ref[...] = _rmsnorm(x_ref[...], g_ref[...])


def _final_norm(x, g, tm):
    n = x.shape[0]
    row = lambda i: (i, 0)
    return pl.pallas_call(
        _final_norm_kernel,
        grid=(n // tm,),
        in_specs=[pl.BlockSpec((tm, D_MODEL), row), _resident((1, D_MODEL))],
        out_specs=pl.BlockSpec((tm, D_MODEL), row),
        out_shape=jax.ShapeDtypeStruct((n, D_MODEL), F32),
        compiler_params=_cparams("parallel"),
        name="final_norm",
    )(x, g.reshape(1, D_MODEL))


def _norm_matmul_small_kernel(x_ref, g_ref, w_ref, o_ref):
    xn = _rmsnorm(x_ref[...], g_ref[...]).astype(BF16)
    o_ref[...] = _dot(xn, w_ref[...])


def _norm_matmul_small(x, g, w, n_col_tiles):
    n, k = x.shape[0], w.shape[1]
    tn = k // n_col_tiles
    return pl.pallas_call(
        _norm_matmul_small_kernel,
        grid=(n_col_tiles,),
        in_specs=[_resident((n, D_MODEL)), _resident((1, D_MODEL)), pl.BlockSpec((D_MODEL, tn), lambda j: (0, j))],
        out_specs=pl.BlockSpec((n, tn), lambda j: (0, j)),
        out_shape=jax.ShapeDtypeStruct((n, k), F32),
        compiler_params=_cparams("parallel"),
        name="norm_matmul_small",
    )(x, g.reshape(1, D_MODEL), w)


_CMP_PAGES_PER_STEP = 8
_CMP_BLK_PER_PAGE = PAGE_SIZE // L_CMP


def _nsa_compress_sample_kernel(pt_ref, *refs, n_pages, n_blk, n_out):
    pages = refs[:_CMP_PAGES_PER_STEP]
    new_ref, bd_ref, pe_ref, o_ref, rows_scr = refs[_CMP_PAGES_PER_STEP:]
    b = pl.program_id(0)
    pg = pl.program_id(1)
    past = n_pages * PAGE_SIZE
    n_rows = rows_scr.shape[1]
    for r in range(_CMP_PAGES_PER_STEP):
        r0 = pl.multiple_of((pg * _CMP_PAGES_PER_STEP + r) * PAGE_SIZE, PAGE_SIZE)
        for c in range(4):
            two = pages[r][c // 2, 2 * (c % 2):2 * (c % 2) + 2].reshape(2 * NSA_DH, PAGE_SIZE)
            rows_scr[c, pl.ds(r0, PAGE_SIZE), :] = two.T

    @pl.when(pg == pl.num_programs(1) - 1)
    def _():
        new = new_ref[pl.ds(b, 1), :]
        tail = _iota((n_rows - past, LANES), 0)
        for c in range(4):
            rows_scr[c, past:n_rows, :] = jnp.where(tail == 0, new[:, c * LANES:(c + 1) * LANES], 0.0)
        o_ref[...] = jnp.zeros(o_ref.shape, F32)

        def load_rows(plane, l):
            return rows_scr[plane, pl.ds(l, n_blk, stride=L_CMP), :]

        _compress_planes(load_rows, bd_ref, pe_ref, o_ref.at[:, :, 0:n_blk, :], n_blk)


def _rows_last(cache):
    nd = cache.ndim
    return jnp.transpose(cache, tuple(range(nd - 4)) + (nd - 3, nd - 2, nd - 1, nd - 4))


def _nsa_compress_sample(cache_t, layer, page_table, new_rows, bd, pe2):
    n_b, n_pages = page_table.shape
    past = n_pages * PAGE_SIZE
    n_blk = -(-(past + 1) // L_SEL) * L_SEL // L_CMP
    n_blk = -(-n_blk // SUBLANES) * SUBLANES
    n_out = -(-n_blk // LANES) * LANES
    cache4 = cache_t

    def page_spec(r):
        return pl.BlockSpec((None, None, 2, NSA_KV, NSA_DH, PAGE_SIZE),
                            lambda b, pg, pt: (layer, pt[b, pg * _CMP_PAGES_PER_STEP + r], 0, 0, 0, 0))

    grid_spec = pltpu.PrefetchScalarGridSpec(
        num_scalar_prefetch=1,
        grid=(n_b, n_pages // _CMP_PAGES_PER_STEP),
        in_specs=[page_spec(r) for r in range(_CMP_PAGES_PER_STEP)]
        + [pl.BlockSpec(new_rows.shape, lambda b, pg, pt: (0, 0)),
           pl.BlockSpec(bd.shape, lambda b, pg, pt: (0, 0, 0, 0)),
           pl.BlockSpec(pe2.shape, lambda b, pg, pt: (0, 0, 0))],
        out_specs=pl.BlockSpec((None, 2, NSA_KV, n_out, NSA_DH), lambda b, pg, pt: (b, 0, 0, 0, 0)),
        scratch_shapes=[pltpu.VMEM((4, n_blk * L_CMP, LANES), F32)])
    return pl.pallas_call(
        functools.partial(_nsa_compress_sample_kernel, n_pages=n_pages, n_blk=n_blk, n_out=n_out),
        grid_spec=grid_spec,
        out_shape=jax.ShapeDtypeStruct((n_b, 2, NSA_KV, n_out, NSA_DH), F32),
        compiler_params=_cparams("parallel", "arbitrary"),
        name="nsa_compress_sample",
    )(page_table, *([cache4] * _CMP_PAGES_PER_STEP), new_rows, bd, pe2)


def _pad_rows8(x):
    return jnp.concatenate([x, jnp.zeros((SUBLANES - x.shape[0], x.shape[1]), x.dtype)], axis=0)


def _nsa_cmp_topk_sample_kernel(q_ref, kvc_ref, oc_ref, idx_ref, *, qpos, n_sel):
    n_cmp = kvc_ref.shape[2]
    n_lane = 2 * LANES
    c_end = (_iota((SUBLANES, n_cmp), 1) + 1) * L_CMP - 1
    valid = c_end <= qpos
    any_valid = 1.0 if qpos >= L_CMP - 1 else 0.0
    pair_mat = jnp.where((_iota((n_cmp, n_lane), 0) >> 1) == _iota((n_cmp, n_lane), 1), 1.0, 0.0)
    imps = []
    for g in range(NSA_KV):
        q8 = _pad_rows8(q_ref[g * NSA_HPG:(g + 1) * NSA_HPG, :])
        s = _dot_nt(q8, kvc_ref[0, g].astype(BF16))
        p = _softmax_rows(jnp.where(valid, s, NEG)) * any_valid
        o_c = _dot(p.astype(BF16), kvc_ref[1, g].astype(BF16))
        oc_ref[g * NSA_HPG:(g + 1) * NSA_HPG, :] = o_c[0:NSA_HPG]
        imps.append(p[0:1] + p[1:2] + p[2:3] + p[3:4])
    imp = jnp.concatenate(imps + [jnp.zeros((SUBLANES - NSA_KV, n_cmp), F32)], axis=0)
    pair = _dot_f32(imp, pair_mat)
    blk = _iota((SUBLANES, n_lane), 1)
    forced = (blk == 0) | (blk == qpos // L_SEL)
    avail = blk * L_SEL <= qpos
    val = jnp.where(forced, FORCE, jnp.where(avail, pair, -1.0))
    val = jnp.where(blk < n_sel, val, -2.0)
    out = jnp.zeros((SUBLANES, LANES), I32)
    out_lane = _iota((SUBLANES, LANES), 1)
    for r in range(N_SEL):
        m = jnp.max(val, axis=-1, keepdims=True)
        first = jnp.min(jnp.where(val == m, blk, n_lane), axis=-1, keepdims=True)
        out = jnp.where(out_lane == r, first, out)
        val = jnp.where(blk == first, -3.0, val)
    idx_ref[...] = out


def _nsa_cmp_topk_sample(q3, kvc, qpos, n_sel):
    n_b = q3.shape[0]
    n_cmp = kvc.shape[3]
    return pl.pallas_call(
        functools.partial(_nsa_cmp_topk_sample_kernel, qpos=qpos, n_sel=n_sel),
        grid=(n_b,),
        in_specs=[pl.BlockSpec((None, NSA_HEADS, NSA_DH), lambda b: (b, 0, 0)),
                  pl.BlockSpec((None, 2, NSA_KV, n_cmp, NSA_DH), lambda b: (b, 0, 0, 0, 0))],
        out_specs=[pl.BlockSpec((None, NSA_HEADS, NSA_DH), lambda b: (b, 0, 0)),
                   pl.BlockSpec((None, SUBLANES, LANES), lambda b: (b, 0, 0))],
        out_shape=[jax.ShapeDtypeStruct((n_b, NSA_HEADS, NSA_DH), F32),
                   jax.ShapeDtypeStruct((n_b, SUBLANES, LANES), I32)],
        compiler_params=_cparams("parallel"),
        name="nsa_cmp_topk_sample",
    )(q3, kvc)


def _nsa_gather_sel_kernel(pt_ref, idx_ref, *refs):
    blocks = refs[:NSA_KV * N_SEL]
    o_ref = refs[NSA_KV * N_SEL]
    for g in range(NSA_KV):
        for kv in range(2):
            pages = [blocks[g * N_SEL + r][kv] for r in range(N_SEL)]
            o_ref[g, kv] = jnp.concatenate(pages, axis=1).astype(BF16)


def _nsa_gather_sel(cache_t, layer, page_table, idx):
    n_b, n_pages = page_table.shape
    past = n_pages * PAGE_SIZE
    halves = PAGE_SIZE // L_SEL
    last_blk = past // L_SEL - 1

    def blk_spec(n):
        g = n // N_SEL

        def index(b, pt, ix):
            s = jnp.clip(ix[b, n], 0, last_blk)
            return (layer, pt[b, s // halves], 0, g, 0, 0)
        return pl.BlockSpec((None, None, 2, None, NSA_DH, PAGE_SIZE), index)

    grid_spec = pltpu.PrefetchScalarGridSpec(
        num_scalar_prefetch=2,
        grid=(n_b,),
        in_specs=[blk_spec(n) for n in range(NSA_KV * N_SEL)],
        out_specs=pl.BlockSpec((None, NSA_KV, 2, NSA_DH, N_SEL * PAGE_SIZE), lambda b, pt, ix: (b, 0, 0, 0, 0)))
    return pl.pallas_call(
        _nsa_gather_sel_kernel,
        grid_spec=grid_spec,
        out_shape=jax.ShapeDtypeStruct((n_b, NSA_KV, 2, NSA_DH, N_SEL * PAGE_SIZE), BF16),
        name="nsa_gather_sel",
    )(page_table, idx, *([cache_t] * (NSA_KV * N_SEL)))


def _attend_with_new(q8, k_t, v_t, live, k_new, v_new, new_live):
    s = jnp.where(live > 0.5, _dot(q8, k_t), NEG)
    k_new16 = k_new.astype(BF16).astype(F32)
    s_new = jnp.sum(q8.astype(F32) * k_new16, axis=-1, keepdims=True)
    s_new = jnp.where(new_live > 0.5, s_new, NEG)
    m = jnp.maximum(jnp.max(s, axis=-1, keepdims=True), s_new)
    e = jnp.exp(s - m)
    e_new = jnp.exp(s_new - m)
    inv = 1.0 / (jnp.sum(e, axis=-1, keepdims=True) + e_new)
    p16 = (e * inv).astype(BF16)
    p_new16 = (e_new * inv).astype(BF16).astype(F32)
    return _dot_nt(p16, v_t) + p_new16 * v_new.astype(BF16).astype(F32)


def _row_to_col(row):
    return jnp.broadcast_to(row, (SUBLANES, row.shape[1])).T[:, 0:1]


def _nsa_sel_win_sample_kernel(idx_ref, qr_ref, oc_ref, gate_ref, selnew_ref, winnew_ref, win_ref, kv_ref,
                               o_ref, wout_ref, *, qpos, past):
    gates = gate_ref[...]
    glane = _iota((1, LANES), 1)
    n_key = N_SEL * PAGE_SIZE
    lane_k = _iota((SUBLANES, n_key), 1)
    log_page = PAGE_SIZE.bit_length() - 1
    wb = win_ref.shape[3]
    wlane = _iota((SUBLANES, wb), 1)
    kwpos = past - wb + wlane
    dist = qpos - kwpos
    wlive = jnp.where((dist >= 0) & (dist <= WINDOW) & (kwpos >= 0), 1.0, 0.0)
    new_in_window = jnp.full((1, 1), 1.0 if 0 <= qpos - past <= WINDOW else 0.0, F32)
    expand = jnp.where(_iota((LANES, n_key), 0) == (_iota((LANES, n_key), 1) >> log_page), 1.0, 0.0)
    idx_f = idx_ref[...].astype(F32)
    blk_of_lane = _dot_f32(jnp.where(_iota((SUBLANES, LANES), 1) < N_SEL, idx_f, 0.0), expand)
    sel_new = selnew_ref[...]
    win_new = winnew_ref[...]
    last_lane = _iota((NSA_DH, wb), 1) == wb - 1
    for g in range(NSA_KV):
        kc = slice(g * NSA_DH, (g + 1) * NSA_DH)
        vc = slice(NSA_KVW + g * NSA_DH, NSA_KVW + (g + 1) * NSA_DH)
        q8 = _pad_rows8(qr_ref[g * NSA_HPG:(g + 1) * NSA_HPG, :])
        blk_g = blk_of_lane[g:g + 1, :]
        new_blk = float(qpos // L_SEL)
        new_selected = jnp.max(jnp.where(blk_g == new_blk, 1.0, 0.0), axis=-1, keepdims=True)
        halves = PAGE_SIZE // L_SEL
        page_g = jnp.floor(jnp.minimum(blk_g, float(past // L_SEL - 1)) * (1.0 / halves))
        kpos = page_g * float(PAGE_SIZE) + (lane_k & (PAGE_SIZE - 1)).astype(F32)
        in_block = jnp.floor(kpos * (1.0 / L_SEL)) == blk_g
        live = jnp.where(in_block & (kpos < float(past)) & (kpos <= float(qpos)), 1.0, 0.0)
        o_s = _attend_with_new(q8, kv_ref[g, 0], kv_ref[g, 1], live, sel_new[:, kc], sel_new[:, vc], new_selected)
        kw_t = win_ref[0, g]
        vw_t = win_ref[1, g]
        o_w = _attend_with_new(q8, kw_t.astype(BF16), vw_t.astype(BF16), wlive, win_new[:, kc], win_new[:, vc],
                               new_in_window)
        wout_ref[0, g] = jnp.where(last_lane, _row_to_col(win_new[:, kc]), pltpu.roll(kw_t, wb - 1, 1))
        wout_ref[1, g] = jnp.where(last_lane, _row_to_col(win_new[:, vc]), pltpu.roll(vw_t, wb - 1, 1))
        o_c = oc_ref[g * NSA_HPG:(g + 1) * NSA_HPG, :]
        rows = []
        for j in range(NSA_HPG):
            acc = jnp.zeros((1, NSA_DH), F32)
            for br, ob in enumerate((o_c, o_s, o_w)):
                col = br * NSA_HEADS + g * NSA_HPG + j
                gcol = jnp.sum(jnp.where(glane == col, gates, 0.0), axis=-1, keepdims=True)
                acc = acc + gcol * ob[j:j + 1, :]
            rows.append(acc)
        o_ref[g * NSA_HPG:(g + 1) * NSA_HPG, :] = jnp.concatenate(rows, axis=0)


def _nsa_sel_win_sample(kv_sel, idx8, qr3, oc3, gates, sel_new, win_new, win_t, qpos, past):
    n_b = qr3.shape[0]
    wb = win_t.shape[-1]
    b3 = lambda b: (b, 0, 0)
    b5 = lambda b: (b, 0, 0, 0, 0)
    wspec = pl.BlockSpec((None, 2, NSA_KV, NSA_DH, wb), b5)
    return pl.pallas_call(
        functools.partial(_nsa_sel_win_sample_kernel, qpos=qpos, past=past),
        grid=(n_b,),
        in_specs=[pl.BlockSpec((None, SUBLANES, LANES), b3),
                  pl.BlockSpec((None, NSA_HEADS, NSA_DH), b3), pl.BlockSpec((None, NSA_HEADS, NSA_DH), b3),
                  pl.BlockSpec((None, 1, LANES), b3), pl.BlockSpec((None, 1, 2 * NSA_KVW), b3),
                  pl.BlockSpec((None, 1, 2 * NSA_KVW), b3), wspec,
                  pl.BlockSpec((None, NSA_KV, 2, NSA_DH, N_SEL * PAGE_SIZE), b5)],
        out_specs=[pl.BlockSpec((None, NSA_HEADS, NSA_DH), b3), wspec],
        out_shape=[jax.ShapeDtypeStruct((n_b, NSA_HEADS, NSA_DH), F32), jax.ShapeDtypeStruct(win_t.shape, F32)],
        compiler_params=_cparams("parallel"),
        name="nsa_sel_win_sample",
    )(idx8, qr3, oc3, gates.reshape(n_b, 1, LANES), sel_new.reshape(n_b, 1, -1),
      win_new.reshape(n_b, 1, -1), win_t, kv_sel)


def _nsa_sample_layer(x, layer, norm_g, w_in, cmp_pe, cmp_w, w_out, cache_cmp, cache_sel, win_state, page_table,
                      tables):
    n_b = x.shape[0]
    past = page_table.shape[1] * PAGE_SIZE
    qpos = past
    q, qr, cmp_rows, sel_rows, win_rows, gates, _ = _nsa_proj(
        x, norm_g.reshape(1, D_MODEL), _nsa_w_in_padded(w_in), tables, 1, n_b)
    bd, pe2 = _cmp_weights(cmp_w, cmp_pe)
    kvc = _nsa_compress_sample(cache_cmp, layer, page_table, cmp_rows, bd, pe2)
    n_sel = -(-(past + 1) // L_SEL)
    oc3, idx8 = _nsa_cmp_topk_sample(q.reshape(n_b, NSA_HEADS, NSA_DH), kvc, qpos, n_sel)
    idx = idx8[:, :NSA_KV, :N_SEL].reshape(n_b, NSA_KV * N_SEL)
    kv_sel = _nsa_gather_sel(cache_sel, layer, page_table, idx)
    o3, win_out = _nsa_sel_win_sample(kv_sel, idx8, qr.reshape(n_b, NSA_HEADS, NSA_DH), oc3, gates,
                                      sel_rows, win_rows, win_state, qpos, past)
    y = _matmul_residual(o3.reshape(n_b, NSA_Q), w_out.astype(BF16), x, n_b)
    return y, cmp_rows, sel_rows, win_out


def _gdn_sample_pre_kernel(z_ref, b0_ref, b1_ref, b2_ref, cw_ref, alog_ref, dtb_ref, q_ref, k_ref, v_ref, gb_ref):
    outs = (q_ref, k_ref, v_ref)
    for c in range(GDN_CONV_CH // _GDN_CW):
        c0 = c * _GDN_CW
        cols = slice(c0, c0 + _GDN_CW)
        w = cw_ref[:, cols]
        y = b0_ref[:, cols] * w[0:1] + b1_ref[:, cols] * w[1:2] + b2_ref[:, cols] * w[2:3] + z_ref[:, cols] * w[3:4]
        o_ref = outs[c0 // GDN_QK]
        o_ref[:, c0 % GDN_QK:c0 % GDN_QK + _GDN_CW] = _gdn_post_conv(y, c0)
    gb_ref[...] = _gdn_gates(z_ref[:, GDN_CONV_CH + GDN_VW:_GDN_W_COLS], alog_ref[...], dtb_ref[...])


def _gdn_sample_pre(z, bufs, conv_w, alog, dtb):
    n = z.shape[0]
    f = jax.ShapeDtypeStruct((n, GDN_QK), F32)
    return pl.pallas_call(
        _gdn_sample_pre_kernel,
        out_shape=[f, f, f, jax.ShapeDtypeStruct((n, LANES), F32)],
        compiler_params=pltpu.CompilerParams(vmem_limit_bytes=VMEM_LIMIT),
        name="gdn_sample_pre",
    )(z, *bufs, conv_w, alog, dtb)


def _gdn_sample_state_kernel(s0_ref, qt_ref, kt_ref, v_ref, gb_ref, gate_ref, ng_ref, s_ref, o_ref):
    gb = gb_ref[...]
    lane = _iota((1, LANES), 1)
    ng = ng_ref[...]
    for h in range(GDN_HEADS):
        cols = slice(h * GDN_DV, (h + 1) * GDN_DV)
        g_h = jnp.sum(jnp.where(lane == h, gb, 0.0), axis=-1, keepdims=True)
        beta = jnp.sum(jnp.where(lane == GDN_HEADS + h, gb, 0.0), axis=-1, keepdims=True)
        eg = jnp.exp(g_h)
        s_prev = s0_ref[h]
        k_col = kt_ref[:, h:h + 1]
        q_col = qt_ref[:, h:h + 1]
        v_row = v_ref[:, cols]
        u = v_row * beta - jnp.sum(s_prev * (k_col * beta * eg), axis=0, keepdims=True)
        attn = jnp.sum(q_col * k_col, axis=0, keepdims=True)
        o = jnp.sum(s_prev * (q_col * eg), axis=0, keepdims=True) + attn * u
        s_ref[h] = s_prev * eg + k_col * u
        o_ref[:, cols] = (_rmsnorm(o, ng) * jax.nn.silu(gate_ref[:, cols])).astype(BF16)


def _gdn_sample_state(s0, q_t, k_t, v, gb, gate, head_norm_g):
    n_b = s0.shape[0]
    st = pl.BlockSpec((None, GDN_HEADS, GDN_DK, GDN_DV), lambda b: (b, 0, 0, 0))
    col = pl.BlockSpec((None, GDN_DK, GDN_HEADS), lambda b: (b, 0, 0))
    wide = pl.BlockSpec((None, 1, GDN_VW), lambda b: (b, 0, 0))
    return pl.pallas_call(
        _gdn_sample_state_kernel,
        grid=(n_b,),
        in_specs=[st, col, col, wide, pl.BlockSpec((None, 1, LANES), lambda b: (b, 0, 0)), wide,
                  _resident((1, GDN_DV))],
        out_specs=[st, wide],
        out_shape=[jax.ShapeDtypeStruct(s0.shape, F32), jax.ShapeDtypeStruct((n_b, 1, GDN_VW), BF16)],
        compiler_params=_cparams("parallel"),
        name="gdn_sample_state",
    )(s0, q_t, k_t, v.reshape(n_b, 1, GDN_VW), gb.reshape(n_b, 1, LANES), gate.reshape(n_b, 1, GDN_VW),
      head_norm_g.reshape(1, GDN_DV))


def _gdn_sample_layer(x, norm_g, s0, conv_buf, w_in, conv_w, a_log, dt_bias, head_norm_g, w_out):
    n_b = x.shape[0]
    z = _norm_matmul_small(x, norm_g, _gdn_w_in_padded(w_in), 3)
    bufs = [conv_buf[:, j, :] for j in range(GDN_CONV - 1)]
    q, k, v, gb = _gdn_sample_pre(z, bufs, conv_w, _lane_row(a_log), _lane_row(dt_bias))
    to_cols = lambda t: t.reshape(n_b, GDN_HEADS, GDN_DK).transpose(0, 2, 1)
    gate = z[:, GDN_CONV_CH:GDN_CONV_CH + GDN_VW]
    s_new, o = _gdn_sample_state(s0, to_cols(q), to_cols(k), v, gb, gate, head_norm_g)
    y = _matmul_residual(o.reshape(n_b, GDN_VW), w_out.astype(BF16), x, n_b)
    new_buf = jnp.concatenate([conv_buf[:, 1:, :], z[:, None, :GDN_CONV_CH]], axis=1)
    return y, s_new, new_buf


def _lru_sample_kernel(z_ref, b0_ref, b1_ref, b2_ref, cw_ref, cb_ref, wga_ref, bga_ref, wgx_ref, bgx_ref, lam_ref,
                       h0_ref, h_ref, y_ref):
    for c in range(D_RNN // _LRU_CW):
        c0 = c * _LRU_CW
        cols = slice(c0, c0 + _LRU_CW)
        w = cw_ref[:, cols]
        xc = (b0_ref[:, cols] * w[0:1] + b1_ref[:, cols] * w[1:2] + b2_ref[:, cols] * w[2:3]
              + z_ref[:, D_RNN + c0:D_RNN + c0 + _LRU_CW] * w[3:4] + cb_ref[:, cols])
        a, b = _lru_gates(xc, wga_ref, bga_ref[:, cols], wgx_ref, bgx_ref[:, cols], lam_ref[:, cols], c0 // LRU_BW)
        h = a * h0_ref[:, cols] + b
        h_ref[:, cols] = h
        y_ref[:, cols] = (h * jax.nn.gelu(z_ref[:, cols])).astype(BF16)


def _lru_sample_layer(x, norm_g, h0, conv_buf, w_in, conv_w, conv_b, w_ga, b_ga, w_gx, b_gx, lam, w_out):
    n_b = x.shape[0]
    z = _norm_matmul_small(x, norm_g, w_in.astype(BF16), 2)
    r1 = lambda v: v.reshape(1, D_RNN)
    bufs = [conv_buf[:, j, :] for j in range(LRU_CONV - 1)]
    h, y = pl.pallas_call(
        _lru_sample_kernel,
        out_shape=[jax.ShapeDtypeStruct((n_b, D_RNN), F32), jax.ShapeDtypeStruct((n_b, D_RNN), BF16)],
        compiler_params=pltpu.CompilerParams(vmem_limit_bytes=VMEM_LIMIT),
        name="lru_sample",
    )(z, *bufs, conv_w, r1(conv_b), w_ga.astype(BF16), r1(b_ga), w_gx.astype(BF16), r1(b_gx), r1(lam), h0)
    y = _matmul_residual(y, w_out.astype(BF16), x, n_b)
    new_buf = jnp.concatenate([conv_buf[:, 1:, :], z[:, None, D_RNN:]], axis=1)
    return y, h, new_buf


def _ffn_sample_kernel(x_ref, g_ref, wv_ref, wg_ref, cwv_ref, cwg_ref, b0v_ref, b0g_ref, b1v_ref, b1g_ref, wout_ref,
                       y_ref, uv_ref, ug_ref, xn_scr):
    c = pl.program_id(0)

    @pl.when(c == 0)
    def _():
        x = x_ref[...]
        xn_scr[...] = _rmsnorm(x, g_ref[...]).astype(BF16)
        y_ref[...] = x

    xn = xn_scr[...]
    uv = _dot(xn, wv_ref[...])
    ug = _dot(xn, wg_ref[...])
    uv_ref[...] = uv
    ug_ref[...] = ug
    cwv, cwg = cwv_ref[...], cwg_ref[...]
    val = b0v_ref[...] * cwv[0:1] + b1v_ref[...] * cwv[1:2] + uv * cwv[2:3]
    gt = b0g_ref[...] * cwg[0:1] + b1g_ref[...] * cwg[1:2] + ug * cwg[2:3]
    y_ref[...] += _dot((val * jax.nn.silu(gt)).astype(BF16), wout_ref[...])


def _ffn_sample(x, norm_g, w_in, conv_w, w_out, buf):
    n_b = x.shape[0]
    n_c = D_FF // _FFN_CW
    b0, b1 = buf[:, 0, :], buf[:, 1, :]
    val = lambda shape0: pl.BlockSpec((shape0, _FFN_CW), lambda c: (0, c))
    gat = lambda shape0: pl.BlockSpec((shape0, _FFN_CW), lambda c: (0, n_c + c))
    y, uv, ug = pl.pallas_call(
        _ffn_sample_kernel,
        grid=(n_c,),
        in_specs=[_resident((n_b, D_MODEL)), _resident((1, D_MODEL)), val(D_MODEL), gat(D_MODEL),
                  val(FFN_CONV), gat(FFN_CONV), val(n_b), gat(n_b), val(n_b), gat(n_b),
                  pl.BlockSpec((_FFN_CW, D_MODEL), lambda c: (c, 0))],
        out_specs=[pl.BlockSpec((n_b, D_MODEL), lambda c: (0, 0)), val(n_b), val(n_b)],
        out_shape=[jax.ShapeDtypeStruct((n_b, D_MODEL), F32), jax.ShapeDtypeStruct((n_b, D_FF), F32),
                   jax.ShapeDtypeStruct((n_b, D_FF), F32)],
        scratch_shapes=[pltpu.VMEM((n_b, D_MODEL), BF16)],
        compiler_params=_cparams("arbitrary"),
        name="ffn_sample",
    )(x, norm_g.reshape(1, D_MODEL), w_in, w_in, conv_w, conv_w, b0, b0, b1, b1, w_out)
    new_buf = jnp.stack([b1, jnp.concatenate([uv, ug], axis=-1)], axis=1)
    return y, new_buf


def kernel(x_prompt, x_sample, cache_nsa_cmp, cache_nsa_sel, state_nsa_win, state_gdn_S, state_gdn_conv,
           state_lru_h, state_lru_conv, state_ffn_conv, page_table,
           norm_mix_g, norm_ffn_g, norm_final_g,
           nsa_w_in, nsa_cmp_pe, nsa_cmp_w, nsa_w_out,
           gdn_w_in, gdn_conv_w, gdn_A_log, gdn_dt_bias, gdn_norm_g, gdn_w_out,
           lru_w_in, lru_conv_w, lru_conv_b, lru_w_ga, lru_b_ga, lru_w_gx, lru_b_gx, lru_lambda, lru_w_out,
           ffn_w_in, ffn_conv_w, ffn_w_out):
    n_p, t_len, _ = x_prompt.shape
    n_s = x_sample.shape[0]
    past = page_table.shape[1] * PAGE_SIZE
    wl = min(WINDOW, t_len)
    xp = x_prompt.reshape(n_p * t_len, D_MODEL)
    xs = x_sample.reshape(n_s, D_MODEL)
    tab_p = _rope_tables(jnp.arange(t_len, dtype=I32))
    tab_s = _rope_tables(jnp.full((n_s,), past, I32))
    cache_cmp_t = _rows_last(cache_nsa_cmp)
    cache_sel_t = _rows_last(cache_nsa_sel)
    win_t = _rows_last(state_nsa_win)
    ffn_w_in16 = ffn_w_in.astype(BF16)
    ffn_w_out16 = ffn_w_out.astype(BF16)
    zeros = lambda *shape: jnp.zeros(shape, F32)
    kv5 = lambda rows, nb: rows.reshape(nb, -1, 2, NSA_KV, NSA_DH)

    p_nsa, s_nsa, p_gdn, s_gdn, p_lru, s_lru, p_ffn, s_ffn = [], [], [], [], [], [], [], []
    for li in range(DEPTH):
        j = li // N_MIXERS
        kind = li % N_MIXERS
        if kind == 0:
            xp, c, s, w = _nsa_prompt_layer(xp, n_p, norm_mix_g[li], nsa_w_in[j], nsa_cmp_pe[j], nsa_cmp_w[j],
                                            nsa_w_out[j], tab_p)
            p_nsa.append((kv5(c, n_p), kv5(s, n_p), kv5(w, n_p)[:, t_len - wl:]))
            xs, c, s, w = _nsa_sample_layer(xs, j, norm_mix_g[li], nsa_w_in[j], nsa_cmp_pe[j], nsa_cmp_w[j],
                                            nsa_w_out[j], cache_cmp_t, cache_sel_t, win_t[j], page_table, tab_s)
            s_nsa.append((kv5(c, n_s), kv5(s, n_s), jnp.transpose(w, (0, 4, 1, 2, 3))))
        elif kind == 1:
            args = (gdn_w_in[j], gdn_conv_w[j], gdn_A_log[j], gdn_dt_bias[j], gdn_norm_g[j], gdn_w_out[j])
            xp, s_new, buf = _gdn_prompt_layer(xp, n_p, norm_mix_g[li], zeros(n_p, GDN_HEADS, GDN_DK, GDN_DV),
                                               zeros(n_p, GDN_CONV - 1, GDN_CONV_CH), *args)
            p_gdn.append((s_new, buf))
            xs, s_new, buf = _gdn_sample_layer(xs, norm_mix_g[li], state_gdn_S[j], state_gdn_conv[j], *args)
            s_gdn.append((s_new, buf))
        else:
            args = (lru_w_in[j], lru_conv_w[j], lru_conv_b[j], lru_w_ga[j], lru_b_ga[j], lru_w_gx[j], lru_b_gx[j],
                    lru_lambda[j], lru_w_out[j])
            xp, h, buf = _lru_prompt_layer(xp, n_p, norm_mix_g[li], zeros(n_p, D_RNN), zeros(n_p, LRU_CONV - 1, D_RNN),
                                           *args)
            p_lru.append((h, buf))
            xs, h, buf = _lru_sample_layer(xs, norm_mix_g[li], state_lru_h[j], state_lru_conv[j], *args)
            s_lru.append((h, buf))
        xp, buf = _ffn_prompt(xp, n_p, norm_ffn_g[li], ffn_w_in16[li], ffn_conv_w[li], ffn_w_out16[li],
                              zeros(n_p, FFN_CONV - 1, 2 * D_FF))
        p_ffn.append(buf)
        xs, buf = _ffn_sample(xs, norm_ffn_g[li], ffn_w_in16[li], ffn_conv_w[li], ffn_w_out16[li], state_ffn_conv[li])
        s_ffn.append(buf)

    y_prompt = _final_norm(xp, norm_final_g, 512).reshape(n_p, t_len, D_MODEL)
    y_sample = _final_norm(xs, norm_final_g, n_s).reshape(n_s, 1, D_MODEL)
    stack = lambda entries, k: jnp.stack([e[k] for e in entries])
    return (y_prompt, y_sample,
            stack(p_nsa, 0), stack(s_nsa, 0), stack(p_nsa, 1), stack(s_nsa, 1), stack(p_nsa, 2), stack(s_nsa, 2),
            stack(p_gdn, 0), stack(s_gdn, 0), stack(p_gdn, 1), stack(s_gdn, 1),
            stack(p_lru, 0), stack(s_lru, 0), stack(p_lru, 1), stack(s_lru, 1),
            jnp.stack(p_ffn), jnp.stack(s_ffn))
```

```python
import functools

import jax
import jax.numpy as jnp
from jax import lax
from jax.experimental import pallas as pl
from jax.experimental.pallas import tpu as pltpu

F32 = jnp.float32
BF16 = jnp.bfloat16
I32 = jnp.int32

D_MODEL = 1024
DEPTH = 4
PAST_LEN = 8192
PAGE_SIZE = 128
N_MIXERS = 3

NSA_DH = 64
NSA_HEADS = 16
NSA_HPG = 4
NSA_KV = 4
NSA_Q = NSA_HEADS * NSA_DH
NSA_KVW = NSA_KV * NSA_DH
ROT_DIM = NSA_DH // 4
ROPE_THETA = 500000.0
L_CMP = 32
L_SEL = 64
N_SEL = 16
WINDOW = 512
_LOG_SEL = 6
NSA_SCALE = NSA_DH ** -0.5

GDN_DK = 128
GDN_DV = 128
GDN_HEADS = 8
GDN_QK = GDN_HEADS * GDN_DK
GDN_VW = GDN_HEADS * GDN_DV
GDN_CONV = 4
GDN_CONV_CH = 2 * GDN_QK + GDN_VW
GDN_CHUNK = 64

D_RNN = D_MODEL
LRU_BLOCKS = 8
LRU_BW = D_RNN // LRU_BLOCKS
LRU_CONV = 4
LRU_C = 8.0

D_FF = 2816
FFN_CONV = 3

EPS = 1e-6
NEG = -1e30
FORCE = 1e4

LANES = 128
SUBLANES = 8
MXU_DIM = 256
VMEM_LIMIT = 56 * 1024 * 1024


def _cparams(*sem):
    return pltpu.CompilerParams(dimension_semantics=sem, vmem_limit_bytes=VMEM_LIMIT)


def _resident(shape):
    nd = len(shape)
    return pl.BlockSpec(shape, lambda *_: (0,) * nd, pipeline_mode=pl.Buffered(1))


def _rmsnorm(x, g):
    return x * lax.rsqrt(jnp.mean(x * x, axis=-1, keepdims=True) + EPS) * g


def _dot(a, b):
    return jnp.dot(a, b, preferred_element_type=F32)


def _dot_nt(a, b):
    return lax.dot_general(a, b, (((1,), (1,)), ((), ())), preferred_element_type=F32)


def _dot_tn(a, b):
    return lax.dot_general(a, b, (((0,), (0,)), ((), ())), preferred_element_type=F32)


def _dot_f32(a, b):
    return jnp.dot(a, b, precision=lax.Precision.HIGHEST, preferred_element_type=F32)


def _iota(shape, axis):
    return lax.broadcasted_iota(I32, shape, axis)


def _rope_tables(pos):
    half = ROT_DIM // 2
    inv = ROPE_THETA ** (-jnp.arange(half, dtype=F32) / half)
    ang = pos.astype(F32)[:, None] * inv[None, :]
    lane = jnp.arange(LANES) % NSA_DH
    c = jnp.cos(ang)[:, lane % half]
    s = jnp.sin(ang)[:, lane % half]
    cos = jnp.where(lane < ROT_DIM, c, 1.0)
    sin_a = jnp.where((lane >= half) & (lane < ROT_DIM), s, 0.0)
    sin_b = jnp.where(lane < half, -s, 0.0)
    return cos.astype(F32), sin_a.astype(F32), sin_b.astype(F32)


def _rope(x, cos, sin_a, sin_b):
    half = ROT_DIM // 2
    outs = []
    for c in range(x.shape[1] // LANES):
        xc = x[:, c * LANES:(c + 1) * LANES]
        outs.append(xc * cos + pltpu.roll(xc, half, 1) * sin_a + pltpu.roll(xc, LANES - half, 1) * sin_b)
    return outs[0] if len(outs) == 1 else jnp.concatenate(outs, axis=1)


_NSA_W_COLS = NSA_Q + 6 * NSA_KVW + LANES


def _nsa_proj_kernel(x_ref, g_ref, w_ref, cos_ref, sa_ref, sb_ref,
                     q_ref, qr_ref, cmp_ref, sel_ref, win_ref, gate_ref, kvb_ref):
    xn = _rmsnorm(x_ref[...], g_ref[...]).astype(BF16)
    cos, sa, sb = cos_ref[...], sa_ref[...], sb_ref[...]

    def mm(c0, c1):
        return _dot(xn, w_ref[:, c0:c1])

    zq = mm(0, NSA_Q)
    q_ref[...] = (zq * NSA_SCALE).astype(BF16)
    qr_ref[...] = (_rope(zq, cos, sa, sb) * NSA_SCALE).astype(BF16)
    c0 = NSA_Q
    cmp_ref[...] = mm(c0, c0 + 2 * NSA_KVW)
    c0 += 2 * NSA_KVW
    for kind, o_ref in ((0, sel_ref), (1, win_ref)):
        z = mm(c0, c0 + 2 * NSA_KVW)
        kr = _rope(z[:, :NSA_KVW], cos, sa, sb)
        v = z[:, NSA_KVW:]
        o_ref[:, :NSA_KVW] = kr
        o_ref[:, NSA_KVW:] = v
        for g in range(NSA_KV):
            kvb_ref[2 * kind, g] = kr[:, g * NSA_DH:(g + 1) * NSA_DH].astype(BF16)
            kvb_ref[2 * kind + 1, g] = v[:, g * NSA_DH:(g + 1) * NSA_DH].astype(BF16)
        c0 += 2 * NSA_KVW
    gate_ref[...] = jax.nn.sigmoid(mm(c0, c0 + LANES))


def _nsa_proj(x, g, w, tables, n_batch, tm):
    n = x.shape[0]
    t_len = n // n_batch
    tpb = t_len // tm
    row = lambda i: (i, 0)
    tab = pl.BlockSpec((tm, LANES), lambda i: (i % tpb, 0))
    return pl.pallas_call(
        _nsa_proj_kernel,
        grid=(n // tm,),
        in_specs=[pl.BlockSpec((tm, D_MODEL), row), _resident((1, D_MODEL)), _resident((D_MODEL, _NSA_W_COLS)),
                  tab, tab, tab],
        out_specs=[pl.BlockSpec((tm, NSA_Q), row), pl.BlockSpec((tm, NSA_Q), row),
                   pl.BlockSpec((tm, 2 * NSA_KVW), row), pl.BlockSpec((tm, 2 * NSA_KVW), row),
                   pl.BlockSpec((tm, 2 * NSA_KVW), row), pl.BlockSpec((tm, LANES), row),
                   pl.BlockSpec((None, 4, NSA_KV, tm, NSA_DH), lambda i: (i // tpb, 0, 0, i % tpb, 0))],
        out_shape=[jax.ShapeDtypeStruct((n, NSA_Q), BF16), jax.ShapeDtypeStruct((n, NSA_Q), BF16),
                   jax.ShapeDtypeStruct((n, 2 * NSA_KVW), F32), jax.ShapeDtypeStruct((n, 2 * NSA_KVW), F32),
                   jax.ShapeDtypeStruct((n, 2 * NSA_KVW), F32), jax.ShapeDtypeStruct((n, LANES), F32),
                   jax.ShapeDtypeStruct((n_batch, 4, NSA_KV, t_len, NSA_DH), BF16)],
        compiler_params=_cparams("parallel"),
        name="nsa_proj",
    )(x, g, w, *tables)


def _nsa_w_in_padded(w_in):
    pad = _NSA_W_COLS - w_in.shape[1]
    return jnp.pad(w_in, ((0, 0), (0, pad))).astype(BF16)


def _cmp_weights(cmp_w, cmp_pe):
    z = jnp.zeros_like(cmp_w)
    bd = jnp.concatenate([jnp.concatenate([cmp_w, z], axis=-1), jnp.concatenate([z, cmp_w], axis=-1)], axis=-2)
    return bd.astype(BF16), jnp.concatenate([cmp_pe, cmp_pe], axis=-1)


def _compress_planes(load_rows, bd_ref, pe_ref, o_ref, n_rows_out):
    for plane in range(4):
        kv = plane // 2
        acc = jnp.zeros((n_rows_out, LANES), F32)
        bias = jnp.zeros((SUBLANES, LANES), F32)
        for l in range(L_CMP):
            w = bd_ref[kv, l]
            acc = acc + _dot(load_rows(plane, l).astype(BF16), w)
            pe = jnp.broadcast_to(pe_ref[kv, l:l + 1, :], (SUBLANES, LANES)).astype(BF16)
            bias = bias + _dot(pe, w)
        out = acc + bias[0:1, :]
        g0 = 2 * (plane % 2)
        o_ref[kv, g0] = out[:, :NSA_DH]
        o_ref[kv, g0 + 1] = out[:, NSA_DH:]


def _nsa_compress_prompt_kernel(x0_ref, x1_ref, x2_ref, x3_ref, bd_ref, pe_ref, o_ref, *, n_blk, n_pad):
    planes = (x0_ref, x1_ref, x2_ref, x3_ref)
    if n_pad > n_blk:
        o_ref[...] = jnp.zeros(o_ref.shape, F32)

    def load_rows(plane, l):
        return planes[plane][pl.ds(l, n_blk, stride=L_CMP), :]

    if n_pad == n_blk:
        _compress_planes(load_rows, bd_ref, pe_ref, o_ref, n_blk)
    else:
        _compress_planes(load_rows, bd_ref, pe_ref, o_ref.at[:, :, 0:n_blk, :], n_blk)


def _nsa_compress_prompt(cmp_rows, bd, pe2, n_batch):
    n = cmp_rows.shape[0]
    t_len = n // n_batch
    n_blk = t_len // L_CMP
    n_pad = max(n_blk, LANES)
    planes = [pl.BlockSpec((t_len, LANES), functools.partial(lambda b, c: (b, c), c=c)) for c in range(4)]
    return pl.pallas_call(
        functools.partial(_nsa_compress_prompt_kernel, n_blk=n_blk, n_pad=n_pad),
        grid=(n_batch,),
        in_specs=planes + [_resident(bd.shape), _resident(pe2.shape)],
        out_specs=pl.BlockSpec((None, 2, NSA_KV, n_pad, NSA_DH), lambda b: (b, 0, 0, 0, 0)),
        out_shape=jax.ShapeDtypeStruct((n_batch, 2, NSA_KV, n_pad, NSA_DH), F32),
        compiler_params=_cparams("parallel"),
        name="nsa_compress_prompt",
    )(cmp_rows, cmp_rows, cmp_rows, cmp_rows, bd, pe2)


def _stack_heads(x):
    return jnp.concatenate([x[:, j * NSA_DH:(j + 1) * NSA_DH] for j in range(NSA_HPG)], axis=0)


def _softmax_rows(s):
    m = jnp.max(s, axis=-1, keepdims=True)
    e = jnp.exp(s - m)
    return e / jnp.sum(e, axis=-1, keepdims=True)


def _select_blocks(imp_t, t0, rank_scr):
    n_sel_pad = LANES // 2
    tq = imp_t.shape[1]
    rank_scr[...] = imp_t
    pair = rank_scr[pl.ds(0, n_sel_pad, stride=2), :] + rank_scr[pl.ds(1, n_sel_pad, stride=2), :]
    blk = _iota((n_sel_pad, tq), 0)
    tpos = t0 + _iota((n_sel_pad, tq), 1)
    forced = (blk == 0) | (blk == (tpos >> _LOG_SEL))
    avail = blk * L_SEL <= tpos
    val = jnp.where(forced, FORCE, jnp.where(avail, pair, -1.0))
    rank_scr[0:n_sel_pad, :] = val
    cnt = jnp.zeros((n_sel_pad, tq), F32)
    for s in range(n_sel_pad):
        row = jnp.broadcast_to(rank_scr[pl.ds(s, 1), :], (n_sel_pad, tq))
        ahead = (row > val) | ((row == val) & (blk > s))
        cnt = cnt + jnp.where(ahead, 1.0, 0.0)
    sel = jnp.where(cnt < float(N_SEL), 1.0, 0.0)
    return jnp.concatenate([sel, jnp.zeros((LANES - n_sel_pad, tq), F32)], axis=0)


_ATT_RB = 128


def _softmax_blocks(sc_scr, bias_scr, p_scr, width, rows, tq, post=None):
    for r0 in range(0, rows, _ATT_RB):
        rq = r0 % tq
        s = sc_scr[r0:r0 + _ATT_RB, 0:width] + bias_scr[rq:rq + _ATT_RB, 0:width]
        e = jnp.exp(s - jnp.max(s, axis=-1, keepdims=True))
        p = e / jnp.sum(e, axis=-1, keepdims=True)
        if post is not None:
            p = post(p, r0)
        p_scr[r0:r0 + _ATT_RB, 0:width] = p.astype(BF16)


def _nsa_attn_kernel(q_ref, qr_ref, gate_ref, kvc_ref, kvb_ref, o_ref,
                     rank_scr, sc_scr, bias_scr, p_scr, m_scr, l_scr, acc_scr, imp_scr, *, tq, tk):
    g = pl.program_id(1)
    t0 = pl.program_id(2) * tq
    rows = NSA_HPG * tq
    q4 = _stack_heads(q_ref[...])
    qr4 = _stack_heads(qr_ref[...])

    n_cmp = kvc_ref.shape[1]
    sc_scr[:, 0:n_cmp] = _dot_nt(q4, kvc_ref[0].astype(BF16))
    tpos_c = t0 + _iota((tq, n_cmp), 0)
    c_end = (_iota((tq, n_cmp), 1) + 1) * L_CMP - 1
    bias_scr[:, 0:n_cmp] = jnp.where(c_end <= tpos_c, 0.0, NEG)
    imp_scr[...] = jnp.zeros(imp_scr.shape, F32)

    def cmp_post(p, r0):
        rq = r0 % tq
        p = p * jnp.where(t0 + rq + _iota((_ATT_RB, 1), 0) >= L_CMP - 1, 1.0, 0.0)
        imp_scr[rq:rq + _ATT_RB, :] += p
        return p

    _softmax_blocks(sc_scr, bias_scr, p_scr, n_cmp, rows, tq, post=cmp_post)
    o_c = _dot(p_scr[:, 0:n_cmp], kvc_ref[1].astype(BF16))

    sel_t = _select_blocks(imp_scr[...].T, t0, rank_scr)
    sel = jnp.where(sel_t.T > 0.5, 1.0, 0.0).astype(BF16)

    m_scr[...] = jnp.full(m_scr.shape, NEG, F32)
    l_scr[...] = jnp.zeros(l_scr.shape, F32)
    acc_scr[...] = jnp.zeros(acc_scr.shape, F32)

    def sel_tile(kt, carry):
        k0 = pl.multiple_of(kt * tk, tk)
        sc_scr[:, 0:tk] = _dot_nt(qr4, kvb_ref[0, pl.ds(k0, tk), :])
        expand = jnp.where(_iota((LANES, tk), 0) == ((k0 + _iota((LANES, tk), 1)) >> _LOG_SEL), 1.0, 0.0).astype(BF16)
        picked = _dot(sel, expand)
        live = (picked > 0.5) & (k0 + _iota((tq, tk), 1) <= t0 + _iota((tq, tk), 0))
        bias_scr[:, 0:tk] = jnp.where(live, 0.0, NEG)
        for r0 in range(0, rows, _ATT_RB):
            rq = r0 % tq
            rs = slice(r0, r0 + _ATT_RB)
            s = sc_scr[rs, 0:tk] + bias_scr[rq:rq + _ATT_RB, 0:tk]
            m_old = m_scr[rs, :]
            m_new = jnp.maximum(m_old, jnp.max(s, axis=-1, keepdims=True))
            alpha = jnp.exp(m_old - m_new)
            pe = jnp.exp(s - m_new)
            l_scr[rs, :] = alpha * l_scr[rs, :] + jnp.sum(pe, axis=-1, keepdims=True)
            m_scr[rs, :] = m_new
            acc_scr[rs, :] = alpha * acc_scr[rs, :]
            p_scr[rs, 0:tk] = pe.astype(BF16)
        acc_scr[...] += _dot(p_scr[:, 0:tk], kvb_ref[1, pl.ds(k0, tk), :])
        return carry

    n_kt = (t0 + tq + tk - 1) // tk
    lax.fori_loop(0, n_kt, sel_tile, 0)
    o_s = acc_scr[...] / l_scr[...]

    wk = WINDOW + tq
    w0 = pl.multiple_of(jnp.maximum(t0 - WINDOW, 0), tq)
    sc_scr[:, 0:wk] = _dot_nt(qr4, kvb_ref[2, pl.ds(w0, wk), :])
    dist = (t0 + _iota((tq, wk), 0)) - (w0 + _iota((tq, wk), 1))
    bias_scr[:, 0:wk] = jnp.where((dist >= 0) & (dist <= WINDOW), 0.0, NEG)
    _softmax_blocks(sc_scr, bias_scr, p_scr, wk, rows, tq)
    o_w = _dot(p_scr[:, 0:wk], kvb_ref[3, pl.ds(w0, wk), :])

    gates = gate_ref[...]
    lane = _iota((tq, LANES), 1)
    outs = []
    for j in range(NSA_HPG):
        o = jnp.zeros((tq, NSA_DH), F32)
        for br, ob in enumerate((o_c, o_s, o_w)):
            col = br * NSA_HEADS + g * NSA_HPG + j
            gcol = jnp.sum(jnp.where(lane == col, gates, 0.0), axis=-1, keepdims=True)
            o = o + gcol * ob[j * tq:(j + 1) * tq]
        outs.append(o)
    o_ref[...] = jnp.concatenate(outs, axis=1).astype(BF16)


def _nsa_attn_prompt(q, qr, gates, kvc, kvb, n_batch, tq=128, tk=512):
    n = q.shape[0]
    t_len = n // n_batch
    nq = t_len // tq
    tk = min(tk, t_len)
    assert kvc.shape[3] == LANES and t_len >= WINDOW + tq and t_len % tk == 0
    rows = NSA_HPG * tq
    width = max(tk, WINDOW + tq, LANES)
    qspec =pl.BlockSpec((tq, NSA_HPG * NSA_DH), lambda b, g, i: (b * nq + i, g))
    return pl.pallas_call(
        functools.partial(_nsa_attn_kernel, tq=tq, tk=tk),
        grid=(n_batch, NSA_KV, nq),
        in_specs=[qspec, qspec,
                  pl.BlockSpec((tq, LANES), lambda b, g, i: (b * nq + i, 0)),
                  pl.BlockSpec((None, 2, None, LANES, NSA_DH), lambda b, g, i: (b, 0, g, 0, 0)),
                  pl.BlockSpec((None, 4, None, t_len, NSA_DH), lambda b, g, i: (b, 0, g, 0, 0))],
        out_specs=qspec,
        out_shape=jax.ShapeDtypeStruct((n, NSA_Q), BF16),
        scratch_shapes=[pltpu.VMEM((LANES, tq), F32),
                        pltpu.VMEM((rows, width), F32),
                        pltpu.VMEM((tq, width), F32),
                        pltpu.VMEM((rows, width), BF16),
                        pltpu.VMEM((rows, 1), F32), pltpu.VMEM((rows, 1), F32),
                        pltpu.VMEM((rows, NSA_DH), F32),
                        pltpu.VMEM((tq, LANES), F32)],
        compiler_params=_cparams("parallel", "parallel", "arbitrary"),
        name="nsa_attn_prompt",
    )(q, qr, gates, kvc, kvb)


def _matmul_residual_kernel(a_ref, w_ref, x_ref, o_ref):
    o_ref[...] = x_ref[...] + _dot(a_ref[...].astype(BF16), w_ref[...])


def _matmul_residual(a, w, x, tm):
    n, k = a.shape
    row = lambda i: (i, 0)
    return pl.pallas_call(
        _matmul_residual_kernel,
        grid=(n // tm,),
        in_specs=[pl.BlockSpec((tm, k), row), _resident(w.shape), pl.BlockSpec((tm, D_MODEL), row)],
        out_specs=pl.BlockSpec((tm, D_MODEL), row),
        out_shape=jax.ShapeDtypeStruct((n, D_MODEL), F32),
        compiler_params=_cparams("parallel"),
        name="matmul_residual",
    )(a, w, x)


def _nsa_prompt_layer(x, n_batch, norm_g, w_in, cmp_pe, cmp_w, w_out, tables):
    tm = min(512, x.shape[0] // n_batch)
    q, qr, cmp_rows, sel_rows, win_rows, gates, kvb = _nsa_proj(
        x, norm_g.reshape(1, D_MODEL), _nsa_w_in_padded(w_in), tables, n_batch, tm)
    bd, pe2 = _cmp_weights(cmp_w, cmp_pe)
    kvc = _nsa_compress_prompt(cmp_rows, bd, pe2, n_batch)
    o = _nsa_attn_prompt(q, qr, gates, kvc, kvb, n_batch)
    y = _matmul_residual(o, w_out.astype(BF16), x, tm)
    return y, cmp_rows, sel_rows, win_rows


_FFN_CW = MXU_DIM


def _ffn_kernel(x_ref, g_ref, win_ref, cw_ref, wout_ref, buf_ref, y_ref, st_ref, carry_scr, uv_scr, ug_scr,
                *, tm, tpb):
    ti = pl.program_id(0) % tpb
    x = x_ref[...]
    xn = _rmsnorm(x, g_ref[...]).astype(BF16)

    @pl.when(ti == 0)
    def _():
        carry_scr[SUBLANES - (FFN_CONV - 1):SUBLANES, :] = buf_ref[...]

    acc = jnp.zeros((tm, D_MODEL), F32)
    for c in range(D_FF // _FFN_CW):
        conv = []
        for part, scr in ((0, uv_scr), (1, ug_scr)):
            c0 = part * D_FF + c * _FFN_CW
            u = _dot(xn, win_ref[:, c0:c0 + _FFN_CW])
            scr[0:SUBLANES, :] = carry_scr[:, c0:c0 + _FFN_CW]
            scr[SUBLANES:SUBLANES + tm, :] = u
            carry_scr[:, c0:c0 + _FFN_CW] = u[tm - SUBLANES:tm, :]
            w = cw_ref[:, c0:c0 + _FFN_CW]
            conv.append(scr[SUBLANES - 2:SUBLANES - 2 + tm, :] * w[0:1] + scr[SUBLANES - 1:SUBLANES - 1 + tm, :] * w[1:2]
                        + u * w[2:3])
        act = conv[0] * jax.nn.silu(conv[1])
        acc = acc + _dot(act.astype(BF16), wout_ref[c * _FFN_CW:(c + 1) * _FFN_CW, :])
    y_ref[...] = x + acc

    @pl.when(ti == tpb - 1)
    def _():
        st_ref[...] = carry_scr[SUBLANES - (FFN_CONV - 1):SUBLANES, :]


def _ffn_prompt(x, n_batch, norm_g, w_in, conv_w, w_out, buf, tm=512):
    n = x.shape[0]
    tpb = n // n_batch // tm
    row = lambda i: (i, 0)
    st = pl.BlockSpec((None, FFN_CONV - 1, 2 * D_FF), lambda i: (i // tpb, 0, 0))
    return pl.pallas_call(
        functools.partial(_ffn_kernel, tm=tm, tpb=tpb),
        grid=(n // tm,),
        in_specs=[pl.BlockSpec((tm, D_MODEL), row), _resident((1, D_MODEL)), _resident(w_in.shape),
                  _resident(conv_w.shape), _resident(w_out.shape), st],
        out_specs=[pl.BlockSpec((tm, D_MODEL), row), st],
        out_shape=[jax.ShapeDtypeStruct((n, D_MODEL), F32),
                   jax.ShapeDtypeStruct((n_batch, FFN_CONV - 1, 2 * D_FF), F32)],
        scratch_shapes=[pltpu.VMEM((SUBLANES, 2 * D_FF), F32), pltpu.VMEM((SUBLANES + tm, _FFN_CW), F32),
                        pltpu.VMEM((SUBLANES + tm, _FFN_CW), F32)],
        compiler_params=_cparams("arbitrary"),
        name="ffn_prompt",
    )(x, norm_g.reshape(1, D_MODEL), w_in, conv_w, w_out, buf)


_GDN_W_COLS = GDN_CONV_CH + GDN_VW + LANES
_GDN_CW = MXU_DIM


def _softplus(x):
    return jnp.maximum(x, 0.0) + jnp.log1p(jnp.exp(-jnp.abs(x)))


def _gdn_w_in_padded(w_in):
    return jnp.pad(w_in, ((0, 0), (0, _GDN_W_COLS - w_in.shape[1]))).astype(BF16)


def _lane_row(v):
    return jnp.pad(v.astype(F32), (0, LANES - v.shape[0])).reshape(1, LANES)


def _gdn_post_conv(y, c0):
    y = jax.nn.silu(y)
    kind = c0 // GDN_QK
    if kind == 2:
        return y
    outs = []
    for hh in range(_GDN_CW // GDN_DK):
        seg = y[:, hh * GDN_DK:(hh + 1) * GDN_DK]
        seg = seg * lax.rsqrt(jnp.sum(seg * seg, axis=-1, keepdims=True) + EPS)
        outs.append(seg * GDN_DK ** -0.5 if kind == 0 else seg)
    return jnp.concatenate(outs, axis=1)


def _gdn_gates(ab, alog, dtb):
    lane = _iota(ab.shape, 1)
    gval = -jnp.exp(alog) * _softplus(ab + dtb)
    return jnp.where(lane < GDN_HEADS, gval, jax.nn.sigmoid(ab))


def _gdn_proj_kernel(x_ref, g_ref, w_ref, cw_ref, alog_ref, dtb_ref, buf_ref,
                     q_ref, k_ref, v_ref, gate_ref, gb_ref, st_ref, carry_scr, u_scr, *, tm, tpb):
    ti = pl.program_id(0) % tpb
    xn = _rmsnorm(x_ref[...], g_ref[...]).astype(BF16)
    keep = GDN_CONV - 1

    @pl.when(ti == 0)
    def _():
        carry_scr[SUBLANES - keep:SUBLANES, :] = buf_ref[...]

    outs = (q_ref, k_ref, v_ref)
    for c in range(GDN_CONV_CH // _GDN_CW):
        c0 = c * _GDN_CW
        u = _dot(xn, w_ref[:, c0:c0 + _GDN_CW])
        u_scr[0:SUBLANES, :] = carry_scr[:, c0:c0 + _GDN_CW]
        u_scr[SUBLANES:SUBLANES + tm, :] = u
        carry_scr[:, c0:c0 + _GDN_CW] = u[tm - SUBLANES:tm, :]
        w = cw_ref[:, c0:c0 + _GDN_CW]
        y = u_scr[SUBLANES - 3:SUBLANES - 3 + tm, :] * w[0:1]
        y = y + u_scr[SUBLANES - 2:SUBLANES - 2 + tm, :] * w[1:2]
        y = y + u_scr[SUBLANES - 1:SUBLANES - 1 + tm, :] * w[2:3]
        y = y + u * w[3:4]
        o_ref = outs[c0 // GDN_QK]
        o_ref[:, c0 % GDN_QK:c0 % GDN_QK + _GDN_CW] = _gdn_post_conv(y, c0)
    gate_ref[...] = _dot(xn, w_ref[:, GDN_CONV_CH:GDN_CONV_CH + GDN_VW])
    ab = _dot(xn, w_ref[:, GDN_CONV_CH + GDN_VW:_GDN_W_COLS])
    gb_ref[...] = _gdn_gates(ab, alog_ref[...], dtb_ref[...])

    @pl.when(ti == tpb - 1)
    def _():
        st_ref[...] = carry_scr[SUBLANES - keep:SUBLANES, :]


def _gdn_proj(x, n_batch, norm_g, w, conv_w, alog, dtb, buf, tm=256):
    n = x.shape[0]
    tpb = n // n_batch // tm
    row = lambda i: (i, 0)
    wide = pl.BlockSpec((tm, GDN_QK), row)
    st = pl.BlockSpec((None, GDN_CONV - 1, GDN_CONV_CH), lambda i: (i // tpb, 0, 0))
    f = jax.ShapeDtypeStruct((n, GDN_QK), F32)
    return pl.pallas_call(
        functools.partial(_gdn_proj_kernel, tm=tm, tpb=tpb),
        grid=(n // tm,),
        in_specs=[pl.BlockSpec((tm, D_MODEL), row), _resident((1, D_MODEL)), _resident(w.shape),
                  _resident(conv_w.shape), _resident((1, LANES)), _resident((1, LANES)), st],
        out_specs=[wide, wide, wide, wide, pl.BlockSpec((tm, LANES), row), st],
        out_shape=[f, f, f, f, jax.ShapeDtypeStruct((n, LANES), F32),
                   jax.ShapeDtypeStruct((n_batch, GDN_CONV - 1, GDN_CONV_CH), F32)],
        scratch_shapes=[pltpu.VMEM((SUBLANES, GDN_CONV_CH), F32), pltpu.VMEM((SUBLANES + tm, _GDN_CW), F32)],
        compiler_params=_cparams("arbitrary"),
        name="gdn_proj",
    )(x, norm_g.reshape(1, D_MODEL), w, conv_w, alog, dtb, buf)


_GDN_HG = MXU_DIM // GDN_CHUNK


def _gdn_block_diag(x):
    head = _iota(x.shape, 0) // GDN_CHUNK
    return jnp.concatenate([jnp.where(head == h, x, 0.0) for h in range(_GDN_HG)], axis=1)


def _gdn_group_prep(q, k, v, beta, gcum_col, gcum_row, g_last):
    n = q.shape[0]
    ii = _iota((n, n), 0)
    jj = _iota((n, n), 1)
    same = (ii // GDN_CHUNK) == (jj // GDN_CHUNK)
    decay = jnp.exp(jnp.where(same & (ii >= jj), gcum_col - gcum_row, NEG))
    kb = k * beta
    k16 = k.astype(BF16)
    low = _dot_nt(kb.astype(BF16), k16) * decay * jnp.where(ii > jj, 1.0, 0.0)
    attn = _dot_nt(q.astype(BF16), k16) * decay
    x = jnp.where(ii == jj, 1.0, 0.0) - low
    p = low
    for _ in range(GDN_CHUNK.bit_length() - 2):
        p16 = p.astype(BF16)
        p = _dot(p16, p16)
        x = x + _dot(x.astype(BF16), p.astype(BF16))
    rhs = jnp.concatenate([v * beta, kb * jnp.exp(gcum_col)], axis=1)
    sol = _dot(x.astype(BF16), rhs.astype(BF16))
    u_base, w_dec = sol[:, :GDN_DV], sol[:, GDN_DV:]
    return (u_base, _gdn_block_diag(w_dec).astype(BF16), _gdn_block_diag(q * jnp.exp(gcum_col)).astype(BF16),
            attn.astype(BF16), _gdn_block_diag(k * jnp.exp(g_last - gcum_col)).astype(BF16))


def _gdn_group_step(prep, s_prev, s_decay):
    u_base, w_dec_bd, q_bd, attn, k_dec_bd = prep
    s16 = s_prev.astype(BF16)
    u = u_base - _dot(w_dec_bd, s16)
    u16 = u.astype(BF16)
    o = _dot(q_bd, s16) + _dot(attn, u16)
    s_new = s_prev * s_decay + _dot_tn(k_dec_bd, u16)
    return o, s_new


def _gdn_chunk_kernel(q_ref, k_ref, v_ref, gb_ref, gate_ref, x_ref, s0_ref, ng_ref, wout_ref,
                      y_ref, s_ref, s_scr, o_scr, *, rt, tpb):
    ti = pl.program_id(1)
    n_groups = GDN_HEADS // _GDN_HG

    @pl.when(ti == 0)
    def _():
        s_scr[...] = s0_ref[...].reshape(n_groups, _GDN_HG * GDN_DK, GDN_DV)

    c = GDN_CHUNK
    tri = jnp.where(_iota((c, c), 0) >= _iota((c, c), 1), 1.0, 0.0)
    stack_rows = lambda ref, r0, h0: jnp.concatenate(
        [ref[r0:r0 + c, (h0 + h) * GDN_DK:(h0 + h + 1) * GDN_DK] for h in range(_GDN_HG)], axis=0)

    preps, decays = [], []
    for ci in range(rt // c):
        r0 = ci * c
        gb = gb_ref[r0:r0 + c, :]
        gcum = _dot_f32(tri, gb)
        gcum_t = gcum.T
        for gi in range(n_groups):
            h0 = gi * _GDN_HG
            heads = range(h0, h0 + _GDN_HG)
            col = lambda a: jnp.concatenate([a[:, h:h + 1] for h in heads], axis=0)
            g_last = jnp.concatenate([jnp.broadcast_to(gcum[c - 1:c, h:h + 1], (c, 1)) for h in heads], axis=0)
            gcum_row = jnp.concatenate([gcum_t[h:h + 1, :] for h in heads], axis=1)
            beta = jnp.concatenate([gb[:, GDN_HEADS + h:GDN_HEADS + h + 1] for h in heads], axis=0)
            preps.append(_gdn_group_prep(stack_rows(q_ref, r0, h0), stack_rows(k_ref, r0, h0),
                                         stack_rows(v_ref, r0, h0), beta, col(gcum), gcum_row, g_last))
            decays.append(jnp.concatenate(
                [jnp.broadcast_to(jnp.exp(gcum[c - 1:c, h:h + 1]), (GDN_DK, 1)) for h in heads], axis=0))

    for gi in range(n_groups):
        s = s_scr[gi]
        for ci in range(rt // c):
            o, s = _gdn_group_step(preps[ci * n_groups + gi], s, decays[ci * n_groups + gi])
            for h in range(_GDN_HG):
                o_scr[ci * c:(ci + 1) * c, (gi * _GDN_HG + h) * GDN_DV:(gi * _GDN_HG + h + 1) * GDN_DV] = (
                    o[h * c:(h + 1) * c])
        s_scr[gi] = s

    ng = ng_ref[...]
    outs = []
    for h in range(GDN_HEADS):
        cols = slice(h * GDN_DV, (h + 1) * GDN_DV)
        outs.append((_rmsnorm(o_scr[:, cols], ng) * jax.nn.silu(gate_ref[:, cols])).astype(BF16))
    y_ref[...] = x_ref[...] + _dot(jnp.concatenate(outs, axis=1), wout_ref[...])

    @pl.when(ti == tpb - 1)
    def _():
        s_ref[...] = s_scr[...].reshape(GDN_HEADS, GDN_DK, GDN_DV)


def _gdn_chunked(q, k, v, gb, gate, x, s0, norm_g, w_out, n_batch, rt=256):
    n = x.shape[0]
    tpb = n // n_batch // rt
    row = lambda b, i: (b * tpb + i, 0)
    wide = pl.BlockSpec((rt, GDN_QK), row)
    st = pl.BlockSpec((None, GDN_HEADS, GDN_DK, GDN_DV), lambda b, i: (b, 0, 0, 0))
    return pl.pallas_call(
        functools.partial(_gdn_chunk_kernel, rt=rt, tpb=tpb),
        grid=(n_batch, tpb),
        in_specs=[wide, wide, wide, pl.BlockSpec((rt, LANES), row), wide, wide, st,
                  _resident((1, GDN_DV)), _resident(w_out.shape)],
        out_specs=[wide, st],
        out_shape=[jax.ShapeDtypeStruct((n, D_MODEL), F32),
                   jax.ShapeDtypeStruct((n_batch, GDN_HEADS, GDN_DK, GDN_DV), F32)],
        scratch_shapes=[pltpu.VMEM((GDN_HEADS // _GDN_HG, _GDN_HG * GDN_DK, GDN_DV), F32),
                        pltpu.VMEM((rt, GDN_VW), F32)],
        compiler_params=_cparams("parallel", "arbitrary"),
        name="gdn_chunked",
    )(q, k, v, gb, gate, x, s0, norm_g.reshape(1, GDN_DV), w_out)


def _gdn_prompt_layer(x, n_batch, norm_g, s0, conv_buf, w_in, conv_w, a_log, dt_bias, head_norm_g, w_out):
    q, k, v, gate, gb, new_buf = _gdn_proj(x, n_batch, norm_g, _gdn_w_in_padded(w_in), conv_w,
                                           _lane_row(a_log), _lane_row(dt_bias), conv_buf)
    y, s_new = _gdn_chunked(q, k, v, gb, gate, x, s0, head_norm_g, w_out.astype(BF16), n_batch)
    return y, s_new, new_buf


_LRU_CW = MXU_DIM


def _lru_gates(xc, wga_ref, bga, wgx_ref, bgx, lam, n0):
    r_parts, i_parts = [], []
    for j in range(xc.shape[1] // LRU_BW):
        blk = xc[:, j * LRU_BW:(j + 1) * LRU_BW].astype(BF16)
        r_parts.append(_dot(blk, wga_ref[n0 + j]))
        i_parts.append(_dot(blk, wgx_ref[n0 + j]))
    cat = lambda ps: ps[0] if len(ps) == 1 else jnp.concatenate(ps, axis=1)
    r = jax.nn.sigmoid(cat(r_parts) + bga)
    i = jax.nn.sigmoid(cat(i_parts) + bgx)
    log_a = -LRU_C * r * _softplus(-lam)
    a = jnp.exp(log_a)
    b = jnp.sqrt(-jnp.tanh(log_a) * (a * a + 1.0)) * i * xc
    return a, b


def _lru_proj_kernel(x_ref, g_ref, w_ref, cw_ref, cb_ref, wga_ref, bga_ref, wgx_ref, bgx_ref, lam_ref, buf_ref,
                     gate_ref, a_ref, b_ref, st_ref, carry_scr, u_scr, *, tm, tpb):
    ti = pl.program_id(0) % tpb
    xn = _rmsnorm(x_ref[...], g_ref[...]).astype(BF16)
    keep = LRU_CONV - 1

    @pl.when(ti == 0)
    def _():
        carry_scr[SUBLANES - keep:SUBLANES, :] = buf_ref[...]

    gate_ref[...] = _dot(xn, w_ref[:, 0:D_RNN])
    for c in range(D_RNN // _LRU_CW):
        c0 = c * _LRU_CW
        cols = slice(c0, c0 + _LRU_CW)
        u = _dot(xn, w_ref[:, D_RNN + c0:D_RNN + c0 + _LRU_CW])
        u_scr[0:SUBLANES, :] = carry_scr[:, cols]
        u_scr[SUBLANES:SUBLANES + tm, :] = u
        carry_scr[:, cols] = u[tm - SUBLANES:tm, :]
        w = cw_ref[:, cols]
        xc = u_scr[SUBLANES - 3:SUBLANES - 3 + tm, :] * w[0:1]
        xc = xc + u_scr[SUBLANES - 2:SUBLANES - 2 + tm, :] * w[1:2]
        xc = xc + u_scr[SUBLANES - 1:SUBLANES - 1 + tm, :] * w[2:3]
        xc = xc + u * w[3:4] + cb_ref[:, cols]
        a, b = _lru_gates(xc, wga_ref, bga_ref[:, cols], wgx_ref, bgx_ref[:, cols], lam_ref[:, cols],
                          c0 // LRU_BW)
        a_ref[:, cols] = a
        b_ref[:, cols] = b

    @pl.when(ti == tpb - 1)
    def _():
        st_ref[...] = carry_scr[SUBLANES - keep:SUBLANES, :]


def _lru_proj(x, n_batch, norm_g, w, conv_w, conv_b, w_ga, b_ga, w_gx, b_gx, lam, buf, tm=256):
    n = x.shape[0]
    tpb = n // n_batch // tm
    row = lambda i: (i, 0)
    wide = pl.BlockSpec((tm, D_RNN), row)
    vec = _resident((1, D_RNN))
    st = pl.BlockSpec((None, LRU_CONV - 1, D_RNN), lambda i: (i // tpb, 0, 0))
    f = jax.ShapeDtypeStruct((n, D_RNN), F32)
    r1 = lambda v: v.reshape(1, D_RNN)
    return pl.pallas_call(
        functools.partial(_lru_proj_kernel, tm=tm, tpb=tpb),
        grid=(n // tm,),
        in_specs=[pl.BlockSpec((tm, D_MODEL), row), _resident((1, D_MODEL)), _resident(w.shape),
                  _resident(conv_w.shape), vec, _resident(w_ga.shape), vec, _resident(w_gx.shape), vec, vec, st],
        out_specs=[wide, wide, wide, st],
        out_shape=[f, f, f, jax.ShapeDtypeStruct((n_batch, LRU_CONV - 1, D_RNN), F32)],
        scratch_shapes=[pltpu.VMEM((SUBLANES, D_RNN), F32), pltpu.VMEM((SUBLANES + tm, _LRU_CW), F32)],
        compiler_params=_cparams("arbitrary"),
        name="lru_proj",
    )(x, norm_g.reshape(1, D_MODEL), w, conv_w, r1(conv_b), w_ga, r1(b_ga), w_gx, r1(b_gx), r1(lam), buf)


def _lru_scan_kernel(a_ref, b_ref, gate_ref, x_ref, h0_ref, wout_ref, y_ref, hl_ref, h_scr, hs_scr, *, rt, tpb):
    ti = pl.program_id(1)

    @pl.when(ti == 0)
    def _():
        h_scr[...] = h0_ref[...]

    def step(t, h):
        h = a_ref[pl.ds(t, 1), :] * h + b_ref[pl.ds(t, 1), :]
        hs_scr[pl.ds(t, 1), :] = h
        return h

    h = lax.fori_loop(0, rt, step, h_scr[...], unroll=8)
    h_scr[...] = h
    y = (hs_scr[...] * jax.nn.gelu(gate_ref[...])).astype(BF16)
    y_ref[...] = x_ref[...] + _dot(y, wout_ref[...])

    @pl.when(ti == tpb - 1)
    def _():
        hl_ref[...] = h


def _lru_scan(a, b, gate, x, h0, w_out, n_batch, rt=256):
    n = x.shape[0]
    tpb = n // n_batch // rt
    wide = pl.BlockSpec((rt, D_RNN), lambda bb, i: (bb * tpb + i, 0))
    st = pl.BlockSpec((None, 1, D_RNN), lambda bb, i: (bb, 0, 0))
    y, hl = pl.pallas_call(
        functools.partial(_lru_scan_kernel, rt=rt, tpb=tpb),
        grid=(n_batch, tpb),
        in_specs=[wide, wide, wide, wide, st, _resident(w_out.shape)],
        out_specs=[wide, st],
        out_shape=[jax.ShapeDtypeStruct((n, D_MODEL), F32), jax.ShapeDtypeStruct((n_batch, 1, D_RNN), F32)],
        scratch_shapes=[pltpu.VMEM((1, D_RNN), F32), pltpu.VMEM((rt, D_RNN), F32)],
        compiler_params=_cparams("parallel", "arbitrary"),
        name="lru_scan",
    )(a, b, gate, x, h0.reshape(n_batch, 1, D_RNN), w_out)
    return y, hl.reshape(n_batch, D_RNN)


def _lru_prompt_layer(x, n_batch, norm_g, h0, conv_buf, w_in, conv_w, conv_b, w_ga, b_ga, w_gx, b_gx, lam, w_out):
    gate, a, b, new_buf = _lru_proj(x, n_batch, norm_g, w_in.astype(BF16), conv_w, conv_b, w_ga.astype(BF16), b_ga,
                                    w_gx.astype(BF16), b_gx, lam, conv_buf)
    y, h_last = _lru_scan(a, b, gate, x, h0, w_out.astype(BF16), n_batch)
    return y, h_last, new_buf


def _final_norm_kernel(x_ref, g_ref, o_ref):
    o_ref[...] = _rmsnorm(x_ref[...], g_ref[...])


def _final_norm(x, g, tm):
    n = x.shape[0]
    row = lambda i: (i, 0)
    return pl.pallas_call(
        _final_norm_kernel,
        grid=(n // tm,),
        in_specs=[pl.BlockSpec((tm, D_MODEL), row), _resident((1, D_MODEL))],
        out_specs=pl.BlockSpec((tm, D_MODEL), row),
        out_shape=jax.ShapeDtypeStruct((n, D_MODEL), F32),
        compiler_params=_cparams("parallel"),
        name="final_norm",
    )(x, g.reshape(1, D_MODEL))


def _norm_matmul_small_kernel(x_ref, g_ref, w_ref, o_ref):
    xn = _rmsnorm(x_ref[...], g_ref[...]).astype(BF16)
    o_ref[...] = _dot(xn, w_ref[...])


def _norm_matmul_small(x, g, w, n_col_tiles):
    n, k = x.shape[0], w.shape[1]
    tn = k // n_col_tiles
    return pl.pallas_call(
        _norm_matmul_small_kernel,
        grid=(n_col_tiles,),
        in_specs=[_resident((n, D_MODEL)), _resident((1, D_MODEL)), pl.BlockSpec((D_MODEL, tn), lambda j: (0, j))],
        out_specs=pl.BlockSpec((n, tn), lambda j: (0, j)),
        out_shape=jax.ShapeDtypeStruct((n, k), F32),
        compiler_params=_cparams("parallel"),
        name="norm_matmul_small",
    )(x, g.reshape(1, D_MODEL), w)


_CMP_PAGES_PER_STEP = 8
_CMP_BLK_PER_PAGE = PAGE_SIZE // L_CMP


def _nsa_compress_sample_kernel(pt_ref, *refs, n_pages, n_blk, n_out):
    pages = refs[:_CMP_PAGES_PER_STEP]
    new_ref, bd_ref, pe_ref, o_ref, rows_scr = refs[_CMP_PAGES_PER_STEP:]
    b = pl.program_id(0)
    pg = pl.program_id(1)
    past_blk = n_pages * _CMP_BLK_PER_PAGE
    pair_rows = 2 * PAGE_SIZE
    out_row = _iota((pair_rows, pair_rows), 0)
    src_row = _iota((pair_rows, pair_rows), 1)
    blk_per_pair = 2 * _CMP_BLK_PER_PAGE
    perm = jnp.where(src_row == (out_row % blk_per_pair) * L_CMP + out_row // blk_per_pair, 1.0, 0.0).astype(BF16)
    for r in range(0, _CMP_PAGES_PER_STEP, 2):
        c0 = pl.multiple_of((pg * _CMP_PAGES_PER_STEP + r) * _CMP_BLK_PER_PAGE, blk_per_pair)
        for plane in range(4):
            kv, g0 = plane // 2, 2 * (plane % 2)
            two = jnp.concatenate(
                [pages[r + i][kv, g0:g0 + 2].reshape(2 * NSA_DH, PAGE_SIZE) for i in range(2)], axis=1)
            moved = _dot_nt(perm, two.astype(BF16))
            for l in range(L_CMP):
                rows_scr[plane, l, pl.ds(c0, blk_per_pair), :] = moved[l * blk_per_pair:(l + 1) * blk_per_pair]

    @pl.when(pg == pl.num_programs(1) - 1)
    def _():
        new = new_ref[pl.ds(b, 1), :]
        n_tail = n_blk - past_blk
        first = _iota((n_tail, LANES), 0) == 0
        for plane in range(4):
            for l in range(L_CMP):
                tail = jnp.where(first, new[:, plane * LANES:(plane + 1) * LANES], 0.0) if l == 0 else (
                    jnp.zeros((n_tail, LANES), F32))
                rows_scr[plane, l, past_blk:n_blk, :] = tail
        o_ref[...] = jnp.zeros(o_ref.shape, F32)

        def load_rows(plane, l):
            return rows_scr[plane, l]

        _compress_planes(load_rows, bd_ref, pe_ref, o_ref.at[:, :, 0:n_blk, :], n_blk)


def _rows_last(cache):
    nd = cache.ndim
    return jnp.transpose(cache, tuple(range(nd - 4)) + (nd - 3, nd - 2, nd - 1, nd - 4))


def _nsa_compress_sample(cache_t, layer, page_table, new_rows, bd, pe2):
    n_b, n_pages = page_table.shape
    past = n_pages * PAGE_SIZE
    n_blk = -(-(past + 1) // L_SEL) * L_SEL // L_CMP
    n_blk = -(-n_blk // SUBLANES) * SUBLANES
    n_out = -(-n_blk // LANES) * LANES
    cache4 = cache_t

    def page_spec(r):
        return pl.BlockSpec((None, None, 2, NSA_KV, NSA_DH, PAGE_SIZE),
                            lambda b, pg, pt: (layer, pt[b, pg * _CMP_PAGES_PER_STEP + r], 0, 0, 0, 0))

    grid_spec = pltpu.PrefetchScalarGridSpec(
        num_scalar_prefetch=1,
        grid=(n_b, n_pages // _CMP_PAGES_PER_STEP),
        in_specs=[page_spec(r) for r in range(_CMP_PAGES_PER_STEP)]
        + [pl.BlockSpec(new_rows.shape, lambda b, pg, pt: (0, 0)),
           pl.BlockSpec(bd.shape, lambda b, pg, pt: (0, 0, 0, 0)),
           pl.BlockSpec(pe2.shape, lambda b, pg, pt: (0, 0, 0))],
        out_specs=pl.BlockSpec((None, 2, NSA_KV, n_out, NSA_DH), lambda b, pg, pt: (b, 0, 0, 0, 0)),
        scratch_shapes=[pltpu.VMEM((4, L_CMP, n_blk, LANES), F32)])
    return pl.pallas_call(
        functools.partial(_nsa_compress_sample_kernel, n_pages=n_pages, n_blk=n_blk, n_out=n_out),
        grid_spec=grid_spec,
        out_shape=jax.ShapeDtypeStruct((n_b, 2, NSA_KV, n_out, NSA_DH), F32),
        compiler_params=_cparams("parallel", "arbitrary"),
        name="nsa_compress_sample",
    )(page_table, *([cache4] * _CMP_PAGES_PER_STEP), new_rows, bd, pe2)


def _pad_rows8(x):
    return jnp.concatenate([x, jnp.zeros((SUBLANES - x.shape[0], x.shape[1]), x.dtype)], axis=0)


_TOPK_ROWS_PER_STEP = 8


def _nsa_cmp_topk_sample_kernel(q_ref, kvc_ref, oc_ref, idx_ref, *, qpos, n_sel):
    nb = q_ref.shape[0]
    n_cmp = kvc_ref.shape[3]
    n_lane = 2 * LANES
    c_end = (_iota((SUBLANES, n_cmp), 1) + 1) * L_CMP - 1
    valid = c_end <= qpos
    any_valid = 1.0 if qpos >= L_CMP - 1 else 0.0
    pair_mat = jnp.where((_iota((n_cmp, n_lane), 0) >> 1) == _iota((n_cmp, n_lane), 1), 1.0, 0.0)
    imps = []
    for i in range(nb):
        for g in range(NSA_KV):
            q8 = _pad_rows8(q_ref[i, g * NSA_HPG:(g + 1) * NSA_HPG, :])
            s = _dot_nt(q8, kvc_ref[i, 0, g].astype(BF16))
            p = _softmax_rows(jnp.where(valid, s, NEG)) * any_valid
            o_c = _dot(p.astype(BF16), kvc_ref[i, 1, g].astype(BF16))
            oc_ref[i, g * NSA_HPG:(g + 1) * NSA_HPG, :] = o_c[0:NSA_HPG]
            imps.append(p[0:1] + p[1:2] + p[2:3] + p[3:4])
        imps.append(jnp.zeros((SUBLANES - NSA_KV, n_cmp), F32))
    rows = nb * SUBLANES
    imp = jnp.concatenate(imps, axis=0)
    pair = _dot_f32(imp, pair_mat)
    blk = _iota((rows, n_lane), 1)
    forced = (blk == 0) | (blk == qpos // L_SEL)
    avail = blk * L_SEL <= qpos
    val = jnp.where(forced, FORCE, jnp.where(avail, pair, -1.0))
    val = jnp.where(blk < n_sel, val, -2.0)
    out = jnp.zeros((rows, LANES), I32)
    out_lane = _iota((rows, LANES), 1)
    for r in range(N_SEL):
        m = jnp.max(val, axis=-1, keepdims=True)
        first = jnp.min(jnp.where(val == m, blk, n_lane), axis=-1, keepdims=True)
        out = jnp.where(out_lane == r, first, out)
        val = jnp.where(blk == first, -3.0, val)
    idx_ref[...] = out.reshape(nb, SUBLANES, LANES)


def _nsa_cmp_topk_sample(q3, kvc, qpos, n_sel):
    n_b = q3.shape[0]
    n_cmp = kvc.shape[3]
    nb = _TOPK_ROWS_PER_STEP if n_b % _TOPK_ROWS_PER_STEP == 0 else 1
    return pl.pallas_call(
        functools.partial(_nsa_cmp_topk_sample_kernel, qpos=qpos, n_sel=n_sel),
        grid=(n_b // nb,),
        in_specs=[pl.BlockSpec((nb, NSA_HEADS, NSA_DH), lambda b: (b, 0, 0)),
                  pl.BlockSpec((nb, 2, NSA_KV, n_cmp, NSA_DH), lambda b: (b, 0, 0, 0, 0))],
        out_specs=[pl.BlockSpec((nb, NSA_HEADS, NSA_DH), lambda b: (b, 0, 0)),
                   pl.BlockSpec((nb, SUBLANES, LANES), lambda b: (b, 0, 0))],
        out_shape=[jax.ShapeDtypeStruct((n_b, NSA_HEADS, NSA_DH), F32),
                   jax.ShapeDtypeStruct((n_b, SUBLANES, LANES), I32)],
        compiler_params=_cparams("parallel"),
        name="nsa_cmp_topk_sample",
    )(q3, kvc)


def _nsa_gather_sel_kernel(pt_ref, idx_ref, *refs):
    blocks = refs[:NSA_KV * N_SEL]
    o_ref = refs[NSA_KV * N_SEL]
    for g in range(NSA_KV):
        for kv in range(2):
            pages = [blocks[g * N_SEL + r][kv] for r in range(N_SEL)]
            o_ref[g, kv] = jnp.concatenate(pages, axis=1).astype(BF16)


def _nsa_gather_sel(cache_t, layer, page_table, idx):
    n_b, n_pages = page_table.shape
    past = n_pages * PAGE_SIZE
    halves = PAGE_SIZE // L_SEL
    last_blk = past // L_SEL - 1

    def blk_spec(n):
        g = n // N_SEL

        def index(b, pt, ix):
            s = jnp.clip(ix[b, n], 0, last_blk)
            return (layer, pt[b, s // halves], 0, g, 0, 0)
        return pl.BlockSpec((None, None, 2, None, NSA_DH, PAGE_SIZE), index)

    grid_spec = pltpu.PrefetchScalarGridSpec(
        num_scalar_prefetch=2,
        grid=(n_b,),
        in_specs=[blk_spec(n) for n in range(NSA_KV * N_SEL)],
        out_specs=pl.BlockSpec((None, NSA_KV, 2, NSA_DH, N_SEL * PAGE_SIZE), lambda b, pt, ix: (b, 0, 0, 0, 0)))
    return pl.pallas_call(
        _nsa_gather_sel_kernel,
        grid_spec=grid_spec,
        out_shape=jax.ShapeDtypeStruct((n_b, NSA_KV, 2, NSA_DH, N_SEL * PAGE_SIZE), BF16),
        name="nsa_gather_sel",
    )(page_table, idx, *([cache_t] * (NSA_KV * N_SEL)))


def _attend_with_new(q8, k_t, v_t, live, k_new, v_new, new_live):
    s = jnp.where(live > 0.5, _dot(q8, k_t), NEG)
    k_new16 = k_new.astype(BF16).astype(F32)
    s_new = jnp.sum(q8.astype(F32) * k_new16, axis=-1, keepdims=True)
    s_new = jnp.where(new_live > 0.5, s_new, NEG)
    m = jnp.maximum(jnp.max(s, axis=-1, keepdims=True), s_new)
    e = jnp.exp(s - m)
    e_new = jnp.exp(s_new - m)
    inv = 1.0 / (jnp.sum(e, axis=-1, keepdims=True) + e_new)
    p16 = (e * inv).astype(BF16)
    p_new16 = (e_new * inv).astype(BF16).astype(F32)
    return _dot_nt(p16, v_t) + p_new16 * v_new.astype(BF16).astype(F32)


def _row_to_col(row):
    return jnp.broadcast_to(row, (SUBLANES, row.shape[1])).T[:, 0:1]


def _nsa_sel_win_sample_kernel(idx_ref, qr_ref, oc_ref, gate_ref, selnew_ref, winnew_ref, win_ref, kv_ref,
                               o_ref, wout_ref, *, qpos, past):
    gates = gate_ref[...]
    glane = _iota((1, LANES), 1)
    n_key = N_SEL * PAGE_SIZE
    lane_k = _iota((SUBLANES, n_key), 1)
    log_page = PAGE_SIZE.bit_length() - 1
    wb = win_ref.shape[3]
    wlane = _iota((SUBLANES, wb), 1)
    kwpos = past - wb + wlane
    dist = qpos - kwpos
    wlive = jnp.where((dist >= 0) & (dist <= WINDOW) & (kwpos >= 0), 1.0, 0.0)
    new_in_window = jnp.full((1, 1), 1.0 if 0 <= qpos - past <= WINDOW else 0.0, F32)
    expand = jnp.where(_iota((LANES, n_key), 0) == (_iota((LANES, n_key), 1) >> log_page), 1.0, 0.0)
    idx_f = idx_ref[...].astype(F32)
    blk_of_lane = _dot_f32(jnp.where(_iota((SUBLANES, LANES), 1) < N_SEL, idx_f, 0.0), expand)
    sel_new = selnew_ref[...]
    win_new = winnew_ref[...]
    last_lane = _iota((NSA_DH, wb), 1) == wb - 1
    for g in range(NSA_KV):
        kc = slice(g * NSA_DH, (g + 1) * NSA_DH)
        vc = slice(NSA_KVW + g * NSA_DH, NSA_KVW + (g + 1) * NSA_DH)
        q8 = _pad_rows8(qr_ref[g * NSA_HPG:(g + 1) * NSA_HPG, :])
        blk_g = blk_of_lane[g:g + 1, :]
        new_blk = float(qpos // L_SEL)
        new_selected = jnp.max(jnp.where(blk_g == new_blk, 1.0, 0.0), axis=-1, keepdims=True)
        halves = PAGE_SIZE // L_SEL
        page_g = jnp.floor(jnp.minimum(blk_g, float(past // L_SEL - 1)) * (1.0 / halves))
        kpos = page_g * float(PAGE_SIZE) + (lane_k & (PAGE_SIZE - 1)).astype(F32)
        in_block = jnp.floor(kpos * (1.0 / L_SEL)) == blk_g
        live = jnp.where(in_block & (kpos < float(past)) & (kpos <= float(qpos)), 1.0, 0.0)
        o_s = _attend_with_new(q8, kv_ref[g, 0], kv_ref[g, 1], live, sel_new[:, kc], sel_new[:, vc], new_selected)
        kw_t = win_ref[0, g]
        vw_t = win_ref[1, g]
        o_w = _attend_with_new(q8, kw_t.astype(BF16), vw_t.astype(BF16), wlive, win_new[:, kc], win_new[:, vc],
                               new_in_window)
        wout_ref[0, g] = jnp.where(last_lane, _row_to_col(win_new[:, kc]), pltpu.roll(kw_t, wb - 1, 1))
        wout_ref[1, g] = jnp.where(last_lane, _row_to_col(win_new[:, vc]), pltpu.roll(vw_t, wb - 1, 1))
        o_c = oc_ref[g * NSA_HPG:(g + 1) * NSA_HPG, :]
        rows = []
        for j in range(NSA_HPG):
            acc = jnp.zeros((1, NSA_DH), F32)
            for br, ob in enumerate((o_c, o_s, o_w)):
                col = br * NSA_HEADS + g * NSA_HPG + j
                gcol = jnp.sum(jnp.where(glane == col, gates, 0.0), axis=-1, keepdims=True)
                acc = acc + gcol * ob[j:j + 1, :]
            rows.append(acc)
        o_ref[g * NSA_HPG:(g + 1) * NSA_HPG, :] = jnp.concatenate(rows, axis=0)


def _nsa_sel_win_sample(kv_sel, idx8, qr3, oc3, gates, sel_new, win_new, win_t, qpos, past):
    n_b = qr3.shape[0]
    wb = win_t.shape[-1]
    b3 = lambda b: (b, 0, 0)
    b5 = lambda b: (b, 0, 0, 0, 0)
    wspec = pl.BlockSpec((None, 2, NSA_KV, NSA_DH, wb), b5)
    return pl.pallas_call(
        functools.partial(_nsa_sel_win_sample_kernel, qpos=qpos, past=past),
        grid=(n_b,),
        in_specs=[pl.BlockSpec((None, SUBLANES, LANES), b3),
                  pl.BlockSpec((None, NSA_HEADS, NSA_DH), b3), pl.BlockSpec((None, NSA_HEADS, NSA_DH), b3),
                  pl.BlockSpec((None, 1, LANES), b3), pl.BlockSpec((None, 1, 2 * NSA_KVW), b3),
                  pl.BlockSpec((None, 1, 2 * NSA_KVW), b3), wspec,
                  pl.BlockSpec((None, NSA_KV, 2, NSA_DH, N_SEL * PAGE_SIZE), b5)],
        out_specs=[pl.BlockSpec((None, NSA_HEADS, NSA_DH), b3), wspec],
        out_shape=[jax.ShapeDtypeStruct((n_b, NSA_HEADS, NSA_DH), F32), jax.ShapeDtypeStruct(win_t.shape, F32)],
        compiler_params=_cparams("parallel"),
        name="nsa_sel_win_sample",
    )(idx8, qr3, oc3, gates.reshape(n_b, 1, LANES), sel_new.reshape(n_b, 1, -1),
      win_new.reshape(n_b, 1, -1), win_t, kv_sel)


def _nsa_sample_layer(x, layer, norm_g, w_in, cmp_pe, cmp_w, w_out, cache_cmp, cache_sel, win_state, page_table,
                      tables):
    n_b = x.shape[0]
    past = page_table.shape[1] * PAGE_SIZE
    qpos = past
    q, qr, cmp_rows, sel_rows, win_rows, gates, _ = _nsa_proj(
        x, norm_g.reshape(1, D_MODEL), _nsa_w_in_padded(w_in), tables, 1, n_b)
    bd, pe2 = _cmp_weights(cmp_w, cmp_pe)
    kvc = _nsa_compress_sample(cache_cmp, layer, page_table, cmp_rows, bd, pe2)
    n_sel = -(-(past + 1) // L_SEL)
    oc3, idx8 = _nsa_cmp_topk_sample(q.reshape(n_b, NSA_HEADS, NSA_DH), kvc, qpos, n_sel)
    idx = idx8[:, :NSA_KV, :N_SEL].reshape(n_b, NSA_KV * N_SEL)
    kv_sel = _nsa_gather_sel(cache_sel, layer, page_table, idx)
    o3, win_out = _nsa_sel_win_sample(kv_sel, idx8, qr.reshape(n_b, NSA_HEADS, NSA_DH), oc3, gates,
                                      sel_rows, win_rows, win_state, qpos, past)
    y = _matmul_residual(o3.reshape(n_b, NSA_Q), w_out.astype(BF16), x, n_b)
    return y, cmp_rows, sel_rows, win_out


def _gdn_sample_pre_kernel(z_ref, b0_ref, b1_ref, b2_ref, cw_ref, alog_ref, dtb_ref, q_ref, k_ref, v_ref, gb_ref):
    outs = (q_ref, k_ref, v_ref)
    for c in range(GDN_CONV_CH // _GDN_CW):
        c0 = c * _GDN_CW
        cols = slice(c0, c0 + _GDN_CW)
        w = cw_ref[:, cols]
        y = b0_ref[:, cols] * w[0:1] + b1_ref[:, cols] * w[1:2] + b2_ref[:, cols] * w[2:3] + z_ref[:, cols] * w[3:4]
        o_ref = outs[c0 // GDN_QK]
        o_ref[:, c0 % GDN_QK:c0 % GDN_QK + _GDN_CW] = _gdn_post_conv(y, c0)
    gb_ref[...] = _gdn_gates(z_ref[:, GDN_CONV_CH + GDN_VW:_GDN_W_COLS], alog_ref[...], dtb_ref[...])


def _gdn_sample_pre(z, bufs, conv_w, alog, dtb):
    n = z.shape[0]
    f = jax.ShapeDtypeStruct((n, GDN_QK), F32)
    return pl.pallas_call(
        _gdn_sample_pre_kernel,
        out_shape=[f, f, f, jax.ShapeDtypeStruct((n, LANES), F32)],
        compiler_params=pltpu.CompilerParams(vmem_limit_bytes=VMEM_LIMIT),
        name="gdn_sample_pre",
    )(z, *bufs, conv_w, alog, dtb)


def _gdn_sample_state_kernel(s0_ref, qt_ref, kt_ref, v_ref, gb_ref, gate_ref, ng_ref, s_ref, o_ref):
    gb = gb_ref[...]
    lane = _iota((1, LANES), 1)
    ng = ng_ref[...]
    for h in range(GDN_HEADS):
        cols = slice(h * GDN_DV, (h + 1) * GDN_DV)
        g_h = jnp.sum(jnp.where(lane == h, gb, 0.0), axis=-1, keepdims=True)
        beta = jnp.sum(jnp.where(lane == GDN_HEADS + h, gb, 0.0), axis=-1, keepdims=True)
        eg = jnp.exp(g_h)
        s_prev = s0_ref[h]
        k_col = kt_ref[:, h:h + 1]
        q_col = qt_ref[:, h:h + 1]
        v_row = v_ref[:, cols]
        u = v_row * beta - jnp.sum(s_prev * (k_col * beta * eg), axis=0, keepdims=True)
        attn = jnp.sum(q_col * k_col, axis=0, keepdims=True)
        o = jnp.sum(s_prev * (q_col * eg), axis=0, keepdims=True) + attn * u
        s_ref[h] = s_prev * eg + k_col * u
        o_ref[:, cols] = (_rmsnorm(o, ng) * jax.nn.silu(gate_ref[:, cols])).astype(BF16)


def _gdn_sample_state(s0, q_t, k_t, v, gb, gate, head_norm_g):
    n_b = s0.shape[0]
    st = pl.BlockSpec((None, GDN_HEADS, GDN_DK, GDN_DV), lambda b: (b, 0, 0, 0))
    col = pl.BlockSpec((None, GDN_DK, GDN_HEADS), lambda b: (b, 0, 0))
    wide = pl.BlockSpec((None, 1, GDN_VW), lambda b: (b, 0, 0))
    return pl.pallas_call(
        _gdn_sample_state_kernel,
        grid=(n_b,),
        in_specs=[st, col, col, wide, pl.BlockSpec((None, 1, LANES), lambda b: (b, 0, 0)), wide,
                  _resident((1, GDN_DV))],
        out_specs=[st, wide],
        out_shape=[jax.ShapeDtypeStruct(s0.shape, F32), jax.ShapeDtypeStruct((n_b, 1, GDN_VW), BF16)],
        compiler_params=_cparams("parallel"),
        name="gdn_sample_state",
    )(s0, q_t, k_t, v.reshape(n_b, 1, GDN_VW), gb.reshape(n_b, 1, LANES), gate.reshape(n_b, 1, GDN_VW),
      head_norm_g.reshape(1, GDN_DV))


def _gdn_sample_layer(x, norm_g, s0, conv_buf, w_in, conv_w, a_log, dt_bias, head_norm_g, w_out):
    n_b = x.shape[0]
    z = _norm_matmul_small(x, norm_g, _gdn_w_in_padded(w_in), 3)
    bufs = [conv_buf[:, j, :] for j in range(GDN_CONV - 1)]
    q, k, v, gb = _gdn_sample_pre(z, bufs, conv_w, _lane_row(a_log), _lane_row(dt_bias))
    to_cols = lambda t: t.reshape(n_b, GDN_HEADS, GDN_DK).transpose(0, 2, 1)
    gate = z[:, GDN_CONV_CH:GDN_CONV_CH + GDN_VW]
    s_new, o = _gdn_sample_state(s0, to_cols(q), to_cols(k), v, gb, gate, head_norm_g)
    y = _matmul_residual(o.reshape(n_b, GDN_VW), w_out.astype(BF16), x, n_b)
    new_buf = jnp.concatenate([conv_buf[:, 1:, :], z[:, None, :GDN_CONV_CH]], axis=1)
    return y, s_new, new_buf


def _lru_sample_kernel(z_ref, b0_ref, b1_ref, b2_ref, cw_ref, cb_ref, wga_ref, bga_ref, wgx_ref, bgx_ref, lam_ref,
                       h0_ref, h_ref, y_ref):
    for c in range(D_RNN // _LRU_CW):
        c0 = c * _LRU_CW
        cols = slice(c0, c0 + _LRU_CW)
        w = cw_ref[:, cols]
        xc = (b0_ref[:, cols] * w[0:1] + b1_ref[:, cols] * w[1:2] + b2_ref[:, cols] * w[2:3]
              + z_ref[:, D_RNN + c0:D_RNN + c0 + _LRU_CW] * w[3:4] + cb_ref[:, cols])
        a, b = _lru_gates(xc, wga_ref, bga_ref[:, cols], wgx_ref, bgx_ref[:, cols], lam_ref[:, cols], c0 // LRU_BW)
        h = a * h0_ref[:, cols] + b
        h_ref[:, cols] = h
        y_ref[:, cols] = (h * jax.nn.gelu(z_ref[:, cols])).astype(BF16)


def _lru_sample_layer(x, norm_g, h0, conv_buf, w_in, conv_w, conv_b, w_ga, b_ga, w_gx, b_gx, lam, w_out):
    n_b = x.shape[0]
    z = _norm_matmul_small(x, norm_g, w_in.astype(BF16), 2)
    r1 = lambda v: v.reshape(1, D_RNN)
    bufs = [conv_buf[:, j, :] for j in range(LRU_CONV - 1)]
    h, y = pl.pallas_call(
        _lru_sample_kernel,
        out_shape=[jax.ShapeDtypeStruct((n_b, D_RNN), F32), jax.ShapeDtypeStruct((n_b, D_RNN), BF16)],
        compiler_params=pltpu.CompilerParams(vmem_limit_bytes=VMEM_LIMIT),
        name="lru_sample",
    )(z, *bufs, conv_w, r1(conv_b), w_ga.astype(BF16), r1(b_ga), w_gx.astype(BF16), r1(b_gx), r1(lam), h0)
    y = _matmul_residual(y, w_out.astype(BF16), x, n_b)
    new_buf = jnp.concatenate([conv_buf[:, 1:, :], z[:, None, D_RNN:]], axis=1)
    return y, h, new_buf


def _ffn_sample_kernel(x_ref, g_ref, wv_ref, wg_ref, cwv_ref, cwg_ref, b0v_ref, b0g_ref, b1v_ref, b1g_ref, wout_ref,
                       y_ref, uv_ref, ug_ref, xn_scr):
    c = pl.program_id(0)

    @pl.when(c == 0)
    def _():
        x = x_ref[...]
        xn_scr[...] = _rmsnorm(x, g_ref[...]).astype(BF16)
        y_ref[...] = x

    xn = xn_scr[...]
    uv = _dot(xn, wv_ref[...])
    ug = _dot(xn, wg_ref[...])
    uv_ref[...] = uv
    ug_ref[...] = ug
    cwv, cwg = cwv_ref[...], cwg_ref[...]
    val = b0v_ref[...] * cwv[0:1] + b1v_ref[...] * cwv[1:2] + uv * cwv[2:3]
    gt = b0g_ref[...] * cwg[0:1] + b1g_ref[...] * cwg[1:2] + ug * cwg[2:3]
    y_ref[...] += _dot((val * jax.nn.silu(gt)).astype(BF16), wout_ref[...])


def _ffn_sample(x, norm_g, w_in, conv_w, w_out, buf):
    n_b = x.shape[0]
    n_c = D_FF // _FFN_CW
    b0, b1 = buf[:, 0, :], buf[:, 1, :]
    val = lambda shape0: pl.BlockSpec((shape0, _FFN_CW), lambda c: (0, c))
    gat = lambda shape0: pl.BlockSpec((shape0, _FFN_CW), lambda c: (0, n_c + c))
    y, uv, ug = pl.pallas_call(
        _ffn_sample_kernel,
        grid=(n_c,),
        in_specs=[_resident((n_b, D_MODEL)), _resident((1, D_MODEL)), val(D_MODEL), gat(D_MODEL),
                  val(FFN_CONV), gat(FFN_CONV), val(n_b), gat(n_b), val(n_b), gat(n_b),
                  pl.BlockSpec((_FFN_CW, D_MODEL), lambda c: (c, 0))],
        out_specs=[pl.BlockSpec((n_b, D_MODEL), lambda c: (0, 0)), val(n_b), val(n_b)],
        out_shape=[jax.ShapeDtypeStruct((n_b, D_MODEL), F32), jax.ShapeDtypeStruct((n_b, D_FF), F32),
                   jax.ShapeDtypeStruct((n_b, D_FF), F32)],
        scratch_shapes=[pltpu.VMEM((n_b, D_MODEL), BF16)],
        compiler_params=_cparams("arbitrary"),
        name="ffn_sample",
    )(x, norm_g.reshape(1, D_MODEL), w_in, w_in, conv_w, conv_w, b0, b0, b1, b1, w_out)
    new_buf = jnp.stack([b1, jnp.concatenate([uv, ug], axis=-1)], axis=1)
    return y, new_buf


def kernel(x_prompt, x_sample, cache_nsa_cmp, cache_nsa_sel, state_nsa_win, state_gdn_S, state_gdn_conv,
           state_lru_h, state_lru_conv, state_ffn_conv, page_table,
           norm_mix_g, norm_ffn_g, norm_final_g,
           nsa_w_in, nsa_cmp_pe, nsa_cmp_w, nsa_w_out,
           gdn_w_in, gdn_conv_w, gdn_A_log, gdn_dt_bias, gdn_norm_g, gdn_w_out,
           lru_w_in, lru_conv_w, lru_conv_b, lru_w_ga, lru_b_ga, lru_w_gx, lru_b_gx, lru_lambda, lru_w_out,
           ffn_w_in, ffn_conv_w, ffn_w_out):
    n_p, t_len, _ = x_prompt.shape
    n_s = x_sample.shape[0]
    past = page_table.shape[1] * PAGE_SIZE
    wl = min(WINDOW, t_len)
    xp = x_prompt.reshape(n_p * t_len, D_MODEL)
    xs = x_sample.reshape(n_s, D_MODEL)
    tab_p = _rope_tables(jnp.arange(t_len, dtype=I32))
    tab_s = _rope_tables(jnp.full((n_s,), past, I32))
    cache_cmp_t = _rows_last(cache_nsa_cmp)
    cache_sel_t = _rows_last(cache_nsa_sel)
    win_t = _rows_last(state_nsa_win)
    ffn_w_in16 = ffn_w_in.astype(BF16)
    ffn_w_out16 = ffn_w_out.astype(BF16)
    zeros = lambda *shape: jnp.zeros(shape, F32)
    kv5 = lambda rows, nb: rows.reshape(nb, -1, 2, NSA_KV, NSA_DH)

    p_nsa, s_nsa, p_gdn, s_gdn, p_lru, s_lru, p_ffn, s_ffn = [], [], [], [], [], [], [], []
    for li in range(DEPTH):
        j = li // N_MIXERS
        kind = li % N_MIXERS
        if kind == 0:
            xp, c, s, w = _nsa_prompt_layer(xp, n_p, norm_mix_g[li], nsa_w_in[j], nsa_cmp_pe[j], nsa_cmp_w[j],
                                            nsa_w_out[j], tab_p)
            p_nsa.append((kv5(c, n_p), kv5(s, n_p), kv5(w, n_p)[:, t_len - wl:]))
            xs, c, s, w = _nsa_sample_layer(xs, j, norm_mix_g[li], nsa_w_in[j], nsa_cmp_pe[j], nsa_cmp_w[j],
                                            nsa_w_out[j], cache_cmp_t, cache_sel_t, win_t[j], page_table, tab_s)
            s_nsa.append((kv5(c, n_s), kv5(s, n_s), jnp.transpose(w, (0, 4, 1, 2, 3))))
        elif kind == 1:
            args = (gdn_w_in[j], gdn_conv_w[j], gdn_A_log[j], gdn_dt_bias[j], gdn_norm_g[j], gdn_w_out[j])
            xp, s_new, buf = _gdn_prompt_layer(xp, n_p, norm_mix_g[li], zeros(n_p, GDN_HEADS, GDN_DK, GDN_DV),
                                               zeros(n_p, GDN_CONV - 1, GDN_CONV_CH), *args)
            p_gdn.append((s_new, buf))
            xs, s_new, buf = _gdn_sample_layer(xs, norm_mix_g[li], state_gdn_S[j], state_gdn_conv[j], *args)
            s_gdn.append((s_new, buf))
        else:
            args = (lru_w_in[j], lru_conv_w[j], lru_conv_b[j], lru_w_ga[j], lru_b_ga[j], lru_w_gx[j], lru_b_gx[j],
                    lru_lambda[j], lru_w_out[j])
            xp, h, buf = _lru_prompt_layer(xp, n_p, norm_mix_g[li], zeros(n_p, D_RNN), zeros(n_p, LRU_CONV - 1, D_RNN),
                                           *args)
            p_lru.append((h, buf))
            xs, h, buf = _lru_sample_layer(xs, norm_mix_g[li], state_lru_h[j], state_lru_conv[j], *args)
            s_lru.append((h, buf))
        xp, buf = _ffn_prompt(xp, n_p, norm_ffn_g[li], ffn_w_in16[li], ffn_conv_w[li], ffn_w_out16[li],
                              zeros(n_p, FFN_CONV - 1, 2 * D_FF))
        p_ffn.append(buf)
        xs, buf = _ffn_sample(xs, norm_ffn_g[li], ffn_w_in16[li], ffn_conv_w[li], ffn_w_out16[li], state_ffn_conv[li])
        s_ffn.append(buf)

    y_prompt = _final_norm(xp, norm_final_g, 512).reshape(n_p, t_len, D_MODEL)
    y_sample = _final_norm(xs, norm_final_g, n_s).reshape(n_s, 1, D_MODEL)
    stack = lambda entries, k: jnp.stack([e[k] for e in entries])
    return (y_prompt, y_sample,
            stack(p_nsa, 0), stack(s_nsa, 0), stack(p_nsa, 1), stack(s_nsa, 1), stack(p_nsa, 2), stack(s_nsa, 2),
            stack(p_gdn, 0), stack(s_gdn, 0), stack(p_gdn, 1), stack(s_gdn, 1),
            stack(p_lru, 0), stack(s_lru, 0), stack(p_lru, 1), stack(s_lru, 1),
            jnp.stack(p_ffn), jnp.stack(s_ffn))
```

```python
import functools

import jax
import jax.numpy as jnp
from jax import lax
from jax.experimental import pallas as pl
from jax.experimental.pallas import tpu as pltpu

F32 = jnp.float32
BF16 = jnp.bfloat16
I32 = jnp.int32

D_MODEL = 1024
DEPTH = 4
PAST_LEN = 8192
PAGE_SIZE = 128
N_MIXERS = 3

NSA_DH = 64
NSA_HEADS = 16
NSA_HPG = 4
NSA_KV = 4
NSA_Q = NSA_HEADS * NSA_DH
NSA_KVW = NSA_KV * NSA_DH
ROT_DIM = NSA_DH // 4
ROPE_THETA = 500000.0
L_CMP = 32
L_SEL = 64
N_SEL = 16
WINDOW = 512
_LOG_SEL = 6
NSA_SCALE = NSA_DH ** -0.5

GDN_DK = 128
GDN_DV = 128
GDN_HEADS = 8
GDN_QK = GDN_HEADS * GDN_DK
GDN_VW = GDN_HEADS * GDN_DV
GDN_CONV = 4
GDN_CONV_CH = 2 * GDN_QK + GDN_VW
GDN_CHUNK = 64

D_RNN = D_MODEL
LRU_BLOCKS = 8
LRU_BW = D_RNN // LRU_BLOCKS
LRU_CONV = 4
LRU_C = 8.0

D_FF = 2816
FFN_CONV = 3

EPS = 1e-6
NEG = -1e30
FORCE = 1e4

LANES = 128
SUBLANES = 8
MXU_DIM = 256
VMEM_LIMIT = 56 * 1024 * 1024


def _cparams(*sem):
    return pltpu.CompilerParams(dimension_semantics=sem, vmem_limit_bytes=VMEM_LIMIT)


def _resident(shape):
    nd = len(shape)
    return pl.BlockSpec(shape, lambda *_: (0,) * nd, pipeline_mode=pl.Buffered(1))


def _rmsnorm(x, g):
    return x * lax.rsqrt(jnp.mean(x * x, axis=-1, keepdims=True) + EPS) * g


def _dot(a, b):
    return jnp.dot(a, b, preferred_element_type=F32)


def _dot_nt(a, b):
    return lax.dot_general(a, b, (((1,), (1,)), ((), ())), preferred_element_type=F32)


def _dot_tn(a, b):
    return lax.dot_general(a, b, (((0,), (0,)), ((), ())), preferred_element_type=F32)


def _dot_f32(a, b):
    return jnp.dot(a, b, precision=lax.Precision.HIGHEST, preferred_element_type=F32)


def _iota(shape, axis):
    return lax.broadcasted_iota(I32, shape, axis)


def _rope_tables(pos):
    half = ROT_DIM // 2
    inv = ROPE_THETA ** (-jnp.arange(half, dtype=F32) / half)
    ang = pos.astype(F32)[:, None] * inv[None, :]
    lane = jnp.arange(LANES) % NSA_DH
    c = jnp.cos(ang)[:, lane % half]
    s = jnp.sin(ang)[:, lane % half]
    cos = jnp.where(lane < ROT_DIM, c, 1.0)
    sin_a = jnp.where((lane >= half) & (lane < ROT_DIM), s, 0.0)
    sin_b = jnp.where(lane < half, -s, 0.0)
    return cos.astype(F32), sin_a.astype(F32), sin_b.astype(F32)


def _rope(x, cos, sin_a, sin_b):
    half = ROT_DIM // 2
    outs = []
    for c in range(x.shape[1] // LANES):
        xc = x[:, c * LANES:(c + 1) * LANES]
        outs.append(xc * cos + pltpu.roll(xc, half, 1) * sin_a + pltpu.roll(xc, LANES - half, 1) * sin_b)
    return outs[0] if len(outs) == 1 else jnp.concatenate(outs, axis=1)


_NSA_W_COLS = NSA_Q + 6 * NSA_KVW + LANES


def _nsa_proj_kernel(x_ref, g_ref, w_ref, cos_ref, sa_ref, sb_ref,
                     q_ref, qr_ref, cmp_ref, sel_ref, win_ref, gate_ref, kvb_ref):
    xn = _rmsnorm(x_ref[...], g_ref[...]).astype(BF16)
    cos, sa, sb = cos_ref[...], sa_ref[...], sb_ref[...]

    def mm(c0, c1):
        return _dot(xn, w_ref[:, c0:c1])

    zq = mm(0, NSA_Q)
    q_ref[...] = (zq * NSA_SCALE).astype(BF16)
    qr_ref[...] = (_rope(zq, cos, sa, sb) * NSA_SCALE).astype(BF16)
    c0 = NSA_Q
    cmp_ref[...] = mm(c0, c0 + 2 * NSA_KVW)
    c0 += 2 * NSA_KVW
    for kind, o_ref in ((0, sel_ref), (1, win_ref)):
        z = mm(c0, c0 + 2 * NSA_KVW)
        kr = _rope(z[:, :NSA_KVW], cos, sa, sb)
        v = z[:, NSA_KVW:]
        o_ref[:, :NSA_KVW] = kr
        o_ref[:, NSA_KVW:] = v
        for g in range(NSA_KV):
            kvb_ref[2 * kind, g] = kr[:, g * NSA_DH:(g + 1) * NSA_DH].astype(BF16)
            kvb_ref[2 * kind + 1, g] = v[:, g * NSA_DH:(g + 1) * NSA_DH].astype(BF16)
        c0 += 2 * NSA_KVW
    gate_ref[...] = jax.nn.sigmoid(mm(c0, c0 + LANES))


def _nsa_proj(x, g, w, tables, n_batch, tm):
    n = x.shape[0]
    t_len = n // n_batch
    tpb = t_len // tm
    row = lambda i: (i, 0)
    tab = pl.BlockSpec((tm, LANES), lambda i: (i % tpb, 0))
    return pl.pallas_call(
        _nsa_proj_kernel,
        grid=(n // tm,),
        in_specs=[pl.BlockSpec((tm, D_MODEL), row), _resident((1, D_MODEL)), _resident((D_MODEL, _NSA_W_COLS)),
                  tab, tab, tab],
        out_specs=[pl.BlockSpec((tm, NSA_Q), row), pl.BlockSpec((tm, NSA_Q), row),
                   pl.BlockSpec((tm, 2 * NSA_KVW), row), pl.BlockSpec((tm, 2 * NSA_KVW), row),
                   pl.BlockSpec((tm, 2 * NSA_KVW), row), pl.BlockSpec((tm, LANES), row),
                   pl.BlockSpec((None, 4, NSA_KV, tm, NSA_DH), lambda i: (i // tpb, 0, 0, i % tpb, 0))],
        out_shape=[jax.ShapeDtypeStruct((n, NSA_Q), BF16), jax.ShapeDtypeStruct((n, NSA_Q), BF16),
                   jax.ShapeDtypeStruct((n, 2 * NSA_KVW), F32), jax.ShapeDtypeStruct((n, 2 * NSA_KVW), F32),
                   jax.ShapeDtypeStruct((n, 2 * NSA_KVW), F32), jax.ShapeDtypeStruct((n, LANES), F32),
                   jax.ShapeDtypeStruct((n_batch, 4, NSA_KV, t_len, NSA_DH), BF16)],
        compiler_params=_cparams("parallel"),
        name="nsa_proj",
    )(x, g, w, *tables)


def _nsa_w_in_padded(w_in):
    pad = _NSA_W_COLS - w_in.shape[1]
    return jnp.pad(w_in, ((0, 0), (0, pad))).astype(BF16)


def _cmp_weights(cmp_w, cmp_pe):
    eye = jnp.eye(NSA_KV, dtype=cmp_w.dtype)
    bd = jnp.einsum("gh,kldx->klgdhx", eye, cmp_w).reshape(2, L_CMP, NSA_KVW, NSA_KVW)
    return bd.astype(BF16), jnp.tile(cmp_pe, (1, 1, NSA_KV))


def _compress_planes(load_rows, bd_ref, pe_ref, o_ref, n_rows_out):
    for kv in range(2):
        acc = jnp.zeros((n_rows_out, NSA_KVW), F32)
        bias = jnp.zeros((SUBLANES, NSA_KVW), F32)
        for l in range(L_CMP):
            w = bd_ref[kv, l]
            acc = acc + _dot(load_rows(kv, l).astype(BF16), w)
            pe = jnp.broadcast_to(pe_ref[kv, l:l + 1, :], (SUBLANES, NSA_KVW)).astype(BF16)
            bias = bias + _dot(pe, w)
        out = acc + bias[0:1, :]
        for g in range(NSA_KV):
            o_ref[kv, g] = out[:, g * NSA_DH:(g + 1) * NSA_DH]


def _nsa_compress_prompt_kernel(x0_ref, x1_ref, x2_ref, x3_ref, bd_ref, pe_ref, o_ref, *, n_blk, n_pad):
    planes = (x0_ref, x1_ref, x2_ref, x3_ref)
    if n_pad > n_blk:
        o_ref[...] = jnp.zeros(o_ref.shape, F32)

    def load_rows(kv, l):
        return jnp.concatenate([planes[2 * kv + i][pl.ds(l, n_blk, stride=L_CMP), :] for i in range(2)], axis=1)

    if n_pad == n_blk:
        _compress_planes(load_rows, bd_ref, pe_ref, o_ref, n_blk)
    else:
        _compress_planes(load_rows, bd_ref, pe_ref, o_ref.at[:, :, 0:n_blk, :], n_blk)


def _nsa_compress_prompt(cmp_rows, bd, pe2, n_batch):
    n = cmp_rows.shape[0]
    t_len = n // n_batch
    n_blk = t_len // L_CMP
    n_pad = max(n_blk, LANES)
    planes = [pl.BlockSpec((t_len, LANES), functools.partial(lambda b, c: (b, c), c=c)) for c in range(4)]
    return pl.pallas_call(
        functools.partial(_nsa_compress_prompt_kernel, n_blk=n_blk, n_pad=n_pad),
        grid=(n_batch,),
        in_specs=planes + [_resident(bd.shape), _resident(pe2.shape)],
        out_specs=pl.BlockSpec((None, 2, NSA_KV, n_pad, NSA_DH), lambda b: (b, 0, 0, 0, 0)),
        out_shape=jax.ShapeDtypeStruct((n_batch, 2, NSA_KV, n_pad, NSA_DH), F32),
        compiler_params=_cparams("parallel"),
        name="nsa_compress_prompt",
    )(cmp_rows, cmp_rows, cmp_rows, cmp_rows, bd, pe2)


def _stack_heads(x):
    return jnp.concatenate([x[:, j * NSA_DH:(j + 1) * NSA_DH] for j in range(NSA_HPG)], axis=0)


def _softmax_rows(s):
    m = jnp.max(s, axis=-1, keepdims=True)
    e = jnp.exp(s - m)
    return e / jnp.sum(e, axis=-1, keepdims=True)


def _select_blocks(imp_t, t0, rank_scr):
    n_sel_pad = LANES // 2
    tq = imp_t.shape[1]
    rank_scr[...] = imp_t
    pair = rank_scr[pl.ds(0, n_sel_pad, stride=2), :] + rank_scr[pl.ds(1, n_sel_pad, stride=2), :]
    blk = _iota((n_sel_pad, tq), 0)
    tpos = t0 + _iota((n_sel_pad, tq), 1)
    forced = (blk == 0) | (blk == (tpos >> _LOG_SEL))
    avail = blk * L_SEL <= tpos
    val = jnp.where(forced, FORCE, jnp.where(avail, pair, -1.0))
    rank_scr[0:n_sel_pad, :] = val
    cnt = jnp.zeros((n_sel_pad, tq), F32)
    for s in range(n_sel_pad):
        row = jnp.broadcast_to(rank_scr[pl.ds(s, 1), :], (n_sel_pad, tq))
        ahead = (row > val) | ((row == val) & (blk > s))
        cnt = cnt + jnp.where(ahead, 1.0, 0.0)
    sel = jnp.where(cnt < float(N_SEL), 1.0, 0.0)
    return jnp.concatenate([sel, jnp.zeros((LANES - n_sel_pad, tq), F32)], axis=0)


_ATT_RB = 128


def _softmax_blocks(sc_scr, bias_scr, p_scr, width, rows, tq, post=None):
    for r0 in range(0, rows, _ATT_RB):
        rq = r0 % tq
        s = sc_scr[r0:r0 + _ATT_RB, 0:width] + bias_scr[rq:rq + _ATT_RB, 0:width]
        e = jnp.exp(s - jnp.max(s, axis=-1, keepdims=True))
        p = e / jnp.sum(e, axis=-1, keepdims=True)
        if post is not None:
            p = post(p, r0)
        p_scr[r0:r0 + _ATT_RB, 0:width] = p.astype(BF16)


def _nsa_attn_kernel(q_ref, qr_ref, gate_ref, kvc_ref, kvb_ref, o_ref,
                     rank_scr, sc_scr, bias_scr, p_scr, m_scr, l_scr, acc_scr, imp_scr, *, tq, tk):
    g = pl.program_id(1)
    t0 = pl.program_id(2) * tq
    rows = NSA_HPG * tq
    q4 = _stack_heads(q_ref[...])
    qr4 = _stack_heads(qr_ref[...])

    n_cmp = kvc_ref.shape[1]
    sc_scr[:, 0:n_cmp] = _dot_nt(q4, kvc_ref[0].astype(BF16))
    tpos_c = t0 + _iota((tq, n_cmp), 0)
    c_end = (_iota((tq, n_cmp), 1) + 1) * L_CMP - 1
    bias_scr[:, 0:n_cmp] = jnp.where(c_end <= tpos_c, 0.0, NEG)
    imp_scr[...] = jnp.zeros(imp_scr.shape, F32)

    def cmp_post(p, r0):
        rq = r0 % tq
        p = p * jnp.where(t0 + rq + _iota((_ATT_RB, 1), 0) >= L_CMP - 1, 1.0, 0.0)
        imp_scr[rq:rq + _ATT_RB, :] += p
        return p

    _softmax_blocks(sc_scr, bias_scr, p_scr, n_cmp, rows, tq, post=cmp_post)
    o_c = _dot(p_scr[:, 0:n_cmp], kvc_ref[1].astype(BF16))

    sel = jnp.concatenate(
        [_select_blocks(imp_scr[h0:h0 + LANES, :].T, t0 + h0, rank_scr).T for h0 in range(0, tq, LANES)], axis=0)
    sel = jnp.where(sel > 0.5, 1.0, 0.0).astype(BF16)

    m_scr[...] = jnp.full(m_scr.shape, NEG, F32)
    l_scr[...] = jnp.zeros(l_scr.shape, F32)
    acc_scr[...] = jnp.zeros(acc_scr.shape, F32)

    def sel_tile(kt, carry):
        k0 = pl.multiple_of(kt * tk, tk)
        sc_scr[:, 0:tk] = _dot_nt(qr4, kvb_ref[0, pl.ds(k0, tk), :])
        expand = jnp.where(_iota((LANES, tk), 0) == ((k0 + _iota((LANES, tk), 1)) >> _LOG_SEL), 1.0, 0.0).astype(BF16)
        picked = _dot(sel, expand)
        live = (picked > 0.5) & (k0 + _iota((tq, tk), 1) <= t0 + _iota((tq, tk), 0))
        bias_scr[:, 0:tk] = jnp.where(live, 0.0, NEG)
        for r0 in range(0, rows, _ATT_RB):
            rq = r0 % tq
            rs = slice(r0, r0 + _ATT_RB)
            s = sc_scr[rs, 0:tk] + bias_scr[rq:rq + _ATT_RB, 0:tk]
            m_old = m_scr[rs, :]
            m_new = jnp.maximum(m_old, jnp.max(s, axis=-1, keepdims=True))
            alpha = jnp.exp(m_old - m_new)
            pe = jnp.exp(s - m_new)
            l_scr[rs, :] = alpha * l_scr[rs, :] + jnp.sum(pe, axis=-1, keepdims=True)
            m_scr[rs, :] = m_new
            acc_scr[rs, :] = alpha * acc_scr[rs, :]
            p_scr[rs, 0:tk] = pe.astype(BF16)
        acc_scr[...] += _dot(p_scr[:, 0:tk], kvb_ref[1, pl.ds(k0, tk), :])
        return carry

    n_kt = (t0 + tq + tk - 1) // tk
    lax.fori_loop(0, n_kt, sel_tile, 0)
    o_s = acc_scr[...] / l_scr[...]

    wk = WINDOW + tq
    w0 = pl.multiple_of(jnp.maximum(t0 - WINDOW, 0), tq)
    sc_scr[:, 0:wk] = _dot_nt(qr4, kvb_ref[2, pl.ds(w0, wk), :])
    dist = (t0 + _iota((tq, wk), 0)) - (w0 + _iota((tq, wk), 1))
    bias_scr[:, 0:wk] = jnp.where((dist >= 0) & (dist <= WINDOW), 0.0, NEG)
    _softmax_blocks(sc_scr, bias_scr, p_scr, wk, rows, tq)
    o_w = _dot(p_scr[:, 0:wk], kvb_ref[3, pl.ds(w0, wk), :])

    gates = gate_ref[...]
    lane = _iota((tq, LANES), 1)
    outs = []
    for j in range(NSA_HPG):
        o = jnp.zeros((tq, NSA_DH), F32)
        for br, ob in enumerate((o_c, o_s, o_w)):
            col = br * NSA_HEADS + g * NSA_HPG + j
            gcol = jnp.sum(jnp.where(lane == col, gates, 0.0), axis=-1, keepdims=True)
            o = o + gcol * ob[j * tq:(j + 1) * tq]
        outs.append(o)
    o_ref[...] = jnp.concatenate(outs, axis=1).astype(BF16)


def _nsa_attn_prompt(q, qr, gates, kvc, kvb, n_batch, tq=256, tk=1024):
    n = q.shape[0]
    t_len = n // n_batch
    nq = t_len // tq
    tk = min(tk, t_len)
    assert kvc.shape[3] == LANES and t_len >= WINDOW + tq and t_len % tk == 0
    rows = NSA_HPG * tq
    width = max(tk, WINDOW + tq, LANES)
    qspec =pl.BlockSpec((tq, NSA_HPG * NSA_DH), lambda b, g, i: (b * nq + i, g))
    return pl.pallas_call(
        functools.partial(_nsa_attn_kernel, tq=tq, tk=tk),
        grid=(n_batch, NSA_KV, nq),
        in_specs=[qspec, qspec,
                  pl.BlockSpec((tq, LANES), lambda b, g, i: (b * nq + i, 0)),
                  pl.BlockSpec((None, 2, None, LANES, NSA_DH), lambda b, g, i: (b, 0, g, 0, 0)),
                  pl.BlockSpec((None, 4, None, t_len, NSA_DH), lambda b, g, i: (b, 0, g, 0, 0))],
        out_specs=qspec,
        out_shape=jax.ShapeDtypeStruct((n, NSA_Q), BF16),
        scratch_shapes=[pltpu.VMEM((LANES, LANES), F32),
                        pltpu.VMEM((rows, width), F32),
                        pltpu.VMEM((tq, width), F32),
                        pltpu.VMEM((rows, width), BF16),
                        pltpu.VMEM((rows, 1), F32), pltpu.VMEM((rows, 1), F32),
                        pltpu.VMEM((rows, NSA_DH), F32),
                        pltpu.VMEM((tq, LANES), F32)],
        compiler_params=_cparams("parallel", "parallel", "arbitrary"),
        name="nsa_attn_prompt",
    )(q, qr, gates, kvc, kvb)


def _matmul_residual_kernel(a_ref, w_ref, x_ref, o_ref):
    o_ref[...] = x_ref[...] + _dot(a_ref[...].astype(BF16), w_ref[...])


def _matmul_residual(a, w, x, tm):
    n, k = a.shape
    row = lambda i: (i, 0)
    return pl.pallas_call(
        _matmul_residual_kernel,
        grid=(n // tm,),
        in_specs=[pl.BlockSpec((tm, k), row), _resident(w.shape), pl.BlockSpec((tm, D_MODEL), row)],
        out_specs=pl.BlockSpec((tm, D_MODEL), row),
        out_shape=jax.ShapeDtypeStruct((n, D_MODEL), F32),
        compiler_params=_cparams("parallel"),
        name="matmul_residual",
    )(a, w, x)


def _nsa_prompt_layer(x, n_batch, norm_g, w_in, cmp_pe, cmp_w, w_out, tables):
    tm = min(512, x.shape[0] // n_batch)
    q, qr, cmp_rows, sel_rows, win_rows, gates, kvb = _nsa_proj(
        x, norm_g.reshape(1, D_MODEL), _nsa_w_in_padded(w_in), tables, n_batch, tm)
    bd, pe2 = _cmp_weights(cmp_w, cmp_pe)
    kvc = _nsa_compress_prompt(cmp_rows, bd, pe2, n_batch)
    o = _nsa_attn_prompt(q, qr, gates, kvc, kvb, n_batch)
    y = _matmul_residual(o, w_out.astype(BF16), x, tm)
    return y, cmp_rows, sel_rows, win_rows


_FFN_CW = MXU_DIM


def _ffn_kernel(x_ref, g_ref, win_ref, cw_ref, wout_ref, buf_ref, y_ref, st_ref, carry_scr, uv_scr, ug_scr,
                *, tm, tpb):
    ti = pl.program_id(0) % tpb
    x = x_ref[...]
    xn = _rmsnorm(x, g_ref[...]).astype(BF16)

    @pl.when(ti == 0)
    def _():
        carry_scr[SUBLANES - (FFN_CONV - 1):SUBLANES, :] = buf_ref[...]

    acc = jnp.zeros((tm, D_MODEL), F32)
    for c in range(D_FF // _FFN_CW):
        conv = []
        for part, scr in ((0, uv_scr), (1, ug_scr)):
            c0 = part * D_FF + c * _FFN_CW
            u = _dot(xn, win_ref[:, c0:c0 + _FFN_CW])
            scr[0:SUBLANES, :] = carry_scr[:, c0:c0 + _FFN_CW]
            scr[SUBLANES:SUBLANES + tm, :] = u
            carry_scr[:, c0:c0 + _FFN_CW] = u[tm - SUBLANES:tm, :]
            w = cw_ref[:, c0:c0 + _FFN_CW]
            conv.append(scr[SUBLANES - 2:SUBLANES - 2 + tm, :] * w[0:1] + scr[SUBLANES - 1:SUBLANES - 1 + tm, :] * w[1:2]
                        + u * w[2:3])
        act = conv[0] * jax.nn.silu(conv[1])
        acc = acc + _dot(act.astype(BF16), wout_ref[c * _FFN_CW:(c + 1) * _FFN_CW, :])
    y_ref[...] = x + acc

    @pl.when(ti == tpb - 1)
    def _():
        st_ref[...] = carry_scr[SUBLANES - (FFN_CONV - 1):SUBLANES, :]


def _ffn_prompt(x, n_batch, norm_g, w_in, conv_w, w_out, buf, tm=512):
    n = x.shape[0]
    tpb = n // n_batch // tm
    row = lambda i: (i, 0)
    st = pl.BlockSpec((None, FFN_CONV - 1, 2 * D_FF), lambda i: (i // tpb, 0, 0))
    return pl.pallas_call(
        functools.partial(_ffn_kernel, tm=tm, tpb=tpb),
        grid=(n // tm,),
        in_specs=[pl.BlockSpec((tm, D_MODEL), row), _resident((1, D_MODEL)), _resident(w_in.shape),
                  _resident(conv_w.shape), _resident(w_out.shape), st],
        out_specs=[pl.BlockSpec((tm, D_MODEL), row), st],
        out_shape=[jax.ShapeDtypeStruct((n, D_MODEL), F32),
                   jax.ShapeDtypeStruct((n_batch, FFN_CONV - 1, 2 * D_FF), F32)],
        scratch_shapes=[pltpu.VMEM((SUBLANES, 2 * D_FF), F32), pltpu.VMEM((SUBLANES + tm, _FFN_CW), F32),
                        pltpu.VMEM((SUBLANES + tm, _FFN_CW), F32)],
        compiler_params=_cparams("arbitrary"),
        name="ffn_prompt",
    )(x, norm_g.reshape(1, D_MODEL), w_in, conv_w, w_out, buf)


_GDN_W_COLS = GDN_CONV_CH + GDN_VW + LANES
_GDN_CW = MXU_DIM


def _softplus(x):
    return jnp.maximum(x, 0.0) + jnp.log1p(jnp.exp(-jnp.abs(x)))


def _gdn_w_in_padded(w_in):
    return jnp.pad(w_in, ((0, 0), (0, _GDN_W_COLS - w_in.shape[1]))).astype(BF16)


def _lane_row(v):
    return jnp.pad(v.astype(F32), (0, LANES - v.shape[0])).reshape(1, LANES)


def _gdn_post_conv(y, c0):
    y = jax.nn.silu(y)
    kind = c0 // GDN_QK
    if kind == 2:
        return y
    outs = []
    for hh in range(_GDN_CW // GDN_DK):
        seg = y[:, hh * GDN_DK:(hh + 1) * GDN_DK]
        seg = seg * lax.rsqrt(jnp.sum(seg * seg, axis=-1, keepdims=True) + EPS)
        outs.append(seg * GDN_DK ** -0.5 if kind == 0 else seg)
    return jnp.concatenate(outs, axis=1)


def _gdn_gates(ab, alog, dtb):
    lane = _iota(ab.shape, 1)
    gval = -jnp.exp(alog) * _softplus(ab + dtb)
    return jnp.where(lane < GDN_HEADS, gval, jax.nn.sigmoid(ab))


def _gdn_proj_kernel(x_ref, g_ref, w_ref, cw_ref, alog_ref, dtb_ref, buf_ref,
                     q_ref, k_ref, v_ref, gate_ref, gb_ref, st_ref, carry_scr, u_scr, *, tm, tpb):
    ti = pl.program_id(0) % tpb
    xn = _rmsnorm(x_ref[...], g_ref[...]).astype(BF16)
    keep = GDN_CONV - 1

    @pl.when(ti == 0)
    def _():
        carry_scr[SUBLANES - keep:SUBLANES, :] = buf_ref[...]

    outs = (q_ref, k_ref, v_ref)
    for c in range(GDN_CONV_CH // _GDN_CW):
        c0 = c * _GDN_CW
        u = _dot(xn, w_ref[:, c0:c0 + _GDN_CW])
        u_scr[0:SUBLANES, :] = carry_scr[:, c0:c0 + _GDN_CW]
        u_scr[SUBLANES:SUBLANES + tm, :] = u
        carry_scr[:, c0:c0 + _GDN_CW] = u[tm - SUBLANES:tm, :]
        w = cw_ref[:, c0:c0 + _GDN_CW]
        y = u_scr[SUBLANES - 3:SUBLANES - 3 + tm, :] * w[0:1]
        y = y + u_scr[SUBLANES - 2:SUBLANES - 2 + tm, :] * w[1:2]
        y = y + u_scr[SUBLANES - 1:SUBLANES - 1 + tm, :] * w[2:3]
        y = y + u * w[3:4]
        o_ref = outs[c0 // GDN_QK]
        o_ref[:, c0 % GDN_QK:c0 % GDN_QK + _GDN_CW] = _gdn_post_conv(y, c0)
    gate_ref[...] = _dot(xn, w_ref[:, GDN_CONV_CH:GDN_CONV_CH + GDN_VW])
    ab = _dot(xn, w_ref[:, GDN_CONV_CH + GDN_VW:_GDN_W_COLS])
    gb_ref[...] = _gdn_gates(ab, alog_ref[...], dtb_ref[...])

    @pl.when(ti == tpb - 1)
    def _():
        st_ref[...] = carry_scr[SUBLANES - keep:SUBLANES, :]


def _gdn_proj(x, n_batch, norm_g, w, conv_w, alog, dtb, buf, tm=256):
    n = x.shape[0]
    tpb = n // n_batch // tm
    row = lambda i: (i, 0)
    wide = pl.BlockSpec((tm, GDN_QK), row)
    st = pl.BlockSpec((None, GDN_CONV - 1, GDN_CONV_CH), lambda i: (i // tpb, 0, 0))
    f = jax.ShapeDtypeStruct((n, GDN_QK), F32)
    return pl.pallas_call(
        functools.partial(_gdn_proj_kernel, tm=tm, tpb=tpb),
        grid=(n // tm,),
        in_specs=[pl.BlockSpec((tm, D_MODEL), row), _resident((1, D_MODEL)), _resident(w.shape),
                  _resident(conv_w.shape), _resident((1, LANES)), _resident((1, LANES)), st],
        out_specs=[wide, wide, wide, wide, pl.BlockSpec((tm, LANES), row), st],
        out_shape=[f, f, f, f, jax.ShapeDtypeStruct((n, LANES), F32),
                   jax.ShapeDtypeStruct((n_batch, GDN_CONV - 1, GDN_CONV_CH), F32)],
        scratch_shapes=[pltpu.VMEM((SUBLANES, GDN_CONV_CH), F32), pltpu.VMEM((SUBLANES + tm, _GDN_CW), F32)],
        compiler_params=_cparams("arbitrary"),
        name="gdn_proj",
    )(x, norm_g.reshape(1, D_MODEL), w, conv_w, alog, dtb, buf)


_GDN_HG = MXU_DIM // GDN_CHUNK


def _gdn_block_diag(x):
    head = _iota(x.shape, 0) // GDN_CHUNK
    return jnp.concatenate([jnp.where(head == h, x, 0.0) for h in range(_GDN_HG)], axis=1)


def _gdn_group_prep(q, k, v, beta, gcum_col, gcum_row, g_last):
    n = q.shape[0]
    ii = _iota((n, n), 0)
    jj = _iota((n, n), 1)
    same = (ii // GDN_CHUNK) == (jj // GDN_CHUNK)
    decay = jnp.exp(jnp.where(same & (ii >= jj), gcum_col - gcum_row, NEG))
    kb = k * beta
    k16 = k.astype(BF16)
    low = _dot_nt(kb.astype(BF16), k16) * decay * jnp.where(ii > jj, 1.0, 0.0)
    attn = _dot_nt(q.astype(BF16), k16) * decay
    x = jnp.where(ii == jj, 1.0, 0.0) - low
    p = low
    for _ in range(GDN_CHUNK.bit_length() - 2):
        p16 = p.astype(BF16)
        p = _dot(p16, p16)
        x = x + _dot(x.astype(BF16), p.astype(BF16))
    rhs = jnp.concatenate([v * beta, kb * jnp.exp(gcum_col)], axis=1)
    sol = _dot(x.astype(BF16), rhs.astype(BF16))
    u_base, w_dec = sol[:, :GDN_DV], sol[:, GDN_DV:]
    return (u_base, _gdn_block_diag(w_dec).astype(BF16), _gdn_block_diag(q * jnp.exp(gcum_col)).astype(BF16),
            attn.astype(BF16), _gdn_block_diag(k * jnp.exp(g_last - gcum_col)).astype(BF16))


def _gdn_group_step(prep, s_prev, s_decay):
    u_base, w_dec_bd, q_bd, attn, k_dec_bd = prep
    s16 = s_prev.astype(BF16)
    u = u_base - _dot(w_dec_bd, s16)
    u16 = u.astype(BF16)
    o = _dot(q_bd, s16) + _dot(attn, u16)
    s_new = s_prev * s_decay + _dot_tn(k_dec_bd, u16)
    return o, s_new


def _gdn_chunk_kernel(q_ref, k_ref, v_ref, gb_ref, gate_ref, x_ref, s0_ref, ng_ref, wout_ref,
                      y_ref, s_ref, s_scr, o_scr, *, rt, tpb):
    ti = pl.program_id(1)
    n_groups = GDN_HEADS // _GDN_HG

    @pl.when(ti == 0)
    def _():
        s_scr[...] = s0_ref[...].reshape(n_groups, _GDN_HG * GDN_DK, GDN_DV)

    c = GDN_CHUNK
    tri = jnp.where(_iota((c, c), 0) >= _iota((c, c), 1), 1.0, 0.0)
    stack_rows = lambda ref, r0, h0: jnp.concatenate(
        [ref[r0:r0 + c, (h0 + h) * GDN_DK:(h0 + h + 1) * GDN_DK] for h in range(_GDN_HG)], axis=0)

    preps, decays = [], []
    for ci in range(rt // c):
        r0 = ci * c
        gb = gb_ref[r0:r0 + c, :]
        gcum = _dot_f32(tri, gb)
        gcum_t = gcum.T
        for gi in range(n_groups):
            h0 = gi * _GDN_HG
            heads = range(h0, h0 + _GDN_HG)
            col = lambda a: jnp.concatenate([a[:, h:h + 1] for h in heads], axis=0)
            g_last = jnp.concatenate([jnp.broadcast_to(gcum[c - 1:c, h:h + 1], (c, 1)) for h in heads], axis=0)
            gcum_row = jnp.concatenate([gcum_t[h:h + 1, :] for h in heads], axis=1)
            beta = jnp.concatenate([gb[:, GDN_HEADS + h:GDN_HEADS + h + 1] for h in heads], axis=0)
            preps.append(_gdn_group_prep(stack_rows(q_ref, r0, h0), stack_rows(k_ref, r0, h0),
                                         stack_rows(v_ref, r0, h0), beta, col(gcum), gcum_row, g_last))
            decays.append(jnp.concatenate(
                [jnp.broadcast_to(jnp.exp(gcum[c - 1:c, h:h + 1]), (GDN_DK, 1)) for h in heads], axis=0))

    for gi in range(n_groups):
        s = s_scr[gi]
        for ci in range(rt // c):
            o, s = _gdn_group_step(preps[ci * n_groups + gi], s, decays[ci * n_groups + gi])
            for h in range(_GDN_HG):
                o_scr[ci * c:(ci + 1) * c, (gi * _GDN_HG + h) * GDN_DV:(gi * _GDN_HG + h + 1) * GDN_DV] = (
                    o[h * c:(h + 1) * c])
        s_scr[gi] = s

    ng = ng_ref[...]
    outs = []
    for h in range(GDN_HEADS):
        cols = slice(h * GDN_DV, (h + 1) * GDN_DV)
        outs.append((_rmsnorm(o_scr[:, cols], ng) * jax.nn.silu(gate_ref[:, cols])).astype(BF16))
    y_ref[...] = x_ref[...] + _dot(jnp.concatenate(outs, axis=1), wout_ref[...])

    @pl.when(ti == tpb - 1)
    def _():
        s_ref[...] = s_scr[...].reshape(GDN_HEADS, GDN_DK, GDN_DV)


def _gdn_chunked(q, k, v, gb, gate, x, s0, norm_g, w_out, n_batch, rt=256):
    n = x.shape[0]
    tpb = n // n_batch // rt
    row = lambda b, i: (b * tpb + i, 0)
    wide = pl.BlockSpec((rt, GDN_QK), row)
    st = pl.BlockSpec((None, GDN_HEADS, GDN_DK, GDN_DV), lambda b, i: (b, 0, 0, 0))
    return pl.pallas_call(
        functools.partial(_gdn_chunk_kernel, rt=rt, tpb=tpb),
        grid=(n_batch, tpb),
        in_specs=[wide, wide, wide, pl.BlockSpec((rt, LANES), row), wide, wide, st,
                  _resident((1, GDN_DV)), _resident(w_out.shape)],
        out_specs=[wide, st],
        out_shape=[jax.ShapeDtypeStruct((n, D_MODEL), F32),
                   jax.ShapeDtypeStruct((n_batch, GDN_HEADS, GDN_DK, GDN_DV), F32)],
        scratch_shapes=[pltpu.VMEM((GDN_HEADS // _GDN_HG, _GDN_HG * GDN_DK, GDN_DV), F32),
                        pltpu.VMEM((rt, GDN_VW), F32)],
        compiler_params=_cparams("parallel", "arbitrary"),
        name="gdn_chunked",
    )(q, k, v, gb, gate, x, s0, norm_g.reshape(1, GDN_DV), w_out)


def _gdn_prompt_layer(x, n_batch, norm_g, s0, conv_buf, w_in, conv_w, a_log, dt_bias, head_norm_g, w_out):
    q, k, v, gate, gb, new_buf = _gdn_proj(x, n_batch, norm_g, _gdn_w_in_padded(w_in), conv_w,
                                           _lane_row(a_log), _lane_row(dt_bias), conv_buf)
    y, s_new = _gdn_chunked(q, k, v, gb, gate, x, s0, head_norm_g, w_out.astype(BF16), n_batch)
    return y, s_new, new_buf


_LRU_CW = MXU_DIM


def _lru_gates(xc, wga_ref, bga, wgx_ref, bgx, lam, n0):
    r_parts, i_parts = [], []
    for j in range(xc.shape[1] // LRU_BW):
        blk = xc[:, j * LRU_BW:(j + 1) * LRU_BW].astype(BF16)
        r_parts.append(_dot(blk, wga_ref[n0 + j]))
        i_parts.append(_dot(blk, wgx_ref[n0 + j]))
    cat = lambda ps: ps[0] if len(ps) == 1 else jnp.concatenate(ps, axis=1)
    r = jax.nn.sigmoid(cat(r_parts) + bga)
    i = jax.nn.sigmoid(cat(i_parts) + bgx)
    log_a = -LRU_C * r * _softplus(-lam)
    a = jnp.exp(log_a)
    b = jnp.sqrt(-jnp.tanh(log_a) * (a * a + 1.0)) * i * xc
    return a, b


def _lru_proj_kernel(x_ref, g_ref, w_ref, cw_ref, cb_ref, wga_ref, bga_ref, wgx_ref, bgx_ref, lam_ref, buf_ref,
                     gate_ref, a_ref, b_ref, st_ref, carry_scr, u_scr, *, tm, tpb):
    ti = pl.program_id(0) % tpb
    xn = _rmsnorm(x_ref[...], g_ref[...]).astype(BF16)
    keep = LRU_CONV - 1

    @pl.when(ti == 0)
    def _():
        carry_scr[SUBLANES - keep:SUBLANES, :] = buf_ref[...]

    gate_ref[...] = _dot(xn, w_ref[:, 0:D_RNN])
    for c in range(D_RNN // _LRU_CW):
        c0 = c * _LRU_CW
        cols = slice(c0, c0 + _LRU_CW)
        u = _dot(xn, w_ref[:, D_RNN + c0:D_RNN + c0 + _LRU_CW])
        u_scr[0:SUBLANES, :] = carry_scr[:, cols]
        u_scr[SUBLANES:SUBLANES + tm, :] = u
        carry_scr[:, cols] = u[tm - SUBLANES:tm, :]
        w = cw_ref[:, cols]
        xc = u_scr[SUBLANES - 3:SUBLANES - 3 + tm, :] * w[0:1]
        xc = xc + u_scr[SUBLANES - 2:SUBLANES - 2 + tm, :] * w[1:2]
        xc = xc + u_scr[SUBLANES - 1:SUBLANES - 1 + tm, :] * w[2:3]
        xc = xc + u * w[3:4] + cb_ref[:, cols]
        a, b = _lru_gates(xc, wga_ref, bga_ref[:, cols], wgx_ref, bgx_ref[:, cols], lam_ref[:, cols],
                          c0 // LRU_BW)
        a_ref[:, cols] = a
        b_ref[:, cols] = b

    @pl.when(ti == tpb - 1)
    def _():
        st_ref[...] = carry_scr[SUBLANES - keep:SUBLANES, :]


def _lru_proj(x, n_batch, norm_g, w, conv_w, conv_b, w_ga, b_ga, w_gx, b_gx, lam, buf, tm=256):
    n = x.shape[0]
    tpb = n // n_batch // tm
    row = lambda i: (i, 0)
    wide = pl.BlockSpec((tm, D_RNN), row)
    vec = _resident((1, D_RNN))
    st = pl.BlockSpec((None, LRU_CONV - 1, D_RNN), lambda i: (i // tpb, 0, 0))
    f = jax.ShapeDtypeStruct((n, D_RNN), F32)
    r1 = lambda v: v.reshape(1, D_RNN)
    return pl.pallas_call(
        functools.partial(_lru_proj_kernel, tm=tm, tpb=tpb),
        grid=(n // tm,),
        in_specs=[pl.BlockSpec((tm, D_MODEL), row), _resident((1, D_MODEL)), _resident(w.shape),
                  _resident(conv_w.shape), vec, _resident(w_ga.shape), vec, _resident(w_gx.shape), vec, vec, st],
        out_specs=[wide, wide, wide, st],
        out_shape=[f, f, f, jax.ShapeDtypeStruct((n_batch, LRU_CONV - 1, D_RNN), F32)],
        scratch_shapes=[pltpu.VMEM((SUBLANES, D_RNN), F32), pltpu.VMEM((SUBLANES + tm, _LRU_CW), F32)],
        compiler_params=_cparams("arbitrary"),
        name="lru_proj",
    )(x, norm_g.reshape(1, D_MODEL), w, conv_w, r1(conv_b), w_ga, r1(b_ga), w_gx, r1(b_gx), r1(lam), buf)


def _lru_scan_kernel(a_ref, b_ref, gate_ref, x_ref, h0_ref, wout_ref, y_ref, hl_ref, h_scr, hs_scr, *, rt, tpb):
    ti = pl.program_id(1)

    @pl.when(ti == 0)
    def _():
        h_scr[...] = h0_ref[...]

    def step(t, h):
        h = a_ref[pl.ds(t, 1), :] * h + b_ref[pl.ds(t, 1), :]
        hs_scr[pl.ds(t, 1), :] = h
        return h

    h = lax.fori_loop(0, rt, step, h_scr[...], unroll=8)
    h_scr[...] = h
    y = (hs_scr[...] * jax.nn.gelu(gate_ref[...])).astype(BF16)
    y_ref[...] = x_ref[...] + _dot(y, wout_ref[...])

    @pl.when(ti == tpb - 1)
    def _():
        hl_ref[...] = h


def _lru_scan(a, b, gate, x, h0, w_out, n_batch, rt=256):
    n = x.shape[0]
    tpb = n // n_batch // rt
    wide = pl.BlockSpec((rt, D_RNN), lambda bb, i: (bb * tpb + i, 0))
    st = pl.BlockSpec((None, 1, D_RNN), lambda bb, i: (bb, 0, 0))
    y, hl = pl.pallas_call(
        functools.partial(_lru_scan_kernel, rt=rt, tpb=tpb),
        grid=(n_batch, tpb),
        in_specs=[wide, wide, wide, wide, st, _resident(w_out.shape)],
        out_specs=[wide, st],
        out_shape=[jax.ShapeDtypeStruct((n, D_MODEL), F32), jax.ShapeDtypeStruct((n_batch, 1, D_RNN), F32)],
        scratch_shapes=[pltpu.VMEM((1, D_RNN), F32), pltpu.VMEM((rt, D_RNN), F32)],
        compiler_params=_cparams("parallel", "arbitrary"),
        name="lru_scan",
    )(a, b, gate, x, h0.reshape(n_batch, 1, D_RNN), w_out)
    return y, hl.reshape(n_batch, D_RNN)


def _lru_prompt_layer(x, n_batch, norm_g, h0, conv_buf, w_in, conv_w, conv_b, w_ga, b_ga, w_gx, b_gx, lam, w_out):
    gate, a, b, new_buf = _lru_proj(x, n_batch, norm_g, w_in.astype(BF16), conv_w, conv_b, w_ga.astype(BF16), b_ga,
                                    w_gx.astype(BF16), b_gx, lam, conv_buf)
    y, h_last = _lru_scan(a, b, gate, x, h0, w_out.astype(BF16), n_batch)
    return y, h_last, new_buf


def _final_norm_kernel(x_ref, g_ref, o_ref):
    o_ref[...] = _rmsnorm(x_ref[...], g_ref[...])


def _final_norm(x, g, tm):
    n = x.shape[0]
    row = lambda i: (i, 0)
    return pl.pallas_call(
        _final_norm_kernel,
        grid=(n // tm,),
        in_specs=[pl.BlockSpec((tm, D_MODEL), row), _resident((1, D_MODEL))],
        out_specs=pl.BlockSpec((tm, D_MODEL), row),
        out_shape=jax.ShapeDtypeStruct((n, D_MODEL), F32),
        compiler_params=_cparams("parallel"),
        name="final_norm",
    )(x, g.reshape(1, D_MODEL))


def _norm_matmul_small_kernel(x_ref, g_ref, w_ref, o_ref):
    xn = _rmsnorm(x_ref[...], g_ref[...]).astype(BF16)
    o_ref[...] = _dot(xn, w_ref[...])


def _norm_matmul_small(x, g, w, n_col_tiles):
    n, k = x.shape[0], w.shape[1]
    tn = k // n_col_tiles
    return pl.pallas_call(
        _norm_matmul_small_kernel,
        grid=(n_col_tiles,),
        in_specs=[_resident((n, D_MODEL)), _resident((1, D_MODEL)), pl.BlockSpec((D_MODEL, tn), lambda j: (0, j))],
        out_specs=pl.BlockSpec((n, tn), lambda j: (0, j)),
        out_shape=jax.ShapeDtypeStruct((n, k), F32),
        compiler_params=_cparams("parallel"),
        name="norm_matmul_small",
    )(x, g.reshape(1, D_MODEL), w)


_CMP_PAGES_PER_STEP = 8
_CMP_BLK_PER_PAGE = PAGE_SIZE // L_CMP


def _nsa_compress_sample_kernel(pt_ref, *refs, n_pages, n_blk, n_out):
    pages = refs[:_CMP_PAGES_PER_STEP]
    new_ref, bd_ref, pe_ref, o_ref, rows_scr = refs[_CMP_PAGES_PER_STEP:]
    b = pl.program_id(0)
    pg = pl.program_id(1)
    past_blk = n_pages * _CMP_BLK_PER_PAGE
    pair_rows = 2 * PAGE_SIZE
    out_row = _iota((pair_rows, pair_rows), 0)
    src_row = _iota((pair_rows, pair_rows), 1)
    blk_per_pair = 2 * _CMP_BLK_PER_PAGE
    perm = jnp.where(src_row == (out_row % blk_per_pair) * L_CMP + out_row // blk_per_pair, 1.0, 0.0).astype(BF16)
    for r in range(0, _CMP_PAGES_PER_STEP, 2):
        c0 = pl.multiple_of((pg * _CMP_PAGES_PER_STEP + r) * _CMP_BLK_PER_PAGE, blk_per_pair)
        for kv in range(2):
            both = jnp.concatenate(
                [pages[r + i][kv].reshape(NSA_KVW, PAGE_SIZE) for i in range(2)], axis=1)
            moved = _dot_nt(perm, both.astype(BF16))
            for l in range(L_CMP):
                rows_scr[kv, l, pl.ds(c0, blk_per_pair), :] = moved[l * blk_per_pair:(l + 1) * blk_per_pair]

    @pl.when(pg == pl.num_programs(1) - 1)
    def _():
        new = new_ref[pl.ds(b, 1), :]
        n_tail = n_blk - past_blk
        first = _iota((n_tail, NSA_KVW), 0) == 0
        for kv in range(2):
            for l in range(L_CMP):
                tail = jnp.where(first, new[:, kv * NSA_KVW:(kv + 1) * NSA_KVW], 0.0) if l == 0 else (
                    jnp.zeros((n_tail, NSA_KVW), F32))
                rows_scr[kv, l, past_blk:n_blk, :] = tail
        o_ref[...] = jnp.zeros(o_ref.shape, F32)

        def load_rows(kv, l):
            return rows_scr[kv, l]

        _compress_planes(load_rows, bd_ref, pe_ref, o_ref.at[:, :, 0:n_blk, :], n_blk)


def _rows_last(cache):
    nd = cache.ndim
    return jnp.transpose(cache, tuple(range(nd - 4)) + (nd - 3, nd - 2, nd - 1, nd - 4))


def _nsa_compress_sample(cache_t, layer, page_table, new_rows, bd, pe2):
    n_b, n_pages = page_table.shape
    past = n_pages * PAGE_SIZE
    n_blk = -(-(past + 1) // L_SEL) * L_SEL // L_CMP
    n_blk = -(-n_blk // SUBLANES) * SUBLANES
    n_out = -(-n_blk // LANES) * LANES
    cache4 = cache_t

    def page_spec(r):
        return pl.BlockSpec((None, None, 2, NSA_KV, NSA_DH, PAGE_SIZE),
                            lambda b, pg, pt: (layer, pt[b, pg * _CMP_PAGES_PER_STEP + r], 0, 0, 0, 0))

    grid_spec = pltpu.PrefetchScalarGridSpec(
        num_scalar_prefetch=1,
        grid=(n_b, n_pages // _CMP_PAGES_PER_STEP),
        in_specs=[page_spec(r) for r in range(_CMP_PAGES_PER_STEP)]
        + [pl.BlockSpec(new_rows.shape, lambda b, pg, pt: (0, 0)),
           pl.BlockSpec(bd.shape, lambda b, pg, pt: (0, 0, 0, 0)),
           pl.BlockSpec(pe2.shape, lambda b, pg, pt: (0, 0, 0))],
        out_specs=pl.BlockSpec((None, 2, NSA_KV, n_out, NSA_DH), lambda b, pg, pt: (b, 0, 0, 0, 0)),
        scratch_shapes=[pltpu.VMEM((2, L_CMP, n_blk, NSA_KVW), F32)])
    return pl.pallas_call(
        functools.partial(_nsa_compress_sample_kernel, n_pages=n_pages, n_blk=n_blk, n_out=n_out),
        grid_spec=grid_spec,
        out_shape=jax.ShapeDtypeStruct((n_b, 2, NSA_KV, n_out, NSA_DH), F32),
        compiler_params=_cparams("parallel", "arbitrary"),
        name="nsa_compress_sample",
    )(page_table, *([cache4] * _CMP_PAGES_PER_STEP), new_rows, bd, pe2)


def _pad_rows8(x):
    return jnp.concatenate([x, jnp.zeros((SUBLANES - x.shape[0], x.shape[1]), x.dtype)], axis=0)


_TOPK_ROWS_PER_STEP = 8


def _nsa_cmp_topk_sample_kernel(q_ref, kvc_ref, oc_ref, idx_ref, *, qpos, n_sel):
    nb = q_ref.shape[0]
    n_cmp = kvc_ref.shape[3]
    n_lane = 2 * LANES
    c_end = (_iota((SUBLANES, n_cmp), 1) + 1) * L_CMP - 1
    valid = c_end <= qpos
    any_valid = 1.0 if qpos >= L_CMP - 1 else 0.0
    pair_mat = jnp.where((_iota((n_cmp, n_lane), 0) >> 1) == _iota((n_cmp, n_lane), 1), 1.0, 0.0)
    imps = []
    for i in range(nb):
        for g in range(NSA_KV):
            q8 = _pad_rows8(q_ref[i, g * NSA_HPG:(g + 1) * NSA_HPG, :])
            s = _dot_nt(q8, kvc_ref[i, 0, g].astype(BF16))
            p = _softmax_rows(jnp.where(valid, s, NEG)) * any_valid
            o_c = _dot(p.astype(BF16), kvc_ref[i, 1, g].astype(BF16))
            oc_ref[i, g * NSA_HPG:(g + 1) * NSA_HPG, :] = o_c[0:NSA_HPG]
            imps.append(p[0:1] + p[1:2] + p[2:3] + p[3:4])
        imps.append(jnp.zeros((SUBLANES - NSA_KV, n_cmp), F32))
    rows = nb * SUBLANES
    imp = jnp.concatenate(imps, axis=0)
    pair = _dot_f32(imp, pair_mat)
    blk = _iota((rows, n_lane), 1)
    forced = (blk == 0) | (blk == qpos // L_SEL)
    avail = blk * L_SEL <= qpos
    val = jnp.where(forced, FORCE, jnp.where(avail, pair, -1.0))
    val = jnp.where(blk < n_sel, val, -2.0)
    out = jnp.zeros((rows, LANES), I32)
    out_lane = _iota((rows, LANES), 1)
    for r in range(N_SEL):
        m = jnp.max(val, axis=-1, keepdims=True)
        first = jnp.min(jnp.where(val == m, blk, n_lane), axis=-1, keepdims=True)
        out = jnp.where(out_lane == r, first, out)
        val = jnp.where(blk == first, -3.0, val)
    idx_ref[...] = out.reshape(nb, SUBLANES, LANES)


def _nsa_cmp_topk_sample(q3, kvc, qpos, n_sel):
    n_b = q3.shape[0]
    n_cmp = kvc.shape[3]
    nb = _TOPK_ROWS_PER_STEP if n_b % _TOPK_ROWS_PER_STEP == 0 else 1
    return pl.pallas_call(
        functools.partial(_nsa_cmp_topk_sample_kernel, qpos=qpos, n_sel=n_sel),
        grid=(n_b // nb,),
        in_specs=[pl.BlockSpec((nb, NSA_HEADS, NSA_DH), lambda b: (b, 0, 0)),
                  pl.BlockSpec((nb, 2, NSA_KV, n_cmp, NSA_DH), lambda b: (b, 0, 0, 0, 0))],
        out_specs=[pl.BlockSpec((nb, NSA_HEADS, NSA_DH), lambda b: (b, 0, 0)),
                   pl.BlockSpec((nb, SUBLANES, LANES), lambda b: (b, 0, 0))],
        out_shape=[jax.ShapeDtypeStruct((n_b, NSA_HEADS, NSA_DH), F32),
                   jax.ShapeDtypeStruct((n_b, SUBLANES, LANES), I32)],
        compiler_params=_cparams("parallel"),
        name="nsa_cmp_topk_sample",
    )(q3, kvc)


def _nsa_gather_sel_kernel(pt_ref, idx_ref, *refs):
    blocks = refs[:NSA_KV * N_SEL]
    o_ref = refs[NSA_KV * N_SEL]
    for g in range(NSA_KV):
        for kv in range(2):
            pages = [blocks[g * N_SEL + r][kv] for r in range(N_SEL)]
            o_ref[g, kv] = jnp.concatenate(pages, axis=1).astype(BF16)


def _nsa_gather_sel(cache_t, layer, page_table, idx):
    n_b, n_pages = page_table.shape
    past = n_pages * PAGE_SIZE
    halves = PAGE_SIZE // L_SEL
    last_blk = past // L_SEL - 1

    def blk_spec(n):
        g = n // N_SEL

        def index(b, pt, ix):
            s = jnp.clip(ix[b, n], 0, last_blk)
            return (layer, pt[b, s // halves], 0, g, 0, 0)
        return pl.BlockSpec((None, None, 2, None, NSA_DH, PAGE_SIZE), index)

    grid_spec = pltpu.PrefetchScalarGridSpec(
        num_scalar_prefetch=2,
        grid=(n_b,),
        in_specs=[blk_spec(n) for n in range(NSA_KV * N_SEL)],
        out_specs=pl.BlockSpec((None, NSA_KV, 2, NSA_DH, N_SEL * PAGE_SIZE), lambda b, pt, ix: (b, 0, 0, 0, 0)))
    return pl.pallas_call(
        _nsa_gather_sel_kernel,
        grid_spec=grid_spec,
        out_shape=jax.ShapeDtypeStruct((n_b, NSA_KV, 2, NSA_DH, N_SEL * PAGE_SIZE), BF16),
        name="nsa_gather_sel",
    )(page_table, idx, *([cache_t] * (NSA_KV * N_SEL)))


def _attend_with_new(q8, k_t, v_t, live, k_new, v_new, new_live):
    s = jnp.where(live > 0.5, _dot(q8, k_t), NEG)
    k_new16 = k_new.astype(BF16).astype(F32)
    s_new = jnp.sum(q8.astype(F32) * k_new16, axis=-1, keepdims=True)
    s_new = jnp.where(new_live > 0.5, s_new, NEG)
    m = jnp.maximum(jnp.max(s, axis=-1, keepdims=True), s_new)
    e = jnp.exp(s - m)
    e_new = jnp.exp(s_new - m)
    inv = 1.0 / (jnp.sum(e, axis=-1, keepdims=True) + e_new)
    p16 = (e * inv).astype(BF16)
    p_new16 = (e_new * inv).astype(BF16).astype(F32)
    return _dot_nt(p16, v_t) + p_new16 * v_new.astype(BF16).astype(F32)


def _row_to_col(row):
    return jnp.broadcast_to(row, (SUBLANES, row.shape[1])).T[:, 0:1]


def _nsa_sel_win_sample_kernel(idx_ref, qr_ref, oc_ref, gate_ref, selnew_ref, winnew_ref, win_ref, kv_ref,
                               o_ref, wout_ref, *, qpos, past):
    gates = gate_ref[...]
    glane = _iota((1, LANES), 1)
    n_key = N_SEL * PAGE_SIZE
    lane_k = _iota((SUBLANES, n_key), 1)
    log_page = PAGE_SIZE.bit_length() - 1
    wb = win_ref.shape[3]
    wlane = _iota((SUBLANES, wb), 1)
    kwpos = past - wb + wlane
    dist = qpos - kwpos
    wlive = jnp.where((dist >= 0) & (dist <= WINDOW) & (kwpos >= 0), 1.0, 0.0)
    new_in_window = jnp.full((1, 1), 1.0 if 0 <= qpos - past <= WINDOW else 0.0, F32)
    expand = jnp.where(_iota((LANES, n_key), 0) == (_iota((LANES, n_key), 1) >> log_page), 1.0, 0.0)
    idx_f = idx_ref[...].astype(F32)
    blk_of_lane = _dot_f32(jnp.where(_iota((SUBLANES, LANES), 1) < N_SEL, idx_f, 0.0), expand)
    sel_new = selnew_ref[...]
    win_new = winnew_ref[...]
    last_lane = _iota((NSA_DH, wb), 1) == wb - 1
    for g in range(NSA_KV):
        kc = slice(g * NSA_DH, (g + 1) * NSA_DH)
        vc = slice(NSA_KVW + g * NSA_DH, NSA_KVW + (g + 1) * NSA_DH)
        q8 = _pad_rows8(qr_ref[g * NSA_HPG:(g + 1) * NSA_HPG, :])
        blk_g = blk_of_lane[g:g + 1, :]
        new_blk = float(qpos // L_SEL)
        new_selected = jnp.max(jnp.where(blk_g == new_blk, 1.0, 0.0), axis=-1, keepdims=True)
        halves = PAGE_SIZE // L_SEL
        page_g = jnp.floor(jnp.minimum(blk_g, float(past // L_SEL - 1)) * (1.0 / halves))
        kpos = page_g * float(PAGE_SIZE) + (lane_k & (PAGE_SIZE - 1)).astype(F32)
        in_block = jnp.floor(kpos * (1.0 / L_SEL)) == blk_g
        live = jnp.where(in_block & (kpos < float(past)) & (kpos <= float(qpos)), 1.0, 0.0)
        o_s = _attend_with_new(q8, kv_ref[g, 0], kv_ref[g, 1], live, sel_new[:, kc], sel_new[:, vc], new_selected)
        kw_t = win_ref[0, g]
        vw_t = win_ref[1, g]
        o_w = _attend_with_new(q8, kw_t.astype(BF16), vw_t.astype(BF16), wlive, win_new[:, kc], win_new[:, vc],
                               new_in_window)
        wout_ref[0, g] = jnp.where(last_lane, _row_to_col(win_new[:, kc]), pltpu.roll(kw_t, wb - 1, 1))
        wout_ref[1, g] = jnp.where(last_lane, _row_to_col(win_new[:, vc]), pltpu.roll(vw_t, wb - 1, 1))
        o_c = oc_ref[g * NSA_HPG:(g + 1) * NSA_HPG, :]
        rows = []
        for j in range(NSA_HPG):
            acc = jnp.zeros((1, NSA_DH), F32)
            for br, ob in enumerate((o_c, o_s, o_w)):
                col = br * NSA_HEADS + g * NSA_HPG + j
                gcol = jnp.sum(jnp.where(glane == col, gates, 0.0), axis=-1, keepdims=True)
                acc = acc + gcol * ob[j:j + 1, :]
            rows.append(acc)
        o_ref[g * NSA_HPG:(g + 1) * NSA_HPG, :] = jnp.concatenate(rows, axis=0)


def _nsa_sel_win_sample(kv_sel, idx8, qr3, oc3, gates, sel_new, win_new, win_t, qpos, past):
    n_b = qr3.shape[0]
    wb = win_t.shape[-1]
    b3 = lambda b: (b, 0, 0)
    b5 = lambda b: (b, 0, 0, 0, 0)
    wspec = pl.BlockSpec((None, 2, NSA_KV, NSA_DH, wb), b5)
    return pl.pallas_call(
        functools.partial(_nsa_sel_win_sample_kernel, qpos=qpos, past=past),
        grid=(n_b,),
        in_specs=[pl.BlockSpec((None, SUBLANES, LANES), b3),
                  pl.BlockSpec((None, NSA_HEADS, NSA_DH), b3), pl.BlockSpec((None, NSA_HEADS, NSA_DH), b3),
                  pl.BlockSpec((None, 1, LANES), b3), pl.BlockSpec((None, 1, 2 * NSA_KVW), b3),
                  pl.BlockSpec((None, 1, 2 * NSA_KVW), b3), wspec,
                  pl.BlockSpec((None, NSA_KV, 2, NSA_DH, N_SEL * PAGE_SIZE), b5)],
        out_specs=[pl.BlockSpec((None, NSA_HEADS, NSA_DH), b3), wspec],
        out_shape=[jax.ShapeDtypeStruct((n_b, NSA_HEADS, NSA_DH), F32), jax.ShapeDtypeStruct(win_t.shape, F32)],
        compiler_params=_cparams("parallel"),
        name="nsa_sel_win_sample",
    )(idx8, qr3, oc3, gates.reshape(n_b, 1, LANES), sel_new.reshape(n_b, 1, -1),
      win_new.reshape(n_b, 1, -1), win_t, kv_sel)


def _nsa_sample_layer(x, layer, norm_g, w_in, cmp_pe, cmp_w, w_out, cache_cmp, cache_sel, win_state, page_table,
                      tables):
    n_b = x.shape[0]
    past = page_table.shape[1] * PAGE_SIZE
    qpos = past
    q, qr, cmp_rows, sel_rows, win_rows, gates, _ = _nsa_proj(
        x, norm_g.reshape(1, D_MODEL), _nsa_w_in_padded(w_in), tables, 1, n_b)
    bd, pe2 = _cmp_weights(cmp_w, cmp_pe)
    kvc = _nsa_compress_sample(cache_cmp, layer, page_table, cmp_rows, bd, pe2)
    n_sel = -(-(past + 1) // L_SEL)
    oc3, idx8 = _nsa_cmp_topk_sample(q.reshape(n_b, NSA_HEADS, NSA_DH), kvc, qpos, n_sel)
    idx = idx8[:, :NSA_KV, :N_SEL].reshape(n_b, NSA_KV * N_SEL)
    kv_sel = _nsa_gather_sel(cache_sel, layer, page_table, idx)
    o3, win_out = _nsa_sel_win_sample(kv_sel, idx8, qr.reshape(n_b, NSA_HEADS, NSA_DH), oc3, gates,
                                      sel_rows, win_rows, win_state, qpos, past)
    y = _matmul_residual(o3.reshape(n_b, NSA_Q), w_out.astype(BF16), x, n_b)
    return y, cmp_rows, sel_rows, win_out


def _gdn_sample_pre_kernel(z_ref, b0_ref, b1_ref, b2_ref, cw_ref, alog_ref, dtb_ref, q_ref, k_ref, v_ref, gb_ref):
    outs = (q_ref, k_ref, v_ref)
    for c in range(GDN_CONV_CH // _GDN_CW):
        c0 = c * _GDN_CW
        cols = slice(c0, c0 + _GDN_CW)
        w = cw_ref[:, cols]
        y = b0_ref[:, cols] * w[0:1] + b1_ref[:, cols] * w[1:2] + b2_ref[:, cols] * w[2:3] + z_ref[:, cols] * w[3:4]
        o_ref = outs[c0 // GDN_QK]
        o_ref[:, c0 % GDN_QK:c0 % GDN_QK + _GDN_CW] = _gdn_post_conv(y, c0)
    gb_ref[...] = _gdn_gates(z_ref[:, GDN_CONV_CH + GDN_VW:_GDN_W_COLS], alog_ref[...], dtb_ref[...])


def _gdn_sample_pre(z, bufs, conv_w, alog, dtb):
    n = z.shape[0]
    f = jax.ShapeDtypeStruct((n, GDN_QK), F32)
    return pl.pallas_call(
        _gdn_sample_pre_kernel,
        out_shape=[f, f, f, jax.ShapeDtypeStruct((n, LANES), F32)],
        compiler_params=pltpu.CompilerParams(vmem_limit_bytes=VMEM_LIMIT),
        name="gdn_sample_pre",
    )(z, *bufs, conv_w, alog, dtb)


def _gdn_sample_state_kernel(s0_ref, qt_ref, kt_ref, v_ref, gb_ref, gate_ref, ng_ref, s_ref, o_ref):
    gb = gb_ref[...]
    lane = _iota((1, LANES), 1)
    ng = ng_ref[...]
    for h in range(GDN_HEADS):
        cols = slice(h * GDN_DV, (h + 1) * GDN_DV)
        g_h = jnp.sum(jnp.where(lane == h, gb, 0.0), axis=-1, keepdims=True)
        beta = jnp.sum(jnp.where(lane == GDN_HEADS + h, gb, 0.0), axis=-1, keepdims=True)
        eg = jnp.exp(g_h)
        s_prev = s0_ref[h]
        k_col = kt_ref[:, h:h + 1]
        q_col = qt_ref[:, h:h + 1]
        v_row = v_ref[:, cols]
        u = v_row * beta - jnp.sum(s_prev * (k_col * beta * eg), axis=0, keepdims=True)
        attn = jnp.sum(q_col * k_col, axis=0, keepdims=True)
        o = jnp.sum(s_prev * (q_col * eg), axis=0, keepdims=True) + attn * u
        s_ref[h] = s_prev * eg + k_col * u
        o_ref[:, cols] = (_rmsnorm(o, ng) * jax.nn.silu(gate_ref[:, cols])).astype(BF16)


def _gdn_sample_state(s0, q_t, k_t, v, gb, gate, head_norm_g):
    n_b = s0.shape[0]
    st = pl.BlockSpec((None, GDN_HEADS, GDN_DK, GDN_DV), lambda b: (b, 0, 0, 0))
    col = pl.BlockSpec((None, GDN_DK, GDN_HEADS), lambda b: (b, 0, 0))
    wide = pl.BlockSpec((None, 1, GDN_VW), lambda b: (b, 0, 0))
    return pl.pallas_call(
        _gdn_sample_state_kernel,
        grid=(n_b,),
        in_specs=[st, col, col, wide, pl.BlockSpec((None, 1, LANES), lambda b: (b, 0, 0)), wide,
                  _resident((1, GDN_DV))],
        out_specs=[st, wide],
        out_shape=[jax.ShapeDtypeStruct(s0.shape, F32), jax.ShapeDtypeStruct((n_b, 1, GDN_VW), BF16)],
        compiler_params=_cparams("parallel"),
        name="gdn_sample_state",
    )(s0, q_t, k_t, v.reshape(n_b, 1, GDN_VW), gb.reshape(n_b, 1, LANES), gate.reshape(n_b, 1, GDN_VW),
      head_norm_g.reshape(1, GDN_DV))


def _gdn_sample_layer(x, norm_g, s0, conv_buf, w_in, conv_w, a_log, dt_bias, head_norm_g, w_out):
    n_b = x.shape[0]
    z = _norm_matmul_small(x, norm_g, _gdn_w_in_padded(w_in), 3)
    bufs = [conv_buf[:, j, :] for j in range(GDN_CONV - 1)]
    q, k, v, gb = _gdn_sample_pre(z, bufs, conv_w, _lane_row(a_log), _lane_row(dt_bias))
    to_cols = lambda t: t.reshape(n_b, GDN_HEADS, GDN_DK).transpose(0, 2, 1)
    gate = z[:, GDN_CONV_CH:GDN_CONV_CH + GDN_VW]
    s_new, o = _gdn_sample_state(s0, to_cols(q), to_cols(k), v, gb, gate, head_norm_g)
    y = _matmul_residual(o.reshape(n_b, GDN_VW), w_out.astype(BF16), x, n_b)
    new_buf = jnp.concatenate([conv_buf[:, 1:, :], z[:, None, :GDN_CONV_CH]], axis=1)
    return y, s_new, new_buf


def _lru_sample_kernel(z_ref, b0_ref, b1_ref, b2_ref, cw_ref, cb_ref, wga_ref, bga_ref, wgx_ref, bgx_ref, lam_ref,
                       h0_ref, h_ref, y_ref):
    for c in range(D_RNN // _LRU_CW):
        c0 = c * _LRU_CW
        cols = slice(c0, c0 + _LRU_CW)
        w = cw_ref[:, cols]
        xc = (b0_ref[:, cols] * w[0:1] + b1_ref[:, cols] * w[1:2] + b2_ref[:, cols] * w[2:3]
              + z_ref[:, D_RNN + c0:D_RNN + c0 + _LRU_CW] * w[3:4] + cb_ref[:, cols])
        a, b = _lru_gates(xc, wga_ref, bga_ref[:, cols], wgx_ref, bgx_ref[:, cols], lam_ref[:, cols], c0 // LRU_BW)
        h = a * h0_ref[:, cols] + b
        h_ref[:, cols] = h
        y_ref[:, cols] = (h * jax.nn.gelu(z_ref[:, cols])).astype(BF16)


def _lru_sample_layer(x, norm_g, h0, conv_buf, w_in, conv_w, conv_b, w_ga, b_ga, w_gx, b_gx, lam, w_out):
    n_b = x.shape[0]
    z = _norm_matmul_small(x, norm_g, w_in.astype(BF16), 2)
    r1 = lambda v: v.reshape(1, D_RNN)
    bufs = [conv_buf[:, j, :] for j in range(LRU_CONV - 1)]
    h, y = pl.pallas_call(
        _lru_sample_kernel,
        out_shape=[jax.ShapeDtypeStruct((n_b, D_RNN), F32), jax.ShapeDtypeStruct((n_b, D_RNN), BF16)],
        compiler_params=pltpu.CompilerParams(vmem_limit_bytes=VMEM_LIMIT),
        name="lru_sample",
    )(z, *bufs, conv_w, r1(conv_b), w_ga.astype(BF16), r1(b_ga), w_gx.astype(BF16), r1(b_gx), r1(lam), h0)
    y = _matmul_residual(y, w_out.astype(BF16), x, n_b)
    new_buf = jnp.concatenate([conv_buf[:, 1:, :], z[:, None, D_RNN:]], axis=1)
    return y, h, new_buf


def _ffn_sample_kernel(x_ref, g_ref, wv_ref, wg_ref, cwv_ref, cwg_ref, b0v_ref, b0g_ref, b1v_ref, b1g_ref, wout_ref,
                       y_ref, uv_ref, ug_ref, xn_scr):
    c = pl.program_id(0)

    @pl.when(c == 0)
    def _():
        x = x_ref[...]
        xn_scr[...] = _rmsnorm(x, g_ref[...]).astype(BF16)
        y_ref[...] = x

    xn = xn_scr[...]
    uv = _dot(xn, wv_ref[...])
    ug = _dot(xn, wg_ref[...])
    uv_ref[...] = uv
    ug_ref[...] = ug
    cwv, cwg = cwv_ref[...], cwg_ref[...]
    val = b0v_ref[...] * cwv[0:1] + b1v_ref[...] * cwv[1:2] + uv * cwv[2:3]
    gt = b0g_ref[...] * cwg[0:1] + b1g_ref[...] * cwg[1:2] + ug * cwg[2:3]
    y_ref[...] += _dot((val * jax.nn.silu(gt)).astype(BF16), wout_ref[...])


def _ffn_sample(x, norm_g, w_in, conv_w, w_out, buf):
    n_b = x.shape[0]
    n_c = D_FF // _FFN_CW
    b0, b1 = buf[:, 0, :], buf[:, 1, :]
    val = lambda shape0: pl.BlockSpec((shape0, _FFN_CW), lambda c: (0, c))
    gat = lambda shape0: pl.BlockSpec((shape0, _FFN_CW), lambda c: (0, n_c + c))
    y, uv, ug = pl.pallas_call(
        _ffn_sample_kernel,
        grid=(n_c,),
        in_specs=[_resident((n_b, D_MODEL)), _resident((1, D_MODEL)), val(D_MODEL), gat(D_MODEL),
                  val(FFN_CONV), gat(FFN_CONV), val(n_b), gat(n_b), val(n_b), gat(n_b),
                  pl.BlockSpec((_FFN_CW, D_MODEL), lambda c: (c, 0))],
        out_specs=[pl.BlockSpec((n_b, D_MODEL), lambda c: (0, 0)), val(n_b), val(n_b)],
        out_shape=[jax.ShapeDtypeStruct((n_b, D_MODEL), F32), jax.ShapeDtypeStruct((n_b, D_FF), F32),
                   jax.ShapeDtypeStruct((n_b, D_FF), F32)],
        scratch_shapes=[pltpu.VMEM((n_b, D_MODEL), BF16)],
        compiler_params=_cparams("arbitrary"),
        name="ffn_sample",
    )(x, norm_g.reshape(1, D_MODEL), w_in, w_in, conv_w, conv_w, b0, b0, b1, b1, w_out)
    new_buf = jnp.stack([b1, jnp.concatenate([uv, ug], axis=-1)], axis=1)
    return y, new_buf


def kernel(x_prompt, x_sample, cache_nsa_cmp, cache_nsa_sel, state_nsa_win, state_gdn_S, state_gdn_conv,
           state_lru_h, state_lru_conv, state_ffn_conv, page_table,
           norm_mix_g, norm_ffn_g, norm_final_g,
           nsa_w_in, nsa_cmp_pe, nsa_cmp_w, nsa_w_out,
           gdn_w_in, gdn_conv_w, gdn_A_log, gdn_dt_bias, gdn_norm_g, gdn_w_out,
           lru_w_in, lru_conv_w, lru_conv_b, lru_w_ga, lru_b_ga, lru_w_gx, lru_b_gx, lru_lambda, lru_w_out,
           ffn_w_in, ffn_conv_w, ffn_w_out):
    n_p, t_len, _ = x_prompt.shape
    n_s = x_sample.shape[0]
    past = page_table.shape[1] * PAGE_SIZE
    wl = min(WINDOW, t_len)
    xp = x_prompt.reshape(n_p * t_len, D_MODEL)
    xs = x_sample.reshape(n_s, D_MODEL)
    tab_p = _rope_tables(jnp.arange(t_len, dtype=I32))
    tab_s = _rope_tables(jnp.full((n_s,), past, I32))
    cache_cmp_t = _rows_last(cache_nsa_cmp)
    cache_sel_t = _rows_last(cache_nsa_sel)
    win_t = _rows_last(state_nsa_win)
    ffn_w_in16 = ffn_w_in.astype(BF16)
    ffn_w_out16 = ffn_w_out.astype(BF16)
    zeros = lambda *shape: jnp.zeros(shape, F32)
    kv5 = lambda rows, nb: rows.reshape(nb, -1, 2, NSA_KV, NSA_DH)

    p_nsa, s_nsa, p_gdn, s_gdn, p_lru, s_lru, p_ffn, s_ffn = [], [], [], [], [], [], [], []
    for li in range(DEPTH):
        j = li // N_MIXERS
        kind = li % N_MIXERS
        if kind == 0:
            xp, c, s, w = _nsa_prompt_layer(xp, n_p, norm_mix_g[li], nsa_w_in[j], nsa_cmp_pe[j], nsa_cmp_w[j],
                                            nsa_w_out[j], tab_p)
            p_nsa.append((kv5(c, n_p), kv5(s, n_p), kv5(w, n_p)[:, t_len - wl:]))
            xs, c, s, w = _nsa_sample_layer(xs, j, norm_mix_g[li], nsa_w_in[j], nsa_cmp_pe[j], nsa_cmp_w[j],
                                            nsa_w_out[j], cache_cmp_t, cache_sel_t, win_t[j], page_table, tab_s)
            s_nsa.append((kv5(c, n_s), kv5(s, n_s), jnp.transpose(w, (0, 4, 1, 2, 3))))
        elif kind == 1:
            args = (gdn_w_in[j], gdn_conv_w[j], gdn_A_log[j], gdn_dt_bias[j], gdn_norm_g[j], gdn_w_out[j])
            xp, s_new, buf = _gdn_prompt_layer(xp, n_p, norm_mix_g[li], zeros(n_p, GDN_HEADS, GDN_DK, GDN_DV),
                                               zeros(n_p, GDN_CONV - 1, GDN_CONV_CH), *args)
            p_gdn.append((s_new, buf))
            xs, s_new, buf = _gdn_sample_layer(xs, norm_mix_g[li], state_gdn_S[j], state_gdn_conv[j], *args)
            s_gdn.append((s_new, buf))
        else:
            args = (lru_w_in[j], lru_conv_w[j], lru_conv_b[j], lru_w_ga[j], lru_b_ga[j], lru_w_gx[j], lru_b_gx[j],
                    lru_lambda[j], lru_w_out[j])
            xp, h, buf = _lru_prompt_layer(xp, n_p, norm_mix_g[li], zeros(n_p, D_RNN), zeros(n_p, LRU_CONV - 1, D_RNN),
                                           *args)
            p_lru.append((h, buf))
            xs, h, buf = _lru_sample_layer(xs, norm_mix_g[li], state_lru_h[j], state_lru_conv[j], *args)
            s_lru.append((h, buf))
        xp, buf = _ffn_prompt(xp, n_p, norm_ffn_g[li], ffn_w_in16[li], ffn_conv_w[li], ffn_w_out16[li],
                              zeros(n_p, FFN_CONV - 1, 2 * D_FF))
        p_ffn.append(buf)
        xs, buf = _ffn_sample(xs, norm_ffn_g[li], ffn_w_in16[li], ffn_conv_w[li], ffn_w_out16[li], state_ffn_conv[li])
        s_ffn.append(buf)

    y_prompt = _final_norm(xp, norm_final_g, 512).reshape(n_p, t_len, D_MODEL)
    y_sample = _final_norm(xs, norm_final_g, n_s).reshape(n_s, 1, D_MODEL)
    stack = lambda entries, k: jnp.stack([e[k] for e in entries])
    return (y_prompt, y_sample,
            stack(p_nsa, 0), stack(s_nsa, 0), stack(p_nsa, 1), stack(s_nsa, 1), stack(p_nsa, 2), stack(s_nsa, 2),
            stack(p_gdn, 0), stack(s_gdn, 0), stack(p_gdn, 1), stack(s_gdn, 1),
            stack(p_lru, 0), stack(s_lru, 0), stack(p_lru, 1), stack(s_lru, 1),
            jnp.stack(p_ffn), jnp.stack(s_ffn))
```

```python
import functools

import jax
import jax.numpy as jnp
from jax import lax
from jax.experimental import pallas as pl
from jax.experimental.pallas import tpu as pltpu

F32 = jnp.float32
BF16 = jnp.bfloat16
I32 = jnp.int32

D_MODEL = 1024
DEPTH = 4
PAST_LEN = 8192
PAGE_SIZE = 128
N_MIXERS = 3

NSA_DH = 64
NSA_HEADS = 16
NSA_HPG = 4
NSA_KV = 4
NSA_Q = NSA_HEADS * NSA_DH
NSA_KVW = NSA_KV * NSA_DH
ROT_DIM = NSA_DH // 4
ROPE_THETA = 500000.0
L_CMP = 32
L_SEL = 64
N_SEL = 16
WINDOW = 512
_LOG_SEL = 6
NSA_SCALE = NSA_DH ** -0.5

GDN_DK = 128
GDN_DV = 128
GDN_HEADS = 8
GDN_QK = GDN_HEADS * GDN_DK
GDN_VW = GDN_HEADS * GDN_DV
GDN_CONV = 4
GDN_CONV_CH = 2 * GDN_QK + GDN_VW
GDN_CHUNK = 64

D_RNN = D_MODEL
LRU_BLOCKS = 8
LRU_BW = D_RNN // LRU_BLOCKS
LRU_CONV = 4
LRU_C = 8.0

D_FF = 2816
FFN_CONV = 3

EPS = 1e-6
NEG = -1e30
FORCE = 1e4

LANES = 128
SUBLANES = 8
MXU_DIM = 256
VMEM_LIMIT = 56 * 1024 * 1024


def _cparams(*sem):
    return pltpu.CompilerParams(dimension_semantics=sem, vmem_limit_bytes=VMEM_LIMIT)


def _resident(shape):
    nd = len(shape)
    return pl.BlockSpec(shape, lambda *_: (0,) * nd, pipeline_mode=pl.Buffered(1))


def _rmsnorm(x, g):
    return x * lax.rsqrt(jnp.mean(x * x, axis=-1, keepdims=True) + EPS) * g


def _dot(a, b):
    return jnp.dot(a, b, preferred_element_type=F32)


def _dot_nt(a, b):
    return lax.dot_general(a, b, (((1,), (1,)), ((), ())), preferred_element_type=F32)


def _dot_tn(a, b):
    return lax.dot_general(a, b, (((0,), (0,)), ((), ())), preferred_element_type=F32)


def _dot_f32(a, b):
    return jnp.dot(a, b, precision=lax.Precision.HIGHEST, preferred_element_type=F32)


def _iota(shape, axis):
    return lax.broadcasted_iota(I32, shape, axis)


def _rope_tables(pos):
    half = ROT_DIM // 2
    inv = ROPE_THETA ** (-jnp.arange(half, dtype=F32) / half)
    ang = pos.astype(F32)[:, None] * inv[None, :]
    lane = jnp.arange(LANES) % NSA_DH
    c = jnp.cos(ang)[:, lane % half]
    s = jnp.sin(ang)[:, lane % half]
    cos = jnp.where(lane < ROT_DIM, c, 1.0)
    sin_a = jnp.where((lane >= half) & (lane < ROT_DIM), s, 0.0)
    sin_b = jnp.where(lane < half, -s, 0.0)
    return cos.astype(F32), sin_a.astype(F32), sin_b.astype(F32)


def _rope(x, cos, sin_a, sin_b):
    half = ROT_DIM // 2
    outs = []
    for c in range(x.shape[1] // LANES):
        xc = x[:, c * LANES:(c + 1) * LANES]
        outs.append(xc * cos + pltpu.roll(xc, half, 1) * sin_a + pltpu.roll(xc, LANES - half, 1) * sin_b)
    return outs[0] if len(outs) == 1 else jnp.concatenate(outs, axis=1)


_NSA_W_COLS = NSA_Q + 6 * NSA_KVW + LANES


def _store_seq_last(o_ref, rows):
    for c in range(2 * NSA_KVW // LANES):
        t = rows[:, c * LANES:(c + 1) * LANES].T
        kv, g0 = c // 2, 2 * (c % 2)
        o_ref[kv, g0] = t[:NSA_DH]
        o_ref[kv, g0 + 1] = t[NSA_DH:]


def _nsa_proj_kernel(x_ref, g_ref, w_ref, cos_ref, sa_ref, sb_ref,
                     q_ref, qr_ref, cmp_ref, sel_ref, win_ref, gate_ref, kvb_ref, *seq_last_refs):
    xn = _rmsnorm(x_ref[...], g_ref[...]).astype(BF16)
    cos, sa, sb = cos_ref[...], sa_ref[...], sb_ref[...]

    def mm(c0, c1):
        return _dot(xn, w_ref[:, c0:c1])

    zq = mm(0, NSA_Q)
    q_ref[...] = (zq * NSA_SCALE).astype(BF16)
    qr_ref[...] = (_rope(zq, cos, sa, sb) * NSA_SCALE).astype(BF16)
    c0 = NSA_Q
    zc = mm(c0, c0 + 2 * NSA_KVW)
    cmp_ref[...] = zc
    if seq_last_refs:
        _store_seq_last(seq_last_refs[0], zc)
    c0 += 2 * NSA_KVW
    for kind, o_ref in ((0, sel_ref), (1, win_ref)):
        z = mm(c0, c0 + 2 * NSA_KVW)
        kr = _rope(z[:, :NSA_KVW], cos, sa, sb)
        v = z[:, NSA_KVW:]
        o_ref[:, :NSA_KVW] = kr
        o_ref[:, NSA_KVW:] = v
        if seq_last_refs:
            _store_seq_last(seq_last_refs[1 + kind], o_ref[...])
        for g in range(NSA_KV):
            kvb_ref[2 * kind, g] = kr[:, g * NSA_DH:(g + 1) * NSA_DH].astype(BF16)
            kvb_ref[2 * kind + 1, g] = v[:, g * NSA_DH:(g + 1) * NSA_DH].astype(BF16)
        c0 += 2 * NSA_KVW
    gate_ref[...] = jax.nn.sigmoid(mm(c0, c0 + LANES))


def _nsa_proj(x, g, w, tables, n_batch, tm, seq_last=False):
    n = x.shape[0]
    t_len = n // n_batch
    tpb = t_len // tm
    row = lambda i: (i, 0)
    tab = pl.BlockSpec((tm, LANES), lambda i: (i % tpb, 0))
    out_specs = [pl.BlockSpec((tm, NSA_Q), row), pl.BlockSpec((tm, NSA_Q), row),
                 pl.BlockSpec((tm, 2 * NSA_KVW), row), pl.BlockSpec((tm, 2 * NSA_KVW), row),
                 pl.BlockSpec((tm, 2 * NSA_KVW), row), pl.BlockSpec((tm, LANES), row),
                 pl.BlockSpec((None, 4, NSA_KV, tm, NSA_DH), lambda i: (i // tpb, 0, 0, i % tpb, 0))]
    out_shape = [jax.ShapeDtypeStruct((n, NSA_Q), BF16), jax.ShapeDtypeStruct((n, NSA_Q), BF16),
                 jax.ShapeDtypeStruct((n, 2 * NSA_KVW), F32), jax.ShapeDtypeStruct((n, 2 * NSA_KVW), F32),
                 jax.ShapeDtypeStruct((n, 2 * NSA_KVW), F32), jax.ShapeDtypeStruct((n, LANES), F32),
                 jax.ShapeDtypeStruct((n_batch, 4, NSA_KV, t_len, NSA_DH), BF16)]
    if seq_last:
        out_specs += [pl.BlockSpec((None, 2, NSA_KV, NSA_DH, tm), lambda i: (i // tpb, 0, 0, 0, i % tpb))] * 3
        out_shape += [jax.ShapeDtypeStruct((n_batch, 2, NSA_KV, NSA_DH, t_len), F32)] * 3
    return pl.pallas_call(
        _nsa_proj_kernel,
        grid=(n // tm,),
        in_specs=[pl.BlockSpec((tm, D_MODEL), row), _resident((1, D_MODEL)), _resident((D_MODEL, _NSA_W_COLS)),
                  tab, tab, tab],
        out_specs=out_specs,
        out_shape=out_shape,
        compiler_params=_cparams("parallel"),
        name="nsa_proj",
    )(x, g, w, *tables)


def _nsa_w_in_padded(w_in):
    pad = _NSA_W_COLS - w_in.shape[1]
    return jnp.pad(w_in, ((0, 0), (0, pad))).astype(BF16)


def _cmp_weights(cmp_w, cmp_pe):
    eye = jnp.eye(NSA_KV, dtype=cmp_w.dtype)
    bd = jnp.einsum("gh,kldx->klgdhx", eye, cmp_w).reshape(2, L_CMP, NSA_KVW, NSA_KVW)
    return bd.astype(BF16), jnp.tile(cmp_pe, (1, 1, NSA_KV))


def _compress_planes(load_rows, bd_ref, pe_ref, o_ref, n_rows_out):
    for kv in range(2):
        acc = jnp.zeros((n_rows_out, NSA_KVW), F32)
        bias = jnp.zeros((SUBLANES, NSA_KVW), F32)
        for l in range(L_CMP):
            w = bd_ref[kv, l]
            acc = acc + _dot(load_rows(kv, l).astype(BF16), w)
            pe = jnp.broadcast_to(pe_ref[kv, l:l + 1, :], (SUBLANES, NSA_KVW)).astype(BF16)
            bias = bias + _dot(pe, w)
        out = acc + bias[0:1, :]
        for g in range(NSA_KV):
            o_ref[kv, g] = out[:, g * NSA_DH:(g + 1) * NSA_DH]


def _nsa_compress_prompt_kernel(x0_ref, x1_ref, x2_ref, x3_ref, bd_ref, pe_ref, o_ref, *, n_blk, n_pad):
    planes = (x0_ref, x1_ref, x2_ref, x3_ref)
    if n_pad > n_blk:
        o_ref[...] = jnp.zeros(o_ref.shape, F32)

    def load_rows(kv, l):
        return jnp.concatenate([planes[2 * kv + i][pl.ds(l, n_blk, stride=L_CMP), :] for i in range(2)], axis=1)

    if n_pad == n_blk:
        _compress_planes(load_rows, bd_ref, pe_ref, o_ref, n_blk)
    else:
        _compress_planes(load_rows, bd_ref, pe_ref, o_ref.at[:, :, 0:n_blk, :], n_blk)


def _nsa_compress_prompt(cmp_rows, bd, pe2, n_batch):
    n = cmp_rows.shape[0]
    t_len = n // n_batch
    n_blk = t_len // L_CMP
    n_pad = max(n_blk, LANES)
    planes = [pl.BlockSpec((t_len, LANES), functools.partial(lambda b, c: (b, c), c=c)) for c in range(4)]
    return pl.pallas_call(
        functools.partial(_nsa_compress_prompt_kernel, n_blk=n_blk, n_pad=n_pad),
        grid=(n_batch,),
        in_specs=planes + [_resident(bd.shape), _resident(pe2.shape)],
        out_specs=pl.BlockSpec((None, 2, NSA_KV, n_pad, NSA_DH), lambda b: (b, 0, 0, 0, 0)),
        out_shape=jax.ShapeDtypeStruct((n_batch, 2, NSA_KV, n_pad, NSA_DH), F32),
        compiler_params=_cparams("parallel"),
        name="nsa_compress_prompt",
    )(cmp_rows, cmp_rows, cmp_rows, cmp_rows, bd, pe2)


def _stack_heads(x):
    return jnp.concatenate([x[:, j * NSA_DH:(j + 1) * NSA_DH] for j in range(NSA_HPG)], axis=0)


def _softmax_rows(s):
    m = jnp.max(s, axis=-1, keepdims=True)
    e = jnp.exp(s - m)
    return e / jnp.sum(e, axis=-1, keepdims=True)


def _select_blocks(imp_t, t0, rank_scr):
    n_sel_pad = LANES // 2
    tq = imp_t.shape[1]
    rank_scr[...] = imp_t
    pair = rank_scr[pl.ds(0, n_sel_pad, stride=2), :] + rank_scr[pl.ds(1, n_sel_pad, stride=2), :]
    blk = _iota((n_sel_pad, tq), 0)
    tpos = t0 + _iota((n_sel_pad, tq), 1)
    forced = (blk == 0) | (blk == (tpos >> _LOG_SEL))
    avail = blk * L_SEL <= tpos
    val = jnp.where(forced, FORCE, jnp.where(avail, pair, -1.0))
    rank_scr[0:n_sel_pad, :] = val
    cnt = jnp.zeros((n_sel_pad, tq), F32)
    for s in range(n_sel_pad):
        row = jnp.broadcast_to(rank_scr[pl.ds(s, 1), :], (n_sel_pad, tq))
        ahead = (row > val) | ((row == val) & (blk > s))
        cnt = cnt + jnp.where(ahead, 1.0, 0.0)
    sel = jnp.where(cnt < float(N_SEL), 1.0, 0.0)
    return jnp.concatenate([sel, jnp.zeros((LANES - n_sel_pad, tq), F32)], axis=0)


_ATT_RB = 128


def _softmax_blocks(sc_scr, bias_scr, p_scr, width, rows, tq, post=None):
    for r0 in range(0, rows, _ATT_RB):
        rq = r0 % tq
        s = sc_scr[r0:r0 + _ATT_RB, 0:width] + bias_scr[rq:rq + _ATT_RB, 0:width]
        e = jnp.exp(s - jnp.max(s, axis=-1, keepdims=True))
        p = e / jnp.sum(e, axis=-1, keepdims=True)
        if post is not None:
            p = post(p, r0)
        p_scr[r0:r0 + _ATT_RB, 0:width] = p.astype(BF16)


def _nsa_attn_kernel(q_ref, qr_ref, gate_ref, kvc_ref, kvb_ref, o_ref,
                     rank_scr, sc_scr, bias_scr, p_scr, m_scr, l_scr, acc_scr, imp_scr, *, tq, tk):
    g = pl.program_id(1)
    t0 = pl.program_id(2) * tq
    rows = NSA_HPG * tq
    q4 = _stack_heads(q_ref[...])
    qr4 = _stack_heads(qr_ref[...])

    n_cmp = kvc_ref.shape[1]
    sc_scr[:, 0:n_cmp] = _dot_nt(q4, kvc_ref[0].astype(BF16))
    tpos_c = t0 + _iota((tq, n_cmp), 0)
    c_end = (_iota((tq, n_cmp), 1) + 1) * L_CMP - 1
    bias_scr[:, 0:n_cmp] = jnp.where(c_end <= tpos_c, 0.0, NEG)
    imp_scr[...] = jnp.zeros(imp_scr.shape, F32)

    def cmp_post(p, r0):
        rq = r0 % tq
        p = p * jnp.where(t0 + rq + _iota((_ATT_RB, 1), 0) >= L_CMP - 1, 1.0, 0.0)
        imp_scr[rq:rq + _ATT_RB, :] += p
        return p

    _softmax_blocks(sc_scr, bias_scr, p_scr, n_cmp, rows, tq, post=cmp_post)
    o_c = _dot(p_scr[:, 0:n_cmp], kvc_ref[1].astype(BF16))

    sel = jnp.concatenate(
        [_select_blocks(imp_scr[h0:h0 + LANES, :].T, t0 + h0, rank_scr).T for h0 in range(0, tq, LANES)], axis=0)
    sel = jnp.where(sel > 0.5, 1.0, 0.0).astype(BF16)

    m_scr[...] = jnp.full(m_scr.shape, NEG, F32)
    l_scr[...] = jnp.zeros(l_scr.shape, F32)
    acc_scr[...] = jnp.zeros(acc_scr.shape, F32)

    def sel_tile(kt, carry):
        k0 = pl.multiple_of(kt * tk, tk)
        sc_scr[:, 0:tk] = _dot_nt(qr4, kvb_ref[0, pl.ds(k0, tk), :])
        expand = jnp.where(_iota((LANES, tk), 0) == ((k0 + _iota((LANES, tk), 1)) >> _LOG_SEL), 1.0, 0.0).astype(BF16)
        picked = _dot(sel, expand)
        live = (picked > 0.5) & (k0 + _iota((tq, tk), 1) <= t0 + _iota((tq, tk), 0))
        bias_scr[:, 0:tk] = jnp.where(live, 0.0, NEG)
        for r0 in range(0, rows, _ATT_RB):
            rq = r0 % tq
            rs = slice(r0, r0 + _ATT_RB)
            s = sc_scr[rs, 0:tk] + bias_scr[rq:rq + _ATT_RB, 0:tk]
            m_old = m_scr[rs, :]
            m_new = jnp.maximum(m_old, jnp.max(s, axis=-1, keepdims=True))
            alpha = jnp.exp(m_old - m_new)
            pe = jnp.exp(s - m_new)
            l_scr[rs, :] = alpha * l_scr[rs, :] + jnp.sum(pe, axis=-1, keepdims=True)
            m_scr[rs, :] = m_new
            acc_scr[rs, :] = alpha * acc_scr[rs, :]
            p_scr[rs, 0:tk] = pe.astype(BF16)
        acc_scr[...] += _dot(p_scr[:, 0:tk], kvb_ref[1, pl.ds(k0, tk), :])
        return carry

    n_kt = (t0 + tq + tk - 1) // tk
    lax.fori_loop(0, n_kt, sel_tile, 0)
    o_s = acc_scr[...] / l_scr[...]

    wk = WINDOW + tq
    w0 = pl.multiple_of(jnp.maximum(t0 - WINDOW, 0), tq)
    sc_scr[:, 0:wk] = _dot_nt(qr4, kvb_ref[2, pl.ds(w0, wk), :])
    dist = (t0 + _iota((tq, wk), 0)) - (w0 + _iota((tq, wk), 1))
    bias_scr[:, 0:wk] = jnp.where((dist >= 0) & (dist <= WINDOW), 0.0, NEG)
    _softmax_blocks(sc_scr, bias_scr, p_scr, wk, rows, tq)
    o_w = _dot(p_scr[:, 0:wk], kvb_ref[3, pl.ds(w0, wk), :])

    gates = gate_ref[...]
    lane = _iota((tq, LANES), 1)
    outs = []
    for j in range(NSA_HPG):
        o = jnp.zeros((tq, NSA_DH), F32)
        for br, ob in enumerate((o_c, o_s, o_w)):
            col = br * NSA_HEADS + g * NSA_HPG + j
            gcol = jnp.sum(jnp.where(lane == col, gates, 0.0), axis=-1, keepdims=True)
            o = o + gcol * ob[j * tq:(j + 1) * tq]
        outs.append(o)
    o_ref[...] = jnp.concatenate(outs, axis=1).astype(BF16)


def _nsa_attn_prompt(q, qr, gates, kvc, kvb, n_batch, tq=256, tk=1024):
    n = q.shape[0]
    t_len = n // n_batch
    nq = t_len // tq
    tk = min(tk, t_len)
    assert kvc.shape[3] == LANES and t_len >= WINDOW + tq and t_len % tk == 0
    rows = NSA_HPG * tq
    width = max(tk, WINDOW + tq, LANES)
    qspec =pl.BlockSpec((tq, NSA_HPG * NSA_DH), lambda b, g, i: (b * nq + i, g))
    return pl.pallas_call(
        functools.partial(_nsa_attn_kernel, tq=tq, tk=tk),
        grid=(n_batch, NSA_KV, nq),
        in_specs=[qspec, qspec,
                  pl.BlockSpec((tq, LANES), lambda b, g, i: (b * nq + i, 0)),
                  pl.BlockSpec((None, 2, None, LANES, NSA_DH), lambda b, g, i: (b, 0, g, 0, 0)),
                  pl.BlockSpec((None, 4, None, t_len, NSA_DH), lambda b, g, i: (b, 0, g, 0, 0))],
        out_specs=qspec,
        out_shape=jax.ShapeDtypeStruct((n, NSA_Q), BF16),
        scratch_shapes=[pltpu.VMEM((LANES, LANES), F32),
                        pltpu.VMEM((rows, width), F32),
                        pltpu.VMEM((tq, width), F32),
                        pltpu.VMEM((rows, width), BF16),
                        pltpu.VMEM((rows, 1), F32), pltpu.VMEM((rows, 1), F32),
                        pltpu.VMEM((rows, NSA_DH), F32),
                        pltpu.VMEM((tq, LANES), F32)],
        compiler_params=_cparams("parallel", "parallel", "arbitrary"),
        name="nsa_attn_prompt",
    )(q, qr, gates, kvc, kvb)


def _matmul_residual_kernel(a_ref, w_ref, x_ref, o_ref):
    o_ref[...] = x_ref[...] + _dot(a_ref[...].astype(BF16), w_ref[...])


def _matmul_residual(a, w, x, tm):
    n, k = a.shape
    row = lambda i: (i, 0)
    return pl.pallas_call(
        _matmul_residual_kernel,
        grid=(n // tm,),
        in_specs=[pl.BlockSpec((tm, k), row), _resident(w.shape), pl.BlockSpec((tm, D_MODEL), row)],
        out_specs=pl.BlockSpec((tm, D_MODEL), row),
        out_shape=jax.ShapeDtypeStruct((n, D_MODEL), F32),
        compiler_params=_cparams("parallel"),
        name="matmul_residual",
    )(a, w, x)


def _nsa_prompt_layer(x, n_batch, norm_g, w_in, cmp_pe, cmp_w, w_out, tables):
    tm = min(512, x.shape[0] // n_batch)
    q, qr, cmp_rows, _, _, gates, kvb, cmp_t, sel_t, win_t = _nsa_proj(
        x, norm_g.reshape(1, D_MODEL), _nsa_w_in_padded(w_in), tables, n_batch, tm, seq_last=True)
    bd, pe2 = _cmp_weights(cmp_w, cmp_pe)
    kvc = _nsa_compress_prompt(cmp_rows, bd, pe2, n_batch)
    o = _nsa_attn_prompt(q, qr, gates, kvc, kvb, n_batch)
    y = _matmul_residual(o, w_out.astype(BF16), x, tm)
    return y, cmp_t, sel_t, win_t


_FFN_CW = MXU_DIM


def _ffn_kernel(x_ref, g_ref, win_ref, cw_ref, wout_ref, buf_ref, y_ref, st_ref, carry_scr, uv_scr, ug_scr,
                *, tm, tpb):
    ti = pl.program_id(0) % tpb
    x = x_ref[...]
    xn = _rmsnorm(x, g_ref[...]).astype(BF16)

    @pl.when(ti == 0)
    def _():
        carry_scr[SUBLANES - (FFN_CONV - 1):SUBLANES, :] = buf_ref[...]

    acc = jnp.zeros((tm, D_MODEL), F32)
    for c in range(D_FF // _FFN_CW):
        conv = []
        for part, scr in ((0, uv_scr), (1, ug_scr)):
            c0 = part * D_FF + c * _FFN_CW
            u = _dot(xn, win_ref[:, c0:c0 + _FFN_CW])
            scr[0:SUBLANES, :] = carry_scr[:, c0:c0 + _FFN_CW]
            scr[SUBLANES:SUBLANES + tm, :] = u
            carry_scr[:, c0:c0 + _FFN_CW] = u[tm - SUBLANES:tm, :]
            w = cw_ref[:, c0:c0 + _FFN_CW]
            conv.append(scr[SUBLANES - 2:SUBLANES - 2 + tm, :] * w[0:1] + scr[SUBLANES - 1:SUBLANES - 1 + tm, :] * w[1:2]
                        + u * w[2:3])
        act = conv[0] * jax.nn.silu(conv[1])
        acc = acc + _dot(act.astype(BF16), wout_ref[c * _FFN_CW:(c + 1) * _FFN_CW, :])
    y_ref[...] = x + acc

    @pl.when(ti == tpb - 1)
    def _():
        st_ref[...] = carry_scr[SUBLANES - (FFN_CONV - 1):SUBLANES, :]


def _ffn_prompt(x, n_batch, norm_g, w_in, conv_w, w_out, buf, tm=512):
    n = x.shape[0]
    tpb = n // n_batch // tm
    row = lambda i: (i, 0)
    st = pl.BlockSpec((None, FFN_CONV - 1, 2 * D_FF), lambda i: (i // tpb, 0, 0))
    return pl.pallas_call(
        functools.partial(_ffn_kernel, tm=tm, tpb=tpb),
        grid=(n // tm,),
        in_specs=[pl.BlockSpec((tm, D_MODEL), row), _resident((1, D_MODEL)), _resident(w_in.shape),
                  _resident(conv_w.shape), _resident(w_out.shape), st],
        out_specs=[pl.BlockSpec((tm, D_MODEL), row), st],
        out_shape=[jax.ShapeDtypeStruct((n, D_MODEL), F32),
                   jax.ShapeDtypeStruct((n_batch, FFN_CONV - 1, 2 * D_FF), F32)],
        scratch_shapes=[pltpu.VMEM((SUBLANES, 2 * D_FF), F32), pltpu.VMEM((SUBLANES + tm, _FFN_CW), F32),
                        pltpu.VMEM((SUBLANES + tm, _FFN_CW), F32)],
        compiler_params=_cparams("arbitrary"),
        name="ffn_prompt",
    )(x, norm_g.reshape(1, D_MODEL), w_in, conv_w, w_out, buf)


_GDN_W_COLS = GDN_CONV_CH + GDN_VW + LANES
_GDN_CW = MXU_DIM


def _softplus(x):
    return jnp.maximum(x, 0.0) + jnp.log1p(jnp.exp(-jnp.abs(x)))


def _gdn_w_in_padded(w_in):
    return jnp.pad(w_in, ((0, 0), (0, _GDN_W_COLS - w_in.shape[1]))).astype(BF16)


def _lane_row(v):
    return jnp.pad(v.astype(F32), (0, LANES - v.shape[0])).reshape(1, LANES)


def _gdn_post_conv(y, c0):
    y = jax.nn.silu(y)
    kind = c0 // GDN_QK
    if kind == 2:
        return y
    outs = []
    for hh in range(_GDN_CW // GDN_DK):
        seg = y[:, hh * GDN_DK:(hh + 1) * GDN_DK]
        seg = seg * lax.rsqrt(jnp.sum(seg * seg, axis=-1, keepdims=True) + EPS)
        outs.append(seg * GDN_DK ** -0.5 if kind == 0 else seg)
    return jnp.concatenate(outs, axis=1)


def _gdn_gates(ab, alog, dtb):
    lane = _iota(ab.shape, 1)
    gval = -jnp.exp(alog) * _softplus(ab + dtb)
    return jnp.where(lane < GDN_HEADS, gval, jax.nn.sigmoid(ab))


def _gdn_proj_kernel(x_ref, g_ref, w_ref, cw_ref, alog_ref, dtb_ref, buf_ref,
                     q_ref, k_ref, v_ref, gate_ref, gb_ref, st_ref, carry_scr, u_scr, *, tm, tpb):
    ti = pl.program_id(0) % tpb
    xn = _rmsnorm(x_ref[...], g_ref[...]).astype(BF16)
    keep = GDN_CONV - 1

    @pl.when(ti == 0)
    def _():
        carry_scr[SUBLANES - keep:SUBLANES, :] = buf_ref[...]

    outs = (q_ref, k_ref, v_ref)
    for c in range(GDN_CONV_CH // _GDN_CW):
        c0 = c * _GDN_CW
        u = _dot(xn, w_ref[:, c0:c0 + _GDN_CW])
        u_scr[0:SUBLANES, :] = carry_scr[:, c0:c0 + _GDN_CW]
        u_scr[SUBLANES:SUBLANES + tm, :] = u
        carry_scr[:, c0:c0 + _GDN_CW] = u[tm - SUBLANES:tm, :]
        w = cw_ref[:, c0:c0 + _GDN_CW]
        y = u_scr[SUBLANES - 3:SUBLANES - 3 + tm, :] * w[0:1]
        y = y + u_scr[SUBLANES - 2:SUBLANES - 2 + tm, :] * w[1:2]
        y = y + u_scr[SUBLANES - 1:SUBLANES - 1 + tm, :] * w[2:3]
        y = y + u * w[3:4]
        o_ref = outs[c0 // GDN_QK]
        o_ref[:, c0 % GDN_QK:c0 % GDN_QK + _GDN_CW] = _gdn_post_conv(y, c0)
    gate_ref[...] = _dot(xn, w_ref[:, GDN_CONV_CH:GDN_CONV_CH + GDN_VW])
    ab = _dot(xn, w_ref[:, GDN_CONV_CH + GDN_VW:_GDN_W_COLS])
    gb_ref[...] = _gdn_gates(ab, alog_ref[...], dtb_ref[...])

    @pl.when(ti == tpb - 1)
    def _():
        st_ref[...] = carry_scr[SUBLANES - keep:SUBLANES, :]


def _gdn_proj(x, n_batch, norm_g, w, conv_w, alog, dtb, buf, tm=256):
    n = x.shape[0]
    tpb = n // n_batch // tm
    row = lambda i: (i, 0)
    wide = pl.BlockSpec((tm, GDN_QK), row)
    st = pl.BlockSpec((None, GDN_CONV - 1, GDN_CONV_CH), lambda i: (i // tpb, 0, 0))
    f = jax.ShapeDtypeStruct((n, GDN_QK), F32)
    return pl.pallas_call(
        functools.partial(_gdn_proj_kernel, tm=tm, tpb=tpb),
        grid=(n // tm,),
        in_specs=[pl.BlockSpec((tm, D_MODEL), row), _resident((1, D_MODEL)), _resident(w.shape),
                  _resident(conv_w.shape), _resident((1, LANES)), _resident((1, LANES)), st],
        out_specs=[wide, wide, wide, wide, pl.BlockSpec((tm, LANES), row), st],
        out_shape=[f, f, f, f, jax.ShapeDtypeStruct((n, LANES), F32),
                   jax.ShapeDtypeStruct((n_batch, GDN_CONV - 1, GDN_CONV_CH), F32)],
        scratch_shapes=[pltpu.VMEM((SUBLANES, GDN_CONV_CH), F32), pltpu.VMEM((SUBLANES + tm, _GDN_CW), F32)],
        compiler_params=_cparams("arbitrary"),
        name="gdn_proj",
    )(x, norm_g.reshape(1, D_MODEL), w, conv_w, alog, dtb, buf)


_GDN_HG = MXU_DIM // GDN_CHUNK


def _gdn_block_diag(x):
    head = _iota(x.shape, 0) // GDN_CHUNK
    return jnp.concatenate([jnp.where(head == h, x, 0.0) for h in range(_GDN_HG)], axis=1)


def _gdn_group_prep(q, k, v, beta, gcum_col, gcum_row, g_last):
    n = q.shape[0]
    ii = _iota((n, n), 0)
    jj = _iota((n, n), 1)
    same = (ii // GDN_CHUNK) == (jj // GDN_CHUNK)
    decay = jnp.exp(jnp.where(same & (ii >= jj), gcum_col - gcum_row, NEG))
    kb = k * beta
    k16 = k.astype(BF16)
    low = _dot_nt(kb.astype(BF16), k16) * decay * jnp.where(ii > jj, 1.0, 0.0)
    attn = _dot_nt(q.astype(BF16), k16) * decay
    x = jnp.where(ii == jj, 1.0, 0.0) - low
    p = low
    for _ in range(GDN_CHUNK.bit_length() - 2):
        p16 = p.astype(BF16)
        p = _dot(p16, p16)
        x = x + _dot(x.astype(BF16), p.astype(BF16))
    rhs = jnp.concatenate([v * beta, kb * jnp.exp(gcum_col)], axis=1)
    sol = _dot(x.astype(BF16), rhs.astype(BF16))
    u_base, w_dec = sol[:, :GDN_DV], sol[:, GDN_DV:]
    return (u_base, _gdn_block_diag(w_dec).astype(BF16), _gdn_block_diag(q * jnp.exp(gcum_col)).astype(BF16),
            attn.astype(BF16), _gdn_block_diag(k * jnp.exp(g_last - gcum_col)).astype(BF16))


def _gdn_group_step(prep, s_prev, s_decay):
    u_base, w_dec_bd, q_bd, attn, k_dec_bd = prep
    s16 = s_prev.astype(BF16)
    u = u_base - _dot(w_dec_bd, s16)
    u16 = u.astype(BF16)
    o = _dot(q_bd, s16) + _dot(attn, u16)
    s_new = s_prev * s_decay + _dot_tn(k_dec_bd, u16)
    return o, s_new


def _gdn_chunk_kernel(q_ref, k_ref, v_ref, gb_ref, gate_ref, x_ref, s0_ref, ng_ref, wout_ref,
                      y_ref, s_ref, s_scr, o_scr, *, rt, tpb):
    ti = pl.program_id(1)
    n_groups = GDN_HEADS // _GDN_HG

    @pl.when(ti == 0)
    def _():
        s_scr[...] = s0_ref[...].reshape(n_groups, _GDN_HG * GDN_DK, GDN_DV)

    c = GDN_CHUNK
    tri = jnp.where(_iota((c, c), 0) >= _iota((c, c), 1), 1.0, 0.0)
    stack_rows = lambda ref, r0, h0: jnp.concatenate(
        [ref[r0:r0 + c, (h0 + h) * GDN_DK:(h0 + h + 1) * GDN_DK] for h in range(_GDN_HG)], axis=0)

    preps, decays = [], []
    for ci in range(rt // c):
        r0 = ci * c
        gb = gb_ref[r0:r0 + c, :]
        gcum = _dot_f32(tri, gb)
        gcum_t = gcum.T
        for gi in range(n_groups):
            h0 = gi * _GDN_HG
            heads = range(h0, h0 + _GDN_HG)
            col = lambda a: jnp.concatenate([a[:, h:h + 1] for h in heads], axis=0)
            g_last = jnp.concatenate([jnp.broadcast_to(gcum[c - 1:c, h:h + 1], (c, 1)) for h in heads], axis=0)
            gcum_row = jnp.concatenate([gcum_t[h:h + 1, :] for h in heads], axis=1)
            beta = jnp.concatenate([gb[:, GDN_HEADS + h:GDN_HEADS + h + 1] for h in heads], axis=0)
            preps.append(_gdn_group_prep(stack_rows(q_ref, r0, h0), stack_rows(k_ref, r0, h0),
                                         stack_rows(v_ref, r0, h0), beta, col(gcum), gcum_row, g_last))
            decays.append(jnp.concatenate(
                [jnp.broadcast_to(jnp.exp(gcum[c - 1:c, h:h + 1]), (GDN_DK, 1)) for h in heads], axis=0))

    for gi in range(n_groups):
        s = s_scr[gi]
        for ci in range(rt // c):
            o, s = _gdn_group_step(preps[ci * n_groups + gi], s, decays[ci * n_groups + gi])
            for h in range(_GDN_HG):
                o_scr[ci * c:(ci + 1) * c, (gi * _GDN_HG + h) * GDN_DV:(gi * _GDN_HG + h + 1) * GDN_DV] = (
                    o[h * c:(h + 1) * c])
        s_scr[gi] = s

    ng = ng_ref[...]
    outs = []
    for h in range(GDN_HEADS):
        cols = slice(h * GDN_DV, (h + 1) * GDN_DV)
        outs.append((_rmsnorm(o_scr[:, cols], ng) * jax.nn.silu(gate_ref[:, cols])).astype(BF16))
    y_ref[...] = x_ref[...] + _dot(jnp.concatenate(outs, axis=1), wout_ref[...])

    @pl.when(ti == tpb - 1)
    def _():
        s_ref[...] = s_scr[...].reshape(GDN_HEADS, GDN_DK, GDN_DV)


def _gdn_chunked(q, k, v, gb, gate, x, s0, norm_g, w_out, n_batch, rt=256):
    n = x.shape[0]
    tpb = n // n_batch // rt
    row = lambda b, i: (b * tpb + i, 0)
    wide = pl.BlockSpec((rt, GDN_QK), row)
    st = pl.BlockSpec((None, GDN_HEADS, GDN_DK, GDN_DV), lambda b, i: (b, 0, 0, 0))
    return pl.pallas_call(
        functools.partial(_gdn_chunk_kernel, rt=rt, tpb=tpb),
        grid=(n_batch, tpb),
        in_specs=[wide, wide, wide, pl.BlockSpec((rt, LANES), row), wide, wide, st,
                  _resident((1, GDN_DV)), _resident(w_out.shape)],
        out_specs=[wide, st],
        out_shape=[jax.ShapeDtypeStruct((n, D_MODEL), F32),
                   jax.ShapeDtypeStruct((n_batch, GDN_HEADS, GDN_DK, GDN_DV), F32)],
        scratch_shapes=[pltpu.VMEM((GDN_HEADS // _GDN_HG, _GDN_HG * GDN_DK, GDN_DV), F32),
                        pltpu.VMEM((rt, GDN_VW), F32)],
        compiler_params=_cparams("parallel", "arbitrary"),
        name="gdn_chunked",
    )(q, k, v, gb, gate, x, s0, norm_g.reshape(1, GDN_DV), w_out)


def _gdn_prompt_layer(x, n_batch, norm_g, s0, conv_buf, w_in, conv_w, a_log, dt_bias, head_norm_g, w_out):
    q, k, v, gate, gb, new_buf = _gdn_proj(x, n_batch, norm_g, _gdn_w_in_padded(w_in), conv_w,
                                           _lane_row(a_log), _lane_row(dt_bias), conv_buf)
    y, s_new = _gdn_chunked(q, k, v, gb, gate, x, s0, head_norm_g, w_out.astype(BF16), n_batch)
    return y, s_new, new_buf


_LRU_CW = MXU_DIM


def _lru_gates(xc, wga_ref, bga, wgx_ref, bgx, lam, n0):
    r_parts, i_parts = [], []
    for j in range(xc.shape[1] // LRU_BW):
        blk = xc[:, j * LRU_BW:(j + 1) * LRU_BW].astype(BF16)
        r_parts.append(_dot(blk, wga_ref[n0 + j]))
        i_parts.append(_dot(blk, wgx_ref[n0 + j]))
    cat = lambda ps: ps[0] if len(ps) == 1 else jnp.concatenate(ps, axis=1)
    r = jax.nn.sigmoid(cat(r_parts) + bga)
    i = jax.nn.sigmoid(cat(i_parts) + bgx)
    log_a = -LRU_C * r * _softplus(-lam)
    a = jnp.exp(log_a)
    b = jnp.sqrt(-jnp.tanh(log_a) * (a * a + 1.0)) * i * xc
    return a, b


def _lru_proj_kernel(x_ref, g_ref, w_ref, cw_ref, cb_ref, wga_ref, bga_ref, wgx_ref, bgx_ref, lam_ref, buf_ref,
                     gate_ref, a_ref, b_ref, st_ref, carry_scr, u_scr, *, tm, tpb):
    ti = pl.program_id(0) % tpb
    xn = _rmsnorm(x_ref[...], g_ref[...]).astype(BF16)
    keep = LRU_CONV - 1

    @pl.when(ti == 0)
    def _():
        carry_scr[SUBLANES - keep:SUBLANES, :] = buf_ref[...]

    gate_ref[...] = _dot(xn, w_ref[:, 0:D_RNN])
    for c in range(D_RNN // _LRU_CW):
        c0 = c * _LRU_CW
        cols = slice(c0, c0 + _LRU_CW)
        u = _dot(xn, w_ref[:, D_RNN + c0:D_RNN + c0 + _LRU_CW])
        u_scr[0:SUBLANES, :] = carry_scr[:, cols]
        u_scr[SUBLANES:SUBLANES + tm, :] = u
        carry_scr[:, cols] = u[tm - SUBLANES:tm, :]
        w = cw_ref[:, cols]
        xc = u_scr[SUBLANES - 3:SUBLANES - 3 + tm, :] * w[0:1]
        xc = xc + u_scr[SUBLANES - 2:SUBLANES - 2 + tm, :] * w[1:2]
        xc = xc + u_scr[SUBLANES - 1:SUBLANES - 1 + tm, :] * w[2:3]
        xc = xc + u * w[3:4] + cb_ref[:, cols]
        a, b = _lru_gates(xc, wga_ref, bga_ref[:, cols], wgx_ref, bgx_ref[:, cols], lam_ref[:, cols],
                          c0 // LRU_BW)
        a_ref[:, cols] = a
        b_ref[:, cols] = b

    @pl.when(ti == tpb - 1)
    def _():
        st_ref[...] = carry_scr[SUBLANES - keep:SUBLANES, :]


def _lru_proj(x, n_batch, norm_g, w, conv_w, conv_b, w_ga, b_ga, w_gx, b_gx, lam, buf, tm=256):
    n = x.shape[0]
    tpb = n // n_batch // tm
    row = lambda i: (i, 0)
    wide = pl.BlockSpec((tm, D_RNN), row)
    vec = _resident((1, D_RNN))
    st = pl.BlockSpec((None, LRU_CONV - 1, D_RNN), lambda i: (i // tpb, 0, 0))
    f = jax.ShapeDtypeStruct((n, D_RNN), F32)
    r1 = lambda v: v.reshape(1, D_RNN)
    return pl.pallas_call(
        functools.partial(_lru_proj_kernel, tm=tm, tpb=tpb),
        grid=(n // tm,),
        in_specs=[pl.BlockSpec((tm, D_MODEL), row), _resident((1, D_MODEL)), _resident(w.shape),
                  _resident(conv_w.shape), vec, _resident(w_ga.shape), vec, _resident(w_gx.shape), vec, vec, st],
        out_specs=[wide, wide, wide, st],
        out_shape=[f, f, f, jax.ShapeDtypeStruct((n_batch, LRU_CONV - 1, D_RNN), F32)],
        scratch_shapes=[pltpu.VMEM((SUBLANES, D_RNN), F32), pltpu.VMEM((SUBLANES + tm, _LRU_CW), F32)],
        compiler_params=_cparams("arbitrary"),
        name="lru_proj",
    )(x, norm_g.reshape(1, D_MODEL), w, conv_w, r1(conv_b), w_ga, r1(b_ga), w_gx, r1(b_gx), r1(lam), buf)


def _lru_scan_kernel(a_ref, b_ref, gate_ref, x_ref, h0_ref, wout_ref, y_ref, hl_ref, h_scr, hs_scr, *, rt, tpb):
    ti = pl.program_id(1)

    @pl.when(ti == 0)
    def _():
        h_scr[...] = h0_ref[...]

    def step(t, h):
        h = a_ref[pl.ds(t, 1), :] * h + b_ref[pl.ds(t, 1), :]
        hs_scr[pl.ds(t, 1), :] = h
        return h

    h = lax.fori_loop(0, rt, step, h_scr[...], unroll=8)
    h_scr[...] = h
    y = (hs_scr[...] * jax.nn.gelu(gate_ref[...])).astype(BF16)
    y_ref[...] = x_ref[...] + _dot(y, wout_ref[...])

    @pl.when(ti == tpb - 1)
    def _():
        hl_ref[...] = h


def _lru_scan(a, b, gate, x, h0, w_out, n_batch, rt=256):
    n = x.shape[0]
    tpb = n // n_batch // rt
    wide = pl.BlockSpec((rt, D_RNN), lambda bb, i: (bb * tpb + i, 0))
    st = pl.BlockSpec((None, 1, D_RNN), lambda bb, i: (bb, 0, 0))
    y, hl = pl.pallas_call(
        functools.partial(_lru_scan_kernel, rt=rt, tpb=tpb),
        grid=(n_batch, tpb),
        in_specs=[wide, wide, wide, wide, st, _resident(w_out.shape)],
        out_specs=[wide, st],
        out_shape=[jax.ShapeDtypeStruct((n, D_MODEL), F32), jax.ShapeDtypeStruct((n_batch, 1, D_RNN), F32)],
        scratch_shapes=[pltpu.VMEM((1, D_RNN), F32), pltpu.VMEM((rt, D_RNN), F32)],
        compiler_params=_cparams("parallel", "arbitrary"),
        name="lru_scan",
    )(a, b, gate, x, h0.reshape(n_batch, 1, D_RNN), w_out)
    return y, hl.reshape(n_batch, D_RNN)


def _lru_prompt_layer(x, n_batch, norm_g, h0, conv_buf, w_in, conv_w, conv_b, w_ga, b_ga, w_gx, b_gx, lam, w_out):
    gate, a, b, new_buf = _lru_proj(x, n_batch, norm_g, w_in.astype(BF16), conv_w, conv_b, w_ga.astype(BF16), b_ga,
                                    w_gx.astype(BF16), b_gx, lam, conv_buf)
    y, h_last = _lru_scan(a, b, gate, x, h0, w_out.astype(BF16), n_batch)
    return y, h_last, new_buf


def _final_norm_kernel(x_ref, g_ref, o_ref):
    o_ref[...] = _rmsnorm(x_ref[...], g_ref[...])


def _final_norm(x, g, tm):
    n = x.shape[0]
    row = lambda i: (i, 0)
    return pl.pallas_call(
        _final_norm_kernel,
        grid=(n // tm,),
        in_specs=[pl.BlockSpec((tm, D_MODEL), row), _resident((1, D_MODEL))],
        out_specs=pl.BlockSpec((tm, D_MODEL), row),
        out_shape=jax.ShapeDtypeStruct((n, D_MODEL), F32),
        compiler_params=_cparams("parallel"),
        name="final_norm",
    )(x, g.reshape(1, D_MODEL))


def _norm_matmul_small_kernel(x_ref, g_ref, w_ref, o_ref):
    xn = _rmsnorm(x_ref[...], g_ref[...]).astype(BF16)
    o_ref[...] = _dot(xn, w_ref[...])


def _norm_matmul_small(x, g, w, n_col_tiles):
    n, k = x.shape[0], w.shape[1]
    tn = k // n_col_tiles
    return pl.pallas_call(
        _norm_matmul_small_kernel,
        grid=(n_col_tiles,),
        in_specs=[_resident((n, D_MODEL)), _resident((1, D_MODEL)), pl.BlockSpec((D_MODEL, tn), lambda j: (0, j))],
        out_specs=pl.BlockSpec((n, tn), lambda j: (0, j)),
        out_shape=jax.ShapeDtypeStruct((n, k), F32),
        compiler_params=_cparams("parallel"),
        name="norm_matmul_small",
    )(x, g.reshape(1, D_MODEL), w)


_CMP_PAGES_PER_STEP = 8
_CMP_BLK_PER_PAGE = PAGE_SIZE // L_CMP


def _nsa_compress_sample_kernel(pt_ref, *refs, n_pages, n_blk, n_out):
    pages = refs[:_CMP_PAGES_PER_STEP]
    new_ref, bd_ref, pe_ref, o_ref, rows_scr = refs[_CMP_PAGES_PER_STEP:]
    b = pl.program_id(0)
    pg = pl.program_id(1)
    past_blk = n_pages * _CMP_BLK_PER_PAGE
    pair_rows = 2 * PAGE_SIZE
    out_row = _iota((pair_rows, pair_rows), 0)
    src_row = _iota((pair_rows, pair_rows), 1)
    blk_per_pair = 2 * _CMP_BLK_PER_PAGE
    perm = jnp.where(src_row == (out_row % blk_per_pair) * L_CMP + out_row // blk_per_pair, 1.0, 0.0).astype(BF16)
    for r in range(0, _CMP_PAGES_PER_STEP, 2):
        c0 = pl.multiple_of((pg * _CMP_PAGES_PER_STEP + r) * _CMP_BLK_PER_PAGE, blk_per_pair)
        for kv in range(2):
            both = jnp.concatenate(
                [pages[r + i][kv].reshape(NSA_KVW, PAGE_SIZE) for i in range(2)], axis=1)
            moved = _dot_nt(perm, both.astype(BF16))
            for l in range(L_CMP):
                rows_scr[kv, l, pl.ds(c0, blk_per_pair), :] = moved[l * blk_per_pair:(l + 1) * blk_per_pair]

    @pl.when(pg == pl.num_programs(1) - 1)
    def _():
        new = new_ref[pl.ds(b, 1), :]
        n_tail = n_blk - past_blk
        first = _iota((n_tail, NSA_KVW), 0) == 0
        for kv in range(2):
            for l in range(L_CMP):
                tail = jnp.where(first, new[:, kv * NSA_KVW:(kv + 1) * NSA_KVW], 0.0) if l == 0 else (
                    jnp.zeros((n_tail, NSA_KVW), F32))
                rows_scr[kv, l, past_blk:n_blk, :] = tail
        o_ref[...] = jnp.zeros(o_ref.shape, F32)

        def load_rows(kv, l):
            return rows_scr[kv, l]

        _compress_planes(load_rows, bd_ref, pe_ref, o_ref.at[:, :, 0:n_blk, :], n_blk)


def _rows_last(cache):
    nd = cache.ndim
    return jnp.transpose(cache, tuple(range(nd - 4)) + (nd - 3, nd - 2, nd - 1, nd - 4))


def _nsa_compress_sample(cache_t, layer, page_table, new_rows, bd, pe2):
    n_b, n_pages = page_table.shape
    past = n_pages * PAGE_SIZE
    n_blk = -(-(past + 1) // L_SEL) * L_SEL // L_CMP
    n_blk = -(-n_blk // SUBLANES) * SUBLANES
    n_out = -(-n_blk // LANES) * LANES
    cache4 = cache_t

    def page_spec(r):
        return pl.BlockSpec((None, None, 2, NSA_KV, NSA_DH, PAGE_SIZE),
                            lambda b, pg, pt: (layer, pt[b, pg * _CMP_PAGES_PER_STEP + r], 0, 0, 0, 0))

    grid_spec = pltpu.PrefetchScalarGridSpec(
        num_scalar_prefetch=1,
        grid=(n_b, n_pages // _CMP_PAGES_PER_STEP),
        in_specs=[page_spec(r) for r in range(_CMP_PAGES_PER_STEP)]
        + [pl.BlockSpec(new_rows.shape, lambda b, pg, pt: (0, 0)),
           pl.BlockSpec(bd.shape, lambda b, pg, pt: (0, 0, 0, 0)),
           pl.BlockSpec(pe2.shape, lambda b, pg, pt: (0, 0, 0))],
        out_specs=pl.BlockSpec((None, 2, NSA_KV, n_out, NSA_DH), lambda b, pg, pt: (b, 0, 0, 0, 0)),
        scratch_shapes=[pltpu.VMEM((2, L_CMP, n_blk, NSA_KVW), F32)])
    return pl.pallas_call(
        functools.partial(_nsa_compress_sample_kernel, n_pages=n_pages, n_blk=n_blk, n_out=n_out),
        grid_spec=grid_spec,
        out_shape=jax.ShapeDtypeStruct((n_b, 2, NSA_KV, n_out, NSA_DH), F32),
        compiler_params=_cparams("parallel", "arbitrary"),
        name="nsa_compress_sample",
    )(page_table, *([cache4] * _CMP_PAGES_PER_STEP), new_rows, bd, pe2)


def _pad_rows8(x):
    return jnp.concatenate([x, jnp.zeros((SUBLANES - x.shape[0], x.shape[1]), x.dtype)], axis=0)


_TOPK_ROWS_PER_STEP = 8


def _nsa_cmp_topk_sample_kernel(q_ref, kvc_ref, oc_ref, idx_ref, *, qpos, n_sel):
    nb = q_ref.shape[0]
    n_cmp = kvc_ref.shape[3]
    n_lane = 2 * LANES
    c_end = (_iota((SUBLANES, n_cmp), 1) + 1) * L_CMP - 1
    valid = c_end <= qpos
    any_valid = 1.0 if qpos >= L_CMP - 1 else 0.0
    pair_mat = jnp.where((_iota((n_cmp, n_lane), 0) >> 1) == _iota((n_cmp, n_lane), 1), 1.0, 0.0)
    imps = []
    for i in range(nb):
        for g in range(NSA_KV):
            q8 = _pad_rows8(q_ref[i, g * NSA_HPG:(g + 1) * NSA_HPG, :])
            s = _dot_nt(q8, kvc_ref[i, 0, g].astype(BF16))
            p = _softmax_rows(jnp.where(valid, s, NEG)) * any_valid
            o_c = _dot(p.astype(BF16), kvc_ref[i, 1, g].astype(BF16))
            oc_ref[i, g * NSA_HPG:(g + 1) * NSA_HPG, :] = o_c[0:NSA_HPG]
            imps.append(p[0:1] + p[1:2] + p[2:3] + p[3:4])
        imps.append(jnp.zeros((SUBLANES - NSA_KV, n_cmp), F32))
    rows = nb * SUBLANES
    imp = jnp.concatenate(imps, axis=0)
    pair = _dot_f32(imp, pair_mat)
    blk = _iota((rows, n_lane), 1)
    forced = (blk == 0) | (blk == qpos // L_SEL)
    avail = blk * L_SEL <= qpos
    val = jnp.where(forced, FORCE, jnp.where(avail, pair, -1.0))
    val = jnp.where(blk < n_sel, val, -2.0)
    out = jnp.zeros((rows, LANES), I32)
    out_lane = _iota((rows, LANES), 1)
    for r in range(N_SEL):
        m = jnp.max(val, axis=-1, keepdims=True)
        first = jnp.min(jnp.where(val == m, blk, n_lane), axis=-1, keepdims=True)
        out = jnp.where(out_lane == r, first, out)
        val = jnp.where(blk == first, -3.0, val)
    idx_ref[...] = out.reshape(nb, SUBLANES, LANES)


def _nsa_cmp_topk_sample(q3, kvc, qpos, n_sel):
    n_b = q3.shape[0]
    n_cmp = kvc.shape[3]
    nb = _TOPK_ROWS_PER_STEP if n_b % _TOPK_ROWS_PER_STEP == 0 else 1
    return pl.pallas_call(
        functools.partial(_nsa_cmp_topk_sample_kernel, qpos=qpos, n_sel=n_sel),
        grid=(n_b // nb,),
        in_specs=[pl.BlockSpec((nb, NSA_HEADS, NSA_DH), lambda b: (b, 0, 0)),
                  pl.BlockSpec((nb, 2, NSA_KV, n_cmp, NSA_DH), lambda b: (b, 0, 0, 0, 0))],
        out_specs=[pl.BlockSpec((nb, NSA_HEADS, NSA_DH), lambda b: (b, 0, 0)),
                   pl.BlockSpec((nb, SUBLANES, LANES), lambda b: (b, 0, 0))],
        out_shape=[jax.ShapeDtypeStruct((n_b, NSA_HEADS, NSA_DH), F32),
                   jax.ShapeDtypeStruct((n_b, SUBLANES, LANES), I32)],
        compiler_params=_cparams("parallel"),
        name="nsa_cmp_topk_sample",
    )(q3, kvc)


def _nsa_gather_sel_kernel(pt_ref, idx_ref, *refs):
    blocks = refs[:NSA_KV * N_SEL]
    o_ref = refs[NSA_KV * N_SEL]
    for g in range(NSA_KV):
        for kv in range(2):
            pages = [blocks[g * N_SEL + r][kv] for r in range(N_SEL)]
            o_ref[g, kv] = jnp.concatenate(pages, axis=1).astype(BF16)


def _nsa_gather_sel(cache_t, layer, page_table, idx):
    n_b, n_pages = page_table.shape
    past = n_pages * PAGE_SIZE
    halves = PAGE_SIZE // L_SEL
    last_blk = past // L_SEL - 1

    def blk_spec(n):
        g = n // N_SEL

        def index(b, pt, ix):
            s = jnp.clip(ix[b, n], 0, last_blk)
            return (layer, pt[b, s // halves], 0, g, 0, 0)
        return pl.BlockSpec((None, None, 2, None, NSA_DH, PAGE_SIZE), index)

    grid_spec = pltpu.PrefetchScalarGridSpec(
        num_scalar_prefetch=2,
        grid=(n_b,),
        in_specs=[blk_spec(n) for n in range(NSA_KV * N_SEL)],
        out_specs=pl.BlockSpec((None, NSA_KV, 2, NSA_DH, N_SEL * PAGE_SIZE), lambda b, pt, ix: (b, 0, 0, 0, 0)))
    return pl.pallas_call(
        _nsa_gather_sel_kernel,
        grid_spec=grid_spec,
        out_shape=jax.ShapeDtypeStruct((n_b, NSA_KV, 2, NSA_DH, N_SEL * PAGE_SIZE), BF16),
        name="nsa_gather_sel",
    )(page_table, idx, *([cache_t] * (NSA_KV * N_SEL)))


def _attend_with_new(q8, k_t, v_t, live, k_new, v_new, new_live):
    s = jnp.where(live > 0.5, _dot(q8, k_t), NEG)
    k_new16 = k_new.astype(BF16).astype(F32)
    s_new = jnp.sum(q8.astype(F32) * k_new16, axis=-1, keepdims=True)
    s_new = jnp.where(new_live > 0.5, s_new, NEG)
    m = jnp.maximum(jnp.max(s, axis=-1, keepdims=True), s_new)
    e = jnp.exp(s - m)
    e_new = jnp.exp(s_new - m)
    inv = 1.0 / (jnp.sum(e, axis=-1, keepdims=True) + e_new)
    p16 = (e * inv).astype(BF16)
    p_new16 = (e_new * inv).astype(BF16).astype(F32)
    return _dot_nt(p16, v_t) + p_new16 * v_new.astype(BF16).astype(F32)


def _row_to_col(row):
    return jnp.broadcast_to(row, (SUBLANES, row.shape[1])).T[:, 0:1]


def _nsa_sel_win_sample_kernel(idx_ref, qr_ref, oc_ref, gate_ref, selnew_ref, winnew_ref, win_ref, kv_ref,
                               o_ref, wout_ref, *, qpos, past):
    gates = gate_ref[...]
    glane = _iota((1, LANES), 1)
    n_key = N_SEL * PAGE_SIZE
    lane_k = _iota((SUBLANES, n_key), 1)
    log_page = PAGE_SIZE.bit_length() - 1
    wb = win_ref.shape[3]
    wlane = _iota((SUBLANES, wb), 1)
    kwpos = past - wb + wlane
    dist = qpos - kwpos
    wlive = jnp.where((dist >= 0) & (dist <= WINDOW) & (kwpos >= 0), 1.0, 0.0)
    new_in_window = jnp.full((1, 1), 1.0 if 0 <= qpos - past <= WINDOW else 0.0, F32)
    expand = jnp.where(_iota((LANES, n_key), 0) == (_iota((LANES, n_key), 1) >> log_page), 1.0, 0.0)
    idx_f = idx_ref[...].astype(F32)
    blk_of_lane = _dot_f32(jnp.where(_iota((SUBLANES, LANES), 1) < N_SEL, idx_f, 0.0), expand)
    sel_new = selnew_ref[...]
    win_new = winnew_ref[...]
    last_lane = _iota((NSA_DH, wb), 1) == wb - 1
    for g in range(NSA_KV):
        kc = slice(g * NSA_DH, (g + 1) * NSA_DH)
        vc = slice(NSA_KVW + g * NSA_DH, NSA_KVW + (g + 1) * NSA_DH)
        q8 = _pad_rows8(qr_ref[g * NSA_HPG:(g + 1) * NSA_HPG, :])
        blk_g = blk_of_lane[g:g + 1, :]
        new_blk = float(qpos // L_SEL)
        new_selected = jnp.max(jnp.where(blk_g == new_blk, 1.0, 0.0), axis=-1, keepdims=True)
        halves = PAGE_SIZE // L_SEL
        page_g = jnp.floor(jnp.minimum(blk_g, float(past // L_SEL - 1)) * (1.0 / halves))
        kpos = page_g * float(PAGE_SIZE) + (lane_k & (PAGE_SIZE - 1)).astype(F32)
        in_block = jnp.floor(kpos * (1.0 / L_SEL)) == blk_g
        live = jnp.where(in_block & (kpos < float(past)) & (kpos <= float(qpos)), 1.0, 0.0)
        o_s = _attend_with_new(q8, kv_ref[g, 0], kv_ref[g, 1], live, sel_new[:, kc], sel_new[:, vc], new_selected)
        kw_t = win_ref[0, g]
        vw_t = win_ref[1, g]
        o_w = _attend_with_new(q8, kw_t.astype(BF16), vw_t.astype(BF16), wlive, win_new[:, kc], win_new[:, vc],
                               new_in_window)
        wout_ref[0, g] = jnp.where(last_lane, _row_to_col(win_new[:, kc]), pltpu.roll(kw_t, wb - 1, 1))
        wout_ref[1, g] = jnp.where(last_lane, _row_to_col(win_new[:, vc]), pltpu.roll(vw_t, wb - 1, 1))
        o_c = oc_ref[g * NSA_HPG:(g + 1) * NSA_HPG, :]
        rows = []
        for j in range(NSA_HPG):
            acc = jnp.zeros((1, NSA_DH), F32)
            for br, ob in enumerate((o_c, o_s, o_w)):
                col = br * NSA_HEADS + g * NSA_HPG + j
                gcol = jnp.sum(jnp.where(glane == col, gates, 0.0), axis=-1, keepdims=True)
                acc = acc + gcol * ob[j:j + 1, :]
            rows.append(acc)
        o_ref[g * NSA_HPG:(g + 1) * NSA_HPG, :] = jnp.concatenate(rows, axis=0)


def _nsa_sel_win_sample(kv_sel, idx8, qr3, oc3, gates, sel_new, win_new, win_t, qpos, past):
    n_b = qr3.shape[0]
    wb = win_t.shape[-1]
    b3 = lambda b: (b, 0, 0)
    b5 = lambda b: (b, 0, 0, 0, 0)
    wspec = pl.BlockSpec((None, 2, NSA_KV, NSA_DH, wb), b5)
    return pl.pallas_call(
        functools.partial(_nsa_sel_win_sample_kernel, qpos=qpos, past=past),
        grid=(n_b,),
        in_specs=[pl.BlockSpec((None, SUBLANES, LANES), b3),
                  pl.BlockSpec((None, NSA_HEADS, NSA_DH), b3), pl.BlockSpec((None, NSA_HEADS, NSA_DH), b3),
                  pl.BlockSpec((None, 1, LANES), b3), pl.BlockSpec((None, 1, 2 * NSA_KVW), b3),
                  pl.BlockSpec((None, 1, 2 * NSA_KVW), b3), wspec,
                  pl.BlockSpec((None, NSA_KV, 2, NSA_DH, N_SEL * PAGE_SIZE), b5)],
        out_specs=[pl.BlockSpec((None, NSA_HEADS, NSA_DH), b3), wspec],
        out_shape=[jax.ShapeDtypeStruct((n_b, NSA_HEADS, NSA_DH), F32), jax.ShapeDtypeStruct(win_t.shape, F32)],
        compiler_params=_cparams("parallel"),
        name="nsa_sel_win_sample",
    )(idx8, qr3, oc3, gates.reshape(n_b, 1, LANES), sel_new.reshape(n_b, 1, -1),
      win_new.reshape(n_b, 1, -1), win_t, kv_sel)


def _nsa_sample_layer(x, layer, norm_g, w_in, cmp_pe, cmp_w, w_out, cache_cmp, cache_sel, win_state, page_table,
                      tables):
    n_b = x.shape[0]
    past = page_table.shape[1] * PAGE_SIZE
    qpos = past
    q, qr, cmp_rows, sel_rows, win_rows, gates, _ = _nsa_proj(
        x, norm_g.reshape(1, D_MODEL), _nsa_w_in_padded(w_in), tables, 1, n_b)
    bd, pe2 = _cmp_weights(cmp_w, cmp_pe)
    kvc = _nsa_compress_sample(cache_cmp, layer, page_table, cmp_rows, bd, pe2)
    n_sel = -(-(past + 1) // L_SEL)
    oc3, idx8 = _nsa_cmp_topk_sample(q.reshape(n_b, NSA_HEADS, NSA_DH), kvc, qpos, n_sel)
    idx = idx8[:, :NSA_KV, :N_SEL].reshape(n_b, NSA_KV * N_SEL)
    kv_sel = _nsa_gather_sel(cache_sel, layer, page_table, idx)
    o3, win_out = _nsa_sel_win_sample(kv_sel, idx8, qr.reshape(n_b, NSA_HEADS, NSA_DH), oc3, gates,
                                      sel_rows, win_rows, win_state, qpos, past)
    y = _matmul_residual(o3.reshape(n_b, NSA_Q), w_out.astype(BF16), x, n_b)
    return y, cmp_rows, sel_rows, win_out


def _gdn_sample_pre_kernel(z_ref, b0_ref, b1_ref, b2_ref, cw_ref, alog_ref, dtb_ref, q_ref, k_ref, v_ref, gb_ref):
    outs = (q_ref, k_ref, v_ref)
    for c in range(GDN_CONV_CH // _GDN_CW):
        c0 = c * _GDN_CW
        cols = slice(c0, c0 + _GDN_CW)
        w = cw_ref[:, cols]
        y = b0_ref[:, cols] * w[0:1] + b1_ref[:, cols] * w[1:2] + b2_ref[:, cols] * w[2:3] + z_ref[:, cols] * w[3:4]
        o_ref = outs[c0 // GDN_QK]
        o_ref[:, c0 % GDN_QK:c0 % GDN_QK + _GDN_CW] = _gdn_post_conv(y, c0)
    gb_ref[...] = _gdn_gates(z_ref[:, GDN_CONV_CH + GDN_VW:_GDN_W_COLS], alog_ref[...], dtb_ref[...])


def _gdn_sample_pre(z, bufs, conv_w, alog, dtb):
    n = z.shape[0]
    f = jax.ShapeDtypeStruct((n, GDN_QK), F32)
    return pl.pallas_call(
        _gdn_sample_pre_kernel,
        out_shape=[f, f, f, jax.ShapeDtypeStruct((n, LANES), F32)],
        compiler_params=pltpu.CompilerParams(vmem_limit_bytes=VMEM_LIMIT),
        name="gdn_sample_pre",
    )(z, *bufs, conv_w, alog, dtb)


def _gdn_sample_state_kernel(s0_ref, qt_ref, kt_ref, v_ref, gb_ref, gate_ref, ng_ref, s_ref, o_ref):
    gb = gb_ref[...]
    lane = _iota((1, LANES), 1)
    ng = ng_ref[...]
    for h in range(GDN_HEADS):
        cols = slice(h * GDN_DV, (h + 1) * GDN_DV)
        g_h = jnp.sum(jnp.where(lane == h, gb, 0.0), axis=-1, keepdims=True)
        beta = jnp.sum(jnp.where(lane == GDN_HEADS + h, gb, 0.0), axis=-1, keepdims=True)
        eg = jnp.exp(g_h)
        s_prev = s0_ref[h]
        k_col = kt_ref[:, h:h + 1]
        q_col = qt_ref[:, h:h + 1]
        v_row = v_ref[:, cols]
        u = v_row * beta - jnp.sum(s_prev * (k_col * beta * eg), axis=0, keepdims=True)
        attn = jnp.sum(q_col * k_col, axis=0, keepdims=True)
        o = jnp.sum(s_prev * (q_col * eg), axis=0, keepdims=True) + attn * u
        s_ref[h] = s_prev * eg + k_col * u
        o_ref[:, cols] = (_rmsnorm(o, ng) * jax.nn.silu(gate_ref[:, cols])).astype(BF16)


def _gdn_sample_state(s0, q_t, k_t, v, gb, gate, head_norm_g):
    n_b = s0.shape[0]
    st = pl.BlockSpec((None, GDN_HEADS, GDN_DK, GDN_DV), lambda b: (b, 0, 0, 0))
    col = pl.BlockSpec((None, GDN_DK, GDN_HEADS), lambda b: (b, 0, 0))
    wide = pl.BlockSpec((None, 1, GDN_VW), lambda b: (b, 0, 0))
    return pl.pallas_call(
        _gdn_sample_state_kernel,
        grid=(n_b,),
        in_specs=[st, col, col, wide, pl.BlockSpec((None, 1, LANES), lambda b: (b, 0, 0)), wide,
                  _resident((1, GDN_DV))],
        out_specs=[st, wide],
        out_shape=[jax.ShapeDtypeStruct(s0.shape, F32), jax.ShapeDtypeStruct((n_b, 1, GDN_VW), BF16)],
        compiler_params=_cparams("parallel"),
        name="gdn_sample_state",
    )(s0, q_t, k_t, v.reshape(n_b, 1, GDN_VW), gb.reshape(n_b, 1, LANES), gate.reshape(n_b, 1, GDN_VW),
      head_norm_g.reshape(1, GDN_DV))


def _gdn_sample_layer(x, norm_g, s0, conv_buf, w_in, conv_w, a_log, dt_bias, head_norm_g, w_out):
    n_b = x.shape[0]
    z = _norm_matmul_small(x, norm_g, _gdn_w_in_padded(w_in), 3)
    bufs = [conv_buf[:, j, :] for j in range(GDN_CONV - 1)]
    q, k, v, gb = _gdn_sample_pre(z, bufs, conv_w, _lane_row(a_log), _lane_row(dt_bias))
    to_cols = lambda t: t.reshape(n_b, GDN_HEADS, GDN_DK).transpose(0, 2, 1)
    gate = z[:, GDN_CONV_CH:GDN_CONV_CH + GDN_VW]
    s_new, o = _gdn_sample_state(s0, to_cols(q), to_cols(k), v, gb, gate, head_norm_g)
    y = _matmul_residual(o.reshape(n_b, GDN_VW), w_out.astype(BF16), x, n_b)
    new_buf = jnp.concatenate([conv_buf[:, 1:, :], z[:, None, :GDN_CONV_CH]], axis=1)
    return y, s_new, new_buf


def _lru_sample_kernel(z_ref, b0_ref, b1_ref, b2_ref, cw_ref, cb_ref, wga_ref, bga_ref, wgx_ref, bgx_ref, lam_ref,
                       h0_ref, h_ref, y_ref):
    for c in range(D_RNN // _LRU_CW):
        c0 = c * _LRU_CW
        cols = slice(c0, c0 + _LRU_CW)
        w = cw_ref[:, cols]
        xc = (b0_ref[:, cols] * w[0:1] + b1_ref[:, cols] * w[1:2] + b2_ref[:, cols] * w[2:3]
              + z_ref[:, D_RNN + c0:D_RNN + c0 + _LRU_CW] * w[3:4] + cb_ref[:, cols])
        a, b = _lru_gates(xc, wga_ref, bga_ref[:, cols], wgx_ref, bgx_ref[:, cols], lam_ref[:, cols], c0 // LRU_BW)
        h = a * h0_ref[:, cols] + b
        h_ref[:, cols] = h
        y_ref[:, cols] = (h * jax.nn.gelu(z_ref[:, cols])).astype(BF16)


def _lru_sample_layer(x, norm_g, h0, conv_buf, w_in, conv_w, conv_b, w_ga, b_ga, w_gx, b_gx, lam, w_out):
    n_b = x.shape[0]
    z = _norm_matmul_small(x, norm_g, w_in.astype(BF16), 2)
    r1 = lambda v: v.reshape(1, D_RNN)
    bufs = [conv_buf[:, j, :] for j in range(LRU_CONV - 1)]
    h, y = pl.pallas_call(
        _lru_sample_kernel,
        out_shape=[jax.ShapeDtypeStruct((n_b, D_RNN), F32), jax.ShapeDtypeStruct((n_b, D_RNN), BF16)],
        compiler_params=pltpu.CompilerParams(vmem_limit_bytes=VMEM_LIMIT),
        name="lru_sample",
    )(z, *bufs, conv_w, r1(conv_b), w_ga.astype(BF16), r1(b_ga), w_gx.astype(BF16), r1(b_gx), r1(lam), h0)
    y = _matmul_residual(y, w_out.astype(BF16), x, n_b)
    new_buf = jnp.concatenate([conv_buf[:, 1:, :], z[:, None, D_RNN:]], axis=1)
    return y, h, new_buf


def _ffn_sample_kernel(x_ref, g_ref, wv_ref, wg_ref, cwv_ref, cwg_ref, b0v_ref, b0g_ref, b1v_ref, b1g_ref, wout_ref,
                       y_ref, uv_ref, ug_ref, xn_scr):
    c = pl.program_id(0)

    @pl.when(c == 0)
    def _():
        x = x_ref[...]
        xn_scr[...] = _rmsnorm(x, g_ref[...]).astype(BF16)
        y_ref[...] = x

    xn = xn_scr[...]
    uv = _dot(xn, wv_ref[...])
    ug = _dot(xn, wg_ref[...])
    uv_ref[...] = uv
    ug_ref[...] = ug
    cwv, cwg = cwv_ref[...], cwg_ref[...]
    val = b0v_ref[...] * cwv[0:1] + b1v_ref[...] * cwv[1:2] + uv * cwv[2:3]
    gt = b0g_ref[...] * cwg[0:1] + b1g_ref[...] * cwg[1:2] + ug * cwg[2:3]
    y_ref[...] += _dot((val * jax.nn.silu(gt)).astype(BF16), wout_ref[...])


def _ffn_sample(x, norm_g, w_in, conv_w, w_out, buf):
    n_b = x.shape[0]
    n_c = D_FF // _FFN_CW
    b0, b1 = buf[:, 0, :], buf[:, 1, :]
    val = lambda shape0: pl.BlockSpec((shape0, _FFN_CW), lambda c: (0, c))
    gat = lambda shape0: pl.BlockSpec((shape0, _FFN_CW), lambda c: (0, n_c + c))
    y, uv, ug = pl.pallas_call(
        _ffn_sample_kernel,
        grid=(n_c,),
        in_specs=[_resident((n_b, D_MODEL)), _resident((1, D_MODEL)), val(D_MODEL), gat(D_MODEL),
                  val(FFN_CONV), gat(FFN_CONV), val(n_b), gat(n_b), val(n_b), gat(n_b),
                  pl.BlockSpec((_FFN_CW, D_MODEL), lambda c: (c, 0))],
        out_specs=[pl.BlockSpec((n_b, D_MODEL), lambda c: (0, 0)), val(n_b), val(n_b)],
        out_shape=[jax.ShapeDtypeStruct((n_b, D_MODEL), F32), jax.ShapeDtypeStruct((n_b, D_FF), F32),
                   jax.ShapeDtypeStruct((n_b, D_FF), F32)],
        scratch_shapes=[pltpu.VMEM((n_b, D_MODEL), BF16)],
        compiler_params=_cparams("arbitrary"),
        name="ffn_sample",
    )(x, norm_g.reshape(1, D_MODEL), w_in, w_in, conv_w, conv_w, b0, b0, b1, b1, w_out)
    new_buf = jnp.stack([b1, jnp.concatenate([uv, ug], axis=-1)], axis=1)
    return y, new_buf


def kernel(x_prompt, x_sample, cache_nsa_cmp, cache_nsa_sel, state_nsa_win, state_gdn_S, state_gdn_conv,
           state_lru_h, state_lru_conv, state_ffn_conv, page_table,
           norm_mix_g, norm_ffn_g, norm_final_g,
           nsa_w_in, nsa_cmp_pe, nsa_cmp_w, nsa_w_out,
           gdn_w_in, gdn_conv_w, gdn_A_log, gdn_dt_bias, gdn_norm_g, gdn_w_out,
           lru_w_in, lru_conv_w, lru_conv_b, lru_w_ga, lru_b_ga, lru_w_gx, lru_b_gx, lru_lambda, lru_w_out,
           ffn_w_in, ffn_conv_w, ffn_w_out):
    n_p, t_len, _ = x_prompt.shape
    n_s = x_sample.shape[0]
    past = page_table.shape[1] * PAGE_SIZE
    wl = min(WINDOW, t_len)
    xp = x_prompt.reshape(n_p * t_len, D_MODEL)
    xs = x_sample.reshape(n_s, D_MODEL)
    tab_p = _rope_tables(jnp.arange(t_len, dtype=I32))
    tab_s = _rope_tables(jnp.full((n_s,), past, I32))
    cache_cmp_t = _rows_last(cache_nsa_cmp)
    cache_sel_t = _rows_last(cache_nsa_sel)
    win_t = _rows_last(state_nsa_win)
    ffn_w_in16 = ffn_w_in.astype(BF16)
    ffn_w_out16 = ffn_w_out.astype(BF16)
    zeros = lambda *shape: jnp.zeros(shape, F32)
    kv5 = lambda rows, nb: rows.reshape(nb, -1, 2, NSA_KV, NSA_DH)

    p_nsa, s_nsa, p_gdn, s_gdn, p_lru, s_lru, p_ffn, s_ffn = [], [], [], [], [], [], [], []
    for li in range(DEPTH):
        j = li // N_MIXERS
        kind = li % N_MIXERS
        if kind == 0:
            xp, c, s, w = _nsa_prompt_layer(xp, n_p, norm_mix_g[li], nsa_w_in[j], nsa_cmp_pe[j], nsa_cmp_w[j],
                                            nsa_w_out[j], tab_p)
            rows_first = lambda a: jnp.transpose(a, (0, 4, 1, 2, 3))
            p_nsa.append((rows_first(c), rows_first(s), rows_first(w[..., t_len - wl:])))
            xs, c, s, w = _nsa_sample_layer(xs, j, norm_mix_g[li], nsa_w_in[j], nsa_cmp_pe[j], nsa_cmp_w[j],
                                            nsa_w_out[j], cache_cmp_t, cache_sel_t, win_t[j], page_table, tab_s)
            s_nsa.append((kv5(c, n_s), kv5(s, n_s), jnp.transpose(w, (0, 4, 1, 2, 3))))
        elif kind == 1:
            args = (gdn_w_in[j], gdn_conv_w[j], gdn_A_log[j], gdn_dt_bias[j], gdn_norm_g[j], gdn_w_out[j])
            xp, s_new, buf = _gdn_prompt_layer(xp, n_p, norm_mix_g[li], zeros(n_p, GDN_HEADS, GDN_DK, GDN_DV),
                                               zeros(n_p, GDN_CONV - 1, GDN_CONV_CH), *args)
            p_gdn.append((s_new, buf))
            xs, s_new, buf = _gdn_sample_layer(xs, norm_mix_g[li], state_gdn_S[j], state_gdn_conv[j], *args)
            s_gdn.append((s_new, buf))
        else:
            args = (lru_w_in[j], lru_conv_w[j], lru_conv_b[j], lru_w_ga[j], lru_b_ga[j], lru_w_gx[j], lru_b_gx[j],
                    lru_lambda[j], lru_w_out[j])
            xp, h, buf = _lru_prompt_layer(xp, n_p, norm_mix_g[li], zeros(n_p, D_RNN), zeros(n_p, LRU_CONV - 1, D_RNN),
                                           *args)
            p_lru.append((h, buf))
            xs, h, buf = _lru_sample_layer(xs, norm_mix_g[li], state_lru_h[j], state_lru_conv[j], *args)
            s_lru.append((h, buf))
        xp, buf = _ffn_prompt(xp, n_p, norm_ffn_g[li], ffn_w_in16[li], ffn_conv_w[li], ffn_w_out16[li],
                              zeros(n_p, FFN_CONV - 1, 2 * D_FF))
        p_ffn.append(buf)
        xs, buf = _ffn_sample(xs, norm_ffn_g[li], ffn_w_in16[li], ffn_conv_w[li], ffn_w_out16[li], state_ffn_conv[li])
        s_ffn.append(buf)

    y_prompt = _final_norm(xp, norm_final_g, 512).reshape(n_p, t_len, D_MODEL)
    y_sample = _final_norm(xs, norm_final_g, n_s).reshape(n_s, 1, D_MODEL)
    stack = lambda entries, k: jnp.stack([e[k] for e in entries])
    return (y_prompt, y_sample,
            stack(p_nsa, 0), stack(s_nsa, 0), stack(p_nsa, 1), stack(s_nsa, 1), stack(p_nsa, 2), stack(s_nsa, 2),
            stack(p_gdn, 0), stack(s_gdn, 0), stack(p_gdn, 1), stack(s_gdn, 1),
            stack(p_lru, 0), stack(s_lru, 0), stack(p_lru, 1), stack(s_lru, 1),
            jnp.stack(p_ffn), jnp.stack(s_ffn))
```

```python
import functools

import jax
import jax.numpy as jnp
from jax import lax
from jax.experimental import pallas as pl
from jax.experimental.pallas import tpu as pltpu

F32 = jnp.float32
BF16 = jnp.bfloat16
I32 = jnp.int32

D_MODEL = 1024
DEPTH = 4
PAST_LEN = 8192
PAGE_SIZE = 128
N_MIXERS = 3

NSA_DH = 64
NSA_HEADS = 16
NSA_HPG = 4
NSA_KV = 4
NSA_Q = NSA_HEADS * NSA_DH
NSA_KVW = NSA_KV * NSA_DH
ROT_DIM = NSA_DH // 4
ROPE_THETA = 500000.0
L_CMP = 32
L_SEL = 64
N_SEL = 16
WINDOW = 512
_LOG_SEL = 6
NSA_SCALE = NSA_DH ** -0.5

GDN_DK = 128
GDN_DV = 128
GDN_HEADS = 8
GDN_QK = GDN_HEADS * GDN_DK
GDN_VW = GDN_HEADS * GDN_DV
GDN_CONV = 4
GDN_CONV_CH = 2 * GDN_QK + GDN_VW
GDN_CHUNK = 64

D_RNN = D_MODEL
LRU_BLOCKS = 8
LRU_BW = D_RNN // LRU_BLOCKS
LRU_CONV = 4
LRU_C = 8.0

D_FF = 2816
FFN_CONV = 3

EPS = 1e-6
NEG = -1e30
FORCE = 1e4

LANES = 128
SUBLANES = 8
MXU_DIM = 256
VMEM_LIMIT = 56 * 1024 * 1024


def _cparams(*sem):
    return pltpu.CompilerParams(dimension_semantics=sem, vmem_limit_bytes=VMEM_LIMIT)


def _resident(shape):
    nd = len(shape)
    return pl.BlockSpec(shape, lambda *_: (0,) * nd, pipeline_mode=pl.Buffered(1))


def _rmsnorm(x, g):
    return x * lax.rsqrt(jnp.mean(x * x, axis=-1, keepdims=True) + EPS) * g


def _dot(a, b):
    return jnp.dot(a, b, preferred_element_type=F32)


def _dot_nt(a, b):
    return lax.dot_general(a, b, (((1,), (1,)), ((), ())), preferred_element_type=F32)


def _dot_tn(a, b):
    return lax.dot_general(a, b, (((0,), (0,)), ((), ())), preferred_element_type=F32)


def _dot_f32(a, b):
    return jnp.dot(a, b, precision=lax.Precision.HIGHEST, preferred_element_type=F32)


def _iota(shape, axis):
    return lax.broadcasted_iota(I32, shape, axis)


def _rope_tables(pos):
    half = ROT_DIM // 2
    inv = ROPE_THETA ** (-jnp.arange(half, dtype=F32) / half)
    ang = pos.astype(F32)[:, None] * inv[None, :]
    lane = jnp.arange(LANES) % NSA_DH
    c = jnp.cos(ang)[:, lane % half]
    s = jnp.sin(ang)[:, lane % half]
    cos = jnp.where(lane < ROT_DIM, c, 1.0)
    sin_a = jnp.where((lane >= half) & (lane < ROT_DIM), s, 0.0)
    sin_b = jnp.where(lane < half, -s, 0.0)
    return cos.astype(F32), sin_a.astype(F32), sin_b.astype(F32)


def _rope(x, cos, sin_a, sin_b):
    half = ROT_DIM // 2
    outs = []
    for c in range(x.shape[1] // LANES):
        xc = x[:, c * LANES:(c + 1) * LANES]
        outs.append(xc * cos + pltpu.roll(xc, half, 1) * sin_a + pltpu.roll(xc, LANES - half, 1) * sin_b)
    return outs[0] if len(outs) == 1 else jnp.concatenate(outs, axis=1)


_NSA_W_COLS = NSA_Q + 6 * NSA_KVW + LANES


def _store_seq_last(o_ref, rows):
    for c in range(2 * NSA_KVW // LANES):
        t = rows[:, c * LANES:(c + 1) * LANES].T
        kv, g0 = c // 2, 2 * (c % 2)
        o_ref[kv, g0] = t[:NSA_DH]
        o_ref[kv, g0 + 1] = t[NSA_DH:]


def _nsa_proj_kernel(x_ref, g_ref, w_ref, cos_ref, sa_ref, sb_ref,
                     q_ref, qr_ref, cmp_ref, sel_ref, win_ref, gate_ref, kvb_ref, *seq_last_refs):
    xn = _rmsnorm(x_ref[...], g_ref[...]).astype(BF16)
    cos, sa, sb = cos_ref[...], sa_ref[...], sb_ref[...]

    def mm(c0, c1):
        return _dot(xn, w_ref[:, c0:c1])

    zq = mm(0, NSA_Q)
    q_ref[...] = (zq * NSA_SCALE).astype(BF16)
    qr_ref[...] = (_rope(zq, cos, sa, sb) * NSA_SCALE).astype(BF16)
    c0 = NSA_Q
    zc = mm(c0, c0 + 2 * NSA_KVW)
    cmp_ref[...] = zc
    if seq_last_refs:
        _store_seq_last(seq_last_refs[0], zc)
    c0 += 2 * NSA_KVW
    for kind, o_ref in ((0, sel_ref), (1, win_ref)):
        z = mm(c0, c0 + 2 * NSA_KVW)
        kr = _rope(z[:, :NSA_KVW], cos, sa, sb)
        v = z[:, NSA_KVW:]
        o_ref[:, :NSA_KVW] = kr
        o_ref[:, NSA_KVW:] = v
        if seq_last_refs:
            _store_seq_last(seq_last_refs[1 + kind], o_ref[...])
        for g in range(NSA_KV):
            kvb_ref[2 * kind, g] = kr[:, g * NSA_DH:(g + 1) * NSA_DH].astype(BF16)
            kvb_ref[2 * kind + 1, g] = v[:, g * NSA_DH:(g + 1) * NSA_DH].astype(BF16)
        c0 += 2 * NSA_KVW
    gate_ref[...] = jax.nn.sigmoid(mm(c0, c0 + LANES))


def _nsa_proj(x, g, w, tables, n_batch, tm, seq_last=False):
    n = x.shape[0]
    t_len = n // n_batch
    tpb = t_len // tm
    row = lambda i: (i, 0)
    tab = pl.BlockSpec((tm, LANES), lambda i: (i % tpb, 0))
    out_specs = [pl.BlockSpec((tm, NSA_Q), row), pl.BlockSpec((tm, NSA_Q), row),
                 pl.BlockSpec((tm, 2 * NSA_KVW), row), pl.BlockSpec((tm, 2 * NSA_KVW), row),
                 pl.BlockSpec((tm, 2 * NSA_KVW), row), pl.BlockSpec((tm, LANES), row),
                 pl.BlockSpec((None, 4, NSA_KV, tm, NSA_DH), lambda i: (i // tpb, 0, 0, i % tpb, 0))]
    out_shape = [jax.ShapeDtypeStruct((n, NSA_Q), BF16), jax.ShapeDtypeStruct((n, NSA_Q), BF16),
                 jax.ShapeDtypeStruct((n, 2 * NSA_KVW), F32), jax.ShapeDtypeStruct((n, 2 * NSA_KVW), F32),
                 jax.ShapeDtypeStruct((n, 2 * NSA_KVW), F32), jax.ShapeDtypeStruct((n, LANES), F32),
                 jax.ShapeDtypeStruct((n_batch, 4, NSA_KV, t_len, NSA_DH), BF16)]
    if seq_last:
        out_specs += [pl.BlockSpec((None, 2, NSA_KV, NSA_DH, tm), lambda i: (i // tpb, 0, 0, 0, i % tpb))] * 3
        out_shape += [jax.ShapeDtypeStruct((n_batch, 2, NSA_KV, NSA_DH, t_len), F32)] * 3
    return pl.pallas_call(
        _nsa_proj_kernel,
        grid=(n // tm,),
        in_specs=[pl.BlockSpec((tm, D_MODEL), row), _resident((1, D_MODEL)), _resident((D_MODEL, _NSA_W_COLS)),
                  tab, tab, tab],
        out_specs=out_specs,
        out_shape=out_shape,
        compiler_params=_cparams("parallel"),
        name="nsa_proj",
    )(x, g, w, *tables)


def _nsa_w_in_padded(w_in):
    pad = _NSA_W_COLS - w_in.shape[1]
    return jnp.pad(w_in, ((0, 0), (0, pad))).astype(BF16)


def _cmp_weights(cmp_w, cmp_pe):
    eye = jnp.eye(NSA_KV, dtype=cmp_w.dtype)
    bd = jnp.einsum("gh,kldx->klgdhx", eye, cmp_w).reshape(2, L_CMP, NSA_KVW, NSA_KVW)
    return bd.astype(BF16), jnp.tile(cmp_pe, (1, 1, NSA_KV))


def _compress_planes(load_rows, bd_ref, pe_ref, o_ref, n_rows_out):
    for kv in range(2):
        acc = jnp.zeros((n_rows_out, NSA_KVW), F32)
        bias = jnp.zeros((SUBLANES, NSA_KVW), F32)
        for l in range(L_CMP):
            w = bd_ref[kv, l]
            acc = acc + _dot(load_rows(kv, l).astype(BF16), w)
            pe = jnp.broadcast_to(pe_ref[kv, l:l + 1, :], (SUBLANES, NSA_KVW)).astype(BF16)
            bias = bias + _dot(pe, w)
        out = acc + bias[0:1, :]
        for g in range(NSA_KV):
            o_ref[kv, g] = out[:, g * NSA_DH:(g + 1) * NSA_DH]


def _nsa_compress_prompt_kernel(x0_ref, x1_ref, x2_ref, x3_ref, bd_ref, pe_ref, o_ref, *, n_blk, n_pad):
    planes = (x0_ref, x1_ref, x2_ref, x3_ref)
    if n_pad > n_blk:
        o_ref[...] = jnp.zeros(o_ref.shape, F32)

    def load_rows(kv, l):
        return jnp.concatenate([planes[2 * kv + i][pl.ds(l, n_blk, stride=L_CMP), :] for i in range(2)], axis=1)

    if n_pad == n_blk:
        _compress_planes(load_rows, bd_ref, pe_ref, o_ref, n_blk)
    else:
        _compress_planes(load_rows, bd_ref, pe_ref, o_ref.at[:, :, 0:n_blk, :], n_blk)


def _nsa_compress_prompt(cmp_rows, bd, pe2, n_batch):
    n = cmp_rows.shape[0]
    t_len = n // n_batch
    n_blk = t_len // L_CMP
    n_pad = max(n_blk, LANES)
    planes = [pl.BlockSpec((t_len, LANES), functools.partial(lambda b, c: (b, c), c=c)) for c in range(4)]
    return pl.pallas_call(
        functools.partial(_nsa_compress_prompt_kernel, n_blk=n_blk, n_pad=n_pad),
        grid=(n_batch,),
        in_specs=planes + [_resident(bd.shape), _resident(pe2.shape)],
        out_specs=pl.BlockSpec((None, 2, NSA_KV, n_pad, NSA_DH), lambda b: (b, 0, 0, 0, 0)),
        out_shape=jax.ShapeDtypeStruct((n_batch, 2, NSA_KV, n_pad, NSA_DH), F32),
        compiler_params=_cparams("parallel"),
        name="nsa_compress_prompt",
    )(cmp_rows, cmp_rows, cmp_rows, cmp_rows, bd, pe2)


def _stack_heads(x):
    return jnp.concatenate([x[:, j * NSA_DH:(j + 1) * NSA_DH] for j in range(NSA_HPG)], axis=0)


def _softmax_rows(s):
    m = jnp.max(s, axis=-1, keepdims=True)
    e = jnp.exp(s - m)
    return e / jnp.sum(e, axis=-1, keepdims=True)


def _select_blocks(imp_t, t0, rank_scr):
    n_sel_pad = LANES // 2
    tq = imp_t.shape[1]
    rank_scr[...] = imp_t
    pair = rank_scr[pl.ds(0, n_sel_pad, stride=2), :] + rank_scr[pl.ds(1, n_sel_pad, stride=2), :]
    blk = _iota((n_sel_pad, tq), 0)
    tpos = t0 + _iota((n_sel_pad, tq), 1)
    forced = (blk == 0) | (blk == (tpos >> _LOG_SEL))
    avail = blk * L_SEL <= tpos
    val = jnp.where(forced, FORCE, jnp.where(avail, pair, -1.0))
    rank_scr[0:n_sel_pad, :] = val
    cnt = jnp.zeros((n_sel_pad, tq), F32)
    for s in range(n_sel_pad):
        row = jnp.broadcast_to(rank_scr[pl.ds(s, 1), :], (n_sel_pad, tq))
        ahead = (row > val) | ((row == val) & (blk > s))
        cnt = cnt + jnp.where(ahead, 1.0, 0.0)
    sel = jnp.where(cnt < float(N_SEL), 1.0, 0.0)
    return jnp.concatenate([sel, jnp.zeros((LANES - n_sel_pad, tq), F32)], axis=0)


_ATT_RB = 128


def _softmax_blocks(sc_scr, bias_scr, p_scr, width, rows, tq, post=None):
    for r0 in range(0, rows, _ATT_RB):
        rq = r0 % tq
        s = sc_scr[r0:r0 + _ATT_RB, 0:width] + bias_scr[rq:rq + _ATT_RB, 0:width]
        e = jnp.exp(s - jnp.max(s, axis=-1, keepdims=True))
        p = e / jnp.sum(e, axis=-1, keepdims=True)
        if post is not None:
            p = post(p, r0)
        p_scr[r0:r0 + _ATT_RB, 0:width] = p.astype(BF16)


def _nsa_attn_kernel(q_ref, qr_ref, gate_ref, kvc_ref, kvb_ref, o_ref,
                     rank_scr, sc_scr, bias_scr, p_scr, m_scr, l_scr, acc_scr, imp_scr, *, tq, tk):
    g = pl.program_id(1)
    t0 = pl.program_id(2) * tq
    rows = NSA_HPG * tq
    q4 = _stack_heads(q_ref[...])
    qr4 = _stack_heads(qr_ref[...])

    n_cmp = kvc_ref.shape[1]
    sc_scr[:, 0:n_cmp] = _dot_nt(q4, kvc_ref[0].astype(BF16))
    tpos_c = t0 + _iota((tq, n_cmp), 0)
    c_end = (_iota((tq, n_cmp), 1) + 1) * L_CMP - 1
    bias_scr[:, 0:n_cmp] = jnp.where(c_end <= tpos_c, 0.0, NEG)
    imp_scr[...] = jnp.zeros(imp_scr.shape, F32)

    def cmp_post(p, r0):
        rq = r0 % tq
        p = p * jnp.where(t0 + rq + _iota((_ATT_RB, 1), 0) >= L_CMP - 1, 1.0, 0.0)
        imp_scr[rq:rq + _ATT_RB, :] += p
        return p

    _softmax_blocks(sc_scr, bias_scr, p_scr, n_cmp, rows, tq, post=cmp_post)
    o_c = _dot(p_scr[:, 0:n_cmp], kvc_ref[1].astype(BF16))

    sel = jnp.concatenate(
        [_select_blocks(imp_scr[h0:h0 + LANES, :].T, t0 + h0, rank_scr).T for h0 in range(0, tq, LANES)], axis=0)
    sel = jnp.where(sel > 0.5, 1.0, 0.0).astype(BF16)

    m_scr[...] = jnp.full(m_scr.shape, NEG, F32)
    l_scr[...] = jnp.zeros(l_scr.shape, F32)
    acc_scr[...] = jnp.zeros(acc_scr.shape, F32)

    def sel_tile(k0, tw):
        sc_scr[:, 0:tw] = _dot_nt(qr4, kvb_ref[0, pl.ds(k0, tw), :])
        expand = jnp.where(_iota((LANES, tw), 0) == ((k0 + _iota((LANES, tw), 1)) >> _LOG_SEL), 1.0, 0.0).astype(BF16)
        picked = _dot(sel, expand)
        live = (picked > 0.5) & (k0 + _iota((tq, tw), 1) <= t0 + _iota((tq, tw), 0))
        bias_scr[:, 0:tw] = jnp.where(live, 0.0, NEG)
        for r0 in range(0, rows, _ATT_RB):
            rq = r0 % tq
            rs = slice(r0, r0 + _ATT_RB)
            s = sc_scr[rs, 0:tw] + bias_scr[rq:rq + _ATT_RB, 0:tw]
            m_old = m_scr[rs, :]
            m_new = jnp.maximum(m_old, jnp.max(s, axis=-1, keepdims=True))
            alpha = jnp.exp(m_old - m_new)
            pe = jnp.exp(s - m_new)
            l_scr[rs, :] = alpha * l_scr[rs, :] + jnp.sum(pe, axis=-1, keepdims=True)
            m_scr[rs, :] = m_new
            acc_scr[rs, :] = alpha * acc_scr[rs, :]
            p_scr[rs, 0:tw] = pe.astype(BF16)
        acc_scr[...] += _dot(p_scr[:, 0:tw], kvb_ref[1, pl.ds(k0, tw), :])

    def full_tile(kt, carry):
        sel_tile(pl.multiple_of(kt * tk, tk), tk)
        return carry

    def tail_tile(kt, carry):
        sel_tile(pl.multiple_of(n_whole * tk, tk), tail)
        return carry

    tail = tk // 2
    n_whole = (t0 + tq) // tk
    rem = t0 + tq - n_whole * tk
    n_full = n_whole + jnp.where(rem > tail, 1, 0)
    n_tail = jnp.where((rem > 0) & (rem <= tail), 1, 0)
    lax.fori_loop(0, n_full, full_tile, 0)
    lax.fori_loop(0, n_tail, tail_tile, 0)
    o_s = acc_scr[...] / l_scr[...]

    wk = WINDOW + tq
    w0 = pl.multiple_of(jnp.maximum(t0 - WINDOW, 0), tq)
    sc_scr[:, 0:wk] = _dot_nt(qr4, kvb_ref[2, pl.ds(w0, wk), :])
    dist = (t0 + _iota((tq, wk), 0)) - (w0 + _iota((tq, wk), 1))
    bias_scr[:, 0:wk] = jnp.where((dist >= 0) & (dist <= WINDOW), 0.0, NEG)
    _softmax_blocks(sc_scr, bias_scr, p_scr, wk, rows, tq)
    o_w = _dot(p_scr[:, 0:wk], kvb_ref[3, pl.ds(w0, wk), :])

    gates = gate_ref[...]
    lane = _iota((tq, LANES), 1)
    outs = []
    for j in range(NSA_HPG):
        o = jnp.zeros((tq, NSA_DH), F32)
        for br, ob in enumerate((o_c, o_s, o_w)):
            col = br * NSA_HEADS + g * NSA_HPG + j
            gcol = jnp.sum(jnp.where(lane == col, gates, 0.0), axis=-1, keepdims=True)
            o = o + gcol * ob[j * tq:(j + 1) * tq]
        outs.append(o)
    o_ref[...] = jnp.concatenate(outs, axis=1).astype(BF16)


def _nsa_attn_prompt(q, qr, gates, kvc, kvb, n_batch, tq=256, tk=1024):
    n = q.shape[0]
    t_len = n // n_batch
    nq = t_len // tq
    tk = min(tk, t_len)
    assert kvc.shape[3] == LANES and t_len >= WINDOW + tq and t_len % tk == 0
    rows = NSA_HPG * tq
    width = max(tk, WINDOW + tq, LANES)
    qspec =pl.BlockSpec((tq, NSA_HPG * NSA_DH), lambda b, g, i: (b * nq + i, g))
    return pl.pallas_call(
        functools.partial(_nsa_attn_kernel, tq=tq, tk=tk),
        grid=(n_batch, NSA_KV, nq),
        in_specs=[qspec, qspec,
                  pl.BlockSpec((tq, LANES), lambda b, g, i: (b * nq + i, 0)),
                  pl.BlockSpec((None, 2, None, LANES, NSA_DH), lambda b, g, i: (b, 0, g, 0, 0)),
                  pl.BlockSpec((None, 4, None, t_len, NSA_DH), lambda b, g, i: (b, 0, g, 0, 0))],
        out_specs=qspec,
        out_shape=jax.ShapeDtypeStruct((n, NSA_Q), BF16),
        scratch_shapes=[pltpu.VMEM((LANES, LANES), F32),
                        pltpu.VMEM((rows, width), F32),
                        pltpu.VMEM((tq, width), F32),
                        pltpu.VMEM((rows, width), BF16),
                        pltpu.VMEM((rows, 1), F32), pltpu.VMEM((rows, 1), F32),
                        pltpu.VMEM((rows, NSA_DH), F32),
                        pltpu.VMEM((tq, LANES), F32)],
        compiler_params=_cparams("parallel", "parallel", "arbitrary"),
        name="nsa_attn_prompt",
    )(q, qr, gates, kvc, kvb)


def _matmul_residual_kernel(a_ref, w_ref, x_ref, o_ref):
    o_ref[...] = x_ref[...] + _dot(a_ref[...].astype(BF16), w_ref[...])


def _matmul_residual(a, w, x, tm):
    n, k = a.shape
    row = lambda i: (i, 0)
    return pl.pallas_call(
        _matmul_residual_kernel,
        grid=(n // tm,),
        in_specs=[pl.BlockSpec((tm, k), row), _resident(w.shape), pl.BlockSpec((tm, D_MODEL), row)],
        out_specs=pl.BlockSpec((tm, D_MODEL), row),
        out_shape=jax.ShapeDtypeStruct((n, D_MODEL), F32),
        compiler_params=_cparams("parallel"),
        name="matmul_residual",
    )(a, w, x)


def _nsa_prompt_layer(x, n_batch, norm_g, w_in, cmp_pe, cmp_w, w_out, tables):
    tm = min(512, x.shape[0] // n_batch)
    q, qr, cmp_rows, _, _, gates, kvb, cmp_t, sel_t, win_t = _nsa_proj(
        x, norm_g.reshape(1, D_MODEL), _nsa_w_in_padded(w_in), tables, n_batch, tm, seq_last=True)
    bd, pe2 = _cmp_weights(cmp_w, cmp_pe)
    kvc = _nsa_compress_prompt(cmp_rows, bd, pe2, n_batch)
    o = _nsa_attn_prompt(q, qr, gates, kvc, kvb, n_batch)
    y = _matmul_residual(o, w_out.astype(BF16), x, tm)
    return y, cmp_t, sel_t, win_t


_FFN_CW = MXU_DIM


def _ffn_kernel(x_ref, g_ref, win_ref, cw_ref, wout_ref, buf_ref, y_ref, st_ref, carry_scr, uv_scr, ug_scr,
                *, tm, tpb):
    ti = pl.program_id(0) % tpb
    x = x_ref[...]
    xn = _rmsnorm(x, g_ref[...]).astype(BF16)

    @pl.when(ti == 0)
    def _():
        carry_scr[SUBLANES - (FFN_CONV - 1):SUBLANES, :] = buf_ref[...]

    acc = jnp.zeros((tm, D_MODEL), F32)
    for c in range(D_FF // _FFN_CW):
        conv = []
        for part, scr in ((0, uv_scr), (1, ug_scr)):
            c0 = part * D_FF + c * _FFN_CW
            u = _dot(xn, win_ref[:, c0:c0 + _FFN_CW])
            scr[0:SUBLANES, :] = carry_scr[:, c0:c0 + _FFN_CW]
            scr[SUBLANES:SUBLANES + tm, :] = u
            carry_scr[:, c0:c0 + _FFN_CW] = u[tm - SUBLANES:tm, :]
            w = cw_ref[:, c0:c0 + _FFN_CW]
            conv.append(scr[SUBLANES - 2:SUBLANES - 2 + tm, :] * w[0:1] + scr[SUBLANES - 1:SUBLANES - 1 + tm, :] * w[1:2]
                        + u * w[2:3])
        act = conv[0] * jax.nn.silu(conv[1])
        acc = acc + _dot(act.astype(BF16), wout_ref[c * _FFN_CW:(c + 1) * _FFN_CW, :])
    y_ref[...] = x + acc

    @pl.when(ti == tpb - 1)
    def _():
        st_ref[...] = carry_scr[SUBLANES - (FFN_CONV - 1):SUBLANES, :]


def _ffn_prompt(x, n_batch, norm_g, w_in, conv_w, w_out, buf, tm=512):
    n = x.shape[0]
    tpb = n // n_batch // tm
    row = lambda i: (i, 0)
    st = pl.BlockSpec((None, FFN_CONV - 1, 2 * D_FF), lambda i: (i // tpb, 0, 0))
    return pl.pallas_call(
        functools.partial(_ffn_kernel, tm=tm, tpb=tpb),
        grid=(n // tm,),
        in_specs=[pl.BlockSpec((tm, D_MODEL), row), _resident((1, D_MODEL)), _resident(w_in.shape),
                  _resident(conv_w.shape), _resident(w_out.shape), st],
        out_specs=[pl.BlockSpec((tm, D_MODEL), row), st],
        out_shape=[jax.ShapeDtypeStruct((n, D_MODEL), F32),
                   jax.ShapeDtypeStruct((n_batch, FFN_CONV - 1, 2 * D_FF), F32)],
        scratch_shapes=[pltpu.VMEM((SUBLANES, 2 * D_FF), F32), pltpu.VMEM((SUBLANES + tm, _FFN_CW), F32),
                        pltpu.VMEM((SUBLANES + tm, _FFN_CW), F32)],
        compiler_params=_cparams("arbitrary"),
        name="ffn_prompt",
    )(x, norm_g.reshape(1, D_MODEL), w_in, conv_w, w_out, buf)


_GDN_W_COLS = GDN_CONV_CH + GDN_VW + LANES
_GDN_CW = MXU_DIM


def _softplus(x):
    return jnp.maximum(x, 0.0) + jnp.log1p(jnp.exp(-jnp.abs(x)))


def _gdn_w_in_padded(w_in):
    return jnp.pad(w_in, ((0, 0), (0, _GDN_W_COLS - w_in.shape[1]))).astype(BF16)


def _lane_row(v):
    return jnp.pad(v.astype(F32), (0, LANES - v.shape[0])).reshape(1, LANES)


def _gdn_post_conv(y, c0):
    y = jax.nn.silu(y)
    kind = c0 // GDN_QK
    if kind == 2:
        return y
    outs = []
    for hh in range(_GDN_CW // GDN_DK):
        seg = y[:, hh * GDN_DK:(hh + 1) * GDN_DK]
        seg = seg * lax.rsqrt(jnp.sum(seg * seg, axis=-1, keepdims=True) + EPS)
        outs.append(seg * GDN_DK ** -0.5 if kind == 0 else seg)
    return jnp.concatenate(outs, axis=1)


def _gdn_gates(ab, alog, dtb):
    lane = _iota(ab.shape, 1)
    gval = -jnp.exp(alog) * _softplus(ab + dtb)
    return jnp.where(lane < GDN_HEADS, gval, jax.nn.sigmoid(ab))


def _gdn_proj_kernel(x_ref, g_ref, w_ref, cw_ref, alog_ref, dtb_ref, buf_ref,
                     q_ref, k_ref, v_ref, gate_ref, gb_ref, st_ref, carry_scr, u_scr, *, tm, tpb):
    ti = pl.program_id(0) % tpb
    xn = _rmsnorm(x_ref[...], g_ref[...]).astype(BF16)
    keep = GDN_CONV - 1

    @pl.when(ti == 0)
    def _():
        carry_scr[SUBLANES - keep:SUBLANES, :] = buf_ref[...]

    outs = (q_ref, k_ref, v_ref)
    for c in range(GDN_CONV_CH // _GDN_CW):
        c0 = c * _GDN_CW
        u = _dot(xn, w_ref[:, c0:c0 + _GDN_CW])
        u_scr[0:SUBLANES, :] = carry_scr[:, c0:c0 + _GDN_CW]
        u_scr[SUBLANES:SUBLANES + tm, :] = u
        carry_scr[:, c0:c0 + _GDN_CW] = u[tm - SUBLANES:tm, :]
        w = cw_ref[:, c0:c0 + _GDN_CW]
        y = u_scr[SUBLANES - 3:SUBLANES - 3 + tm, :] * w[0:1]
        y = y + u_scr[SUBLANES - 2:SUBLANES - 2 + tm, :] * w[1:2]
        y = y + u_scr[SUBLANES - 1:SUBLANES - 1 + tm, :] * w[2:3]
        y = y + u * w[3:4]
        o_ref = outs[c0 // GDN_QK]
        o_ref[:, c0 % GDN_QK:c0 % GDN_QK + _GDN_CW] = _gdn_post_conv(y, c0)
    gate_ref[...] = _dot(xn, w_ref[:, GDN_CONV_CH:GDN_CONV_CH + GDN_VW])
    ab = _dot(xn, w_ref[:, GDN_CONV_CH + GDN_VW:_GDN_W_COLS])
    gb_ref[...] = _gdn_gates(ab, alog_ref[...], dtb_ref[...])

    @pl.when(ti == tpb - 1)
    def _():
        st_ref[...] = carry_scr[SUBLANES - keep:SUBLANES, :]


def _gdn_proj(x, n_batch, norm_g, w, conv_w, alog, dtb, buf, tm=256):
    n = x.shape[0]
    tpb = n // n_batch // tm
    row = lambda i: (i, 0)
    wide = pl.BlockSpec((tm, GDN_QK), row)
    st = pl.BlockSpec((None, GDN_CONV - 1, GDN_CONV_CH), lambda i: (i // tpb, 0, 0))
    f = jax.ShapeDtypeStruct((n, GDN_QK), F32)
    return pl.pallas_call(
        functools.partial(_gdn_proj_kernel, tm=tm, tpb=tpb),
        grid=(n // tm,),
        in_specs=[pl.BlockSpec((tm, D_MODEL), row), _resident((1, D_MODEL)), _resident(w.shape),
                  _resident(conv_w.shape), _resident((1, LANES)), _resident((1, LANES)), st],
        out_specs=[wide, wide, wide, wide, pl.BlockSpec((tm, LANES), row), st],
        out_shape=[f, f, f, f, jax.ShapeDtypeStruct((n, LANES), F32),
                   jax.ShapeDtypeStruct((n_batch, GDN_CONV - 1, GDN_CONV_CH), F32)],
        scratch_shapes=[pltpu.VMEM((SUBLANES, GDN_CONV_CH), F32), pltpu.VMEM((SUBLANES + tm, _GDN_CW), F32)],
        compiler_params=_cparams("arbitrary"),
        name="gdn_proj",
    )(x, norm_g.reshape(1, D_MODEL), w, conv_w, alog, dtb, buf)


_GDN_HG = MXU_DIM // GDN_CHUNK


def _gdn_block_diag(x):
    head = _iota(x.shape, 0) // GDN_CHUNK
    return jnp.concatenate([jnp.where(head == h, x, 0.0) for h in range(_GDN_HG)], axis=1)


def _gdn_group_prep(q, k, v, beta, gcum_col, gcum_row, g_last):
    n = q.shape[0]
    ii = _iota((n, n), 0)
    jj = _iota((n, n), 1)
    same = (ii // GDN_CHUNK) == (jj // GDN_CHUNK)
    decay = jnp.exp(jnp.where(same & (ii >= jj), gcum_col - gcum_row, NEG))
    kb = k * beta
    k16 = k.astype(BF16)
    low = _dot_nt(kb.astype(BF16), k16) * decay * jnp.where(ii > jj, 1.0, 0.0)
    attn = _dot_nt(q.astype(BF16), k16) * decay
    x = jnp.where(ii == jj, 1.0, 0.0) - low
    p = low
    for _ in range(GDN_CHUNK.bit_length() - 2):
        p16 = p.astype(BF16)
        p = _dot(p16, p16)
        x = x + _dot(x.astype(BF16), p.astype(BF16))
    rhs = jnp.concatenate([v * beta, kb * jnp.exp(gcum_col)], axis=1)
    sol = _dot(x.astype(BF16), rhs.astype(BF16))
    u_base, w_dec = sol[:, :GDN_DV], sol[:, GDN_DV:]
    return (u_base, _gdn_block_diag(w_dec).astype(BF16), _gdn_block_diag(q * jnp.exp(gcum_col)).astype(BF16),
            attn.astype(BF16), _gdn_block_diag(k * jnp.exp(g_last - gcum_col)).astype(BF16))


def _gdn_group_step(prep, s_prev, s_decay):
    u_base, w_dec_bd, q_bd, attn, k_dec_bd = prep
    s16 = s_prev.astype(BF16)
    u = u_base - _dot(w_dec_bd, s16)
    u16 = u.astype(BF16)
    o = _dot(q_bd, s16) + _dot(attn, u16)
    s_new = s_prev * s_decay + _dot_tn(k_dec_bd, u16)
    return o, s_new


def _gdn_chunk_kernel(q_ref, k_ref, v_ref, gb_ref, gate_ref, x_ref, s0_ref, ng_ref, wout_ref,
                      y_ref, s_ref, s_scr, o_scr, *, rt, tpb):
    ti = pl.program_id(1)
    n_groups = GDN_HEADS // _GDN_HG

    @pl.when(ti == 0)
    def _():
        s_scr[...] = s0_ref[...].reshape(n_groups, _GDN_HG * GDN_DK, GDN_DV)

    c = GDN_CHUNK
    tri = jnp.where(_iota((c, c), 0) >= _iota((c, c), 1), 1.0, 0.0)
    stack_rows = lambda ref, r0, h0: jnp.concatenate(
        [ref[r0:r0 + c, (h0 + h) * GDN_DK:(h0 + h + 1) * GDN_DK] for h in range(_GDN_HG)], axis=0)

    preps, decays = [], []
    for ci in range(rt // c):
        r0 = ci * c
        gb = gb_ref[r0:r0 + c, :]
        gcum = _dot_f32(tri, gb)
        gcum_t = gcum.T
        for gi in range(n_groups):
            h0 = gi * _GDN_HG
            heads = range(h0, h0 + _GDN_HG)
            col = lambda a: jnp.concatenate([a[:, h:h + 1] for h in heads], axis=0)
            g_last = jnp.concatenate([jnp.broadcast_to(gcum[c - 1:c, h:h + 1], (c, 1)) for h in heads], axis=0)
            gcum_row = jnp.concatenate([gcum_t[h:h + 1, :] for h in heads], axis=1)
            beta = jnp.concatenate([gb[:, GDN_HEADS + h:GDN_HEADS + h + 1] for h in heads], axis=0)
            preps.append(_gdn_group_prep(stack_rows(q_ref, r0, h0), stack_rows(k_ref, r0, h0),
                                         stack_rows(v_ref, r0, h0), beta, col(gcum), gcum_row, g_last))
            decays.append(jnp.concatenate(
                [jnp.broadcast_to(jnp.exp(gcum[c - 1:c, h:h + 1]), (GDN_DK, 1)) for h in heads], axis=0))

    for gi in range(n_groups):
        s = s_scr[gi]
        for ci in range(rt // c):
            o, s = _gdn_group_step(preps[ci * n_groups + gi], s, decays[ci * n_groups + gi])
            for h in range(_GDN_HG):
                o_scr[ci * c:(ci + 1) * c, (gi * _GDN_HG + h) * GDN_DV:(gi * _GDN_HG + h + 1) * GDN_DV] = (
                    o[h * c:(h + 1) * c])
        s_scr[gi] = s

    ng = ng_ref[...]
    outs = []
    for h in range(GDN_HEADS):
        cols = slice(h * GDN_DV, (h + 1) * GDN_DV)
        outs.append((_rmsnorm(o_scr[:, cols], ng) * jax.nn.silu(gate_ref[:, cols])).astype(BF16))
    y_ref[...] = x_ref[...] + _dot(jnp.concatenate(outs, axis=1), wout_ref[...])

    @pl.when(ti == tpb - 1)
    def _():
        s_ref[...] = s_scr[...].reshape(GDN_HEADS, GDN_DK, GDN_DV)


def _gdn_chunked(q, k, v, gb, gate, x, s0, norm_g, w_out, n_batch, rt=256):
    n = x.shape[0]
    tpb = n // n_batch // rt
    row = lambda b, i: (b * tpb + i, 0)
    wide = pl.BlockSpec((rt, GDN_QK), row)
    st = pl.BlockSpec((None, GDN_HEADS, GDN_DK, GDN_DV), lambda b, i: (b, 0, 0, 0))
    return pl.pallas_call(
        functools.partial(_gdn_chunk_kernel, rt=rt, tpb=tpb),
        grid=(n_batch, tpb),
        in_specs=[wide, wide, wide, pl.BlockSpec((rt, LANES), row), wide, wide, st,
                  _resident((1, GDN_DV)), _resident(w_out.shape)],
        out_specs=[wide, st],
        out_shape=[jax.ShapeDtypeStruct((n, D_MODEL), F32),
                   jax.ShapeDtypeStruct((n_batch, GDN_HEADS, GDN_DK, GDN_DV), F32)],
        scratch_shapes=[pltpu.VMEM((GDN_HEADS // _GDN_HG, _GDN_HG * GDN_DK, GDN_DV), F32),
                        pltpu.VMEM((rt, GDN_VW), F32)],
        compiler_params=_cparams("parallel", "arbitrary"),
        name="gdn_chunked",
    )(q, k, v, gb, gate, x, s0, norm_g.reshape(1, GDN_DV), w_out)


def _gdn_prompt_layer(x, n_batch, norm_g, s0, conv_buf, w_in, conv_w, a_log, dt_bias, head_norm_g, w_out):
    q, k, v, gate, gb, new_buf = _gdn_proj(x, n_batch, norm_g, _gdn_w_in_padded(w_in), conv_w,
                                           _lane_row(a_log), _lane_row(dt_bias), conv_buf)
    y, s_new = _gdn_chunked(q, k, v, gb, gate, x, s0, head_norm_g, w_out.astype(BF16), n_batch)
    return y, s_new, new_buf


_LRU_CW = MXU_DIM


def _lru_gates(xc, wga_ref, bga, wgx_ref, bgx, lam, n0):
    r_parts, i_parts = [], []
    for j in range(xc.shape[1] // LRU_BW):
        blk = xc[:, j * LRU_BW:(j + 1) * LRU_BW].astype(BF16)
        r_parts.append(_dot(blk, wga_ref[n0 + j]))
        i_parts.append(_dot(blk, wgx_ref[n0 + j]))
    cat = lambda ps: ps[0] if len(ps) == 1 else jnp.concatenate(ps, axis=1)
    r = jax.nn.sigmoid(cat(r_parts) + bga)
    i = jax.nn.sigmoid(cat(i_parts) + bgx)
    log_a = -LRU_C * r * _softplus(-lam)
    a = jnp.exp(log_a)
    b = jnp.sqrt(-jnp.tanh(log_a) * (a * a + 1.0)) * i * xc
    return a, b


def _lru_proj_kernel(x_ref, g_ref, w_ref, cw_ref, cb_ref, wga_ref, bga_ref, wgx_ref, bgx_ref, lam_ref, buf_ref,
                     gate_ref, a_ref, b_ref, st_ref, carry_scr, u_scr, *, tm, tpb):
    ti = pl.program_id(0) % tpb
    xn = _rmsnorm(x_ref[...], g_ref[...]).astype(BF16)
    keep = LRU_CONV - 1

    @pl.when(ti == 0)
    def _():
        carry_scr[SUBLANES - keep:SUBLANES, :] = buf_ref[...]

    gate_ref[...] = _dot(xn, w_ref[:, 0:D_RNN])
    for c in range(D_RNN // _LRU_CW):
        c0 = c * _LRU_CW
        cols = slice(c0, c0 + _LRU_CW)
        u = _dot(xn, w_ref[:, D_RNN + c0:D_RNN + c0 + _LRU_CW])
        u_scr[0:SUBLANES, :] = carry_scr[:, cols]
        u_scr[SUBLANES:SUBLANES + tm, :] = u
        carry_scr[:, cols] = u[tm - SUBLANES:tm, :]
        w = cw_ref[:, cols]
        xc = u_scr[SUBLANES - 3:SUBLANES - 3 + tm, :] * w[0:1]
        xc = xc + u_scr[SUBLANES - 2:SUBLANES - 2 + tm, :] * w[1:2]
        xc = xc + u_scr[SUBLANES - 1:SUBLANES - 1 + tm, :] * w[2:3]
        xc = xc + u * w[3:4] + cb_ref[:, cols]
        a, b = _lru_gates(xc, wga_ref, bga_ref[:, cols], wgx_ref, bgx_ref[:, cols], lam_ref[:, cols],
                          c0 // LRU_BW)
        a_ref[:, cols] = a
        b_ref[:, cols] = b

    @pl.when(ti == tpb - 1)
    def _():
        st_ref[...] = carry_scr[SUBLANES - keep:SUBLANES, :]


def _lru_proj(x, n_batch, norm_g, w, conv_w, conv_b, w_ga, b_ga, w_gx, b_gx, lam, buf, tm=256):
    n = x.shape[0]
    tpb = n // n_batch // tm
    row = lambda i: (i, 0)
    wide = pl.BlockSpec((tm, D_RNN), row)
    vec = _resident((1, D_RNN))
    st = pl.BlockSpec((None, LRU_CONV - 1, D_RNN), lambda i: (i // tpb, 0, 0))
    f = jax.ShapeDtypeStruct((n, D_RNN), F32)
    r1 = lambda v: v.reshape(1, D_RNN)
    return pl.pallas_call(
        functools.partial(_lru_proj_kernel, tm=tm, tpb=tpb),
        grid=(n // tm,),
        in_specs=[pl.BlockSpec((tm, D_MODEL), row), _resident((1, D_MODEL)), _resident(w.shape),
                  _resident(conv_w.shape), vec, _resident(w_ga.shape), vec, _resident(w_gx.shape), vec, vec, st],
        out_specs=[wide, wide, wide, st],
        out_shape=[f, f, f, jax.ShapeDtypeStruct((n_batch, LRU_CONV - 1, D_RNN), F32)],
        scratch_shapes=[pltpu.VMEM((SUBLANES, D_RNN), F32), pltpu.VMEM((SUBLANES + tm, _LRU_CW), F32)],
        compiler_params=_cparams("arbitrary"),
        name="lru_proj",
    )(x, norm_g.reshape(1, D_MODEL), w, conv_w, r1(conv_b), w_ga, r1(b_ga), w_gx, r1(b_gx), r1(lam), buf)


def _lru_scan_kernel(a_ref, b_ref, gate_ref, x_ref, h0_ref, wout_ref, y_ref, hl_ref, h_scr, hs_scr, *, rt, tpb):
    ti = pl.program_id(1)

    @pl.when(ti == 0)
    def _():
        h_scr[...] = h0_ref[...]

    def step(t, h):
        h = a_ref[pl.ds(t, 1), :] * h + b_ref[pl.ds(t, 1), :]
        hs_scr[pl.ds(t, 1), :] = h
        return h

    h = lax.fori_loop(0, rt, step, h_scr[...], unroll=8)
    h_scr[...] = h
    y = (hs_scr[...] * jax.nn.gelu(gate_ref[...])).astype(BF16)
    y_ref[...] = x_ref[...] + _dot(y, wout_ref[...])

    @pl.when(ti == tpb - 1)
    def _():
        hl_ref[...] = h


def _lru_scan(a, b, gate, x, h0, w_out, n_batch, rt=256):
    n = x.shape[0]
    tpb = n // n_batch // rt
    wide = pl.BlockSpec((rt, D_RNN), lambda bb, i: (bb * tpb + i, 0))
    st = pl.BlockSpec((None, 1, D_RNN), lambda bb, i: (bb, 0, 0))
    y, hl = pl.pallas_call(
        functools.partial(_lru_scan_kernel, rt=rt, tpb=tpb),
        grid=(n_batch, tpb),
        in_specs=[wide, wide, wide, wide, st, _resident(w_out.shape)],
        out_specs=[wide, st],
        out_shape=[jax.ShapeDtypeStruct((n, D_MODEL), F32), jax.ShapeDtypeStruct((n_batch, 1, D_RNN), F32)],
        scratch_shapes=[pltpu.VMEM((1, D_RNN), F32), pltpu.VMEM((rt, D_RNN), F32)],
        compiler_params=_cparams("parallel", "arbitrary"),
        name="lru_scan",
    )(a, b, gate, x, h0.reshape(n_batch, 1, D_RNN), w_out)
    return y, hl.reshape(n_batch, D_RNN)


def _lru_prompt_layer(x, n_batch, norm_g, h0, conv_buf, w_in, conv_w, conv_b, w_ga, b_ga, w_gx, b_gx, lam, w_out):
    gate, a, b, new_buf = _lru_proj(x, n_batch, norm_g, w_in.astype(BF16), conv_w, conv_b, w_ga.astype(BF16), b_ga,
                                    w_gx.astype(BF16), b_gx, lam, conv_buf)
    y, h_last = _lru_scan(a, b, gate, x, h0, w_out.astype(BF16), n_batch)
    return y, h_last, new_buf


def _final_norm_kernel(x_ref, g_ref, o_ref):
    o_ref[...] = _rmsnorm(x_ref[...], g_ref[...])


def _final_norm(x, g, tm):
    n = x.shape[0]
    row = lambda i: (i, 0)
    return pl.pallas_call(
        _final_norm_kernel,
        grid=(n // tm,),
        in_specs=[pl.BlockSpec((tm, D_MODEL), row), _resident((1, D_MODEL))],
        out_specs=pl.BlockSpec((tm, D_MODEL), row),
        out_shape=jax.ShapeDtypeStruct((n, D_MODEL), F32),
        compiler_params=_cparams("parallel"),
        name="final_norm",
    )(x, g.reshape(1, D_MODEL))


def _norm_matmul_small_kernel(x_ref, g_ref, w_ref, o_ref):
    xn = _rmsnorm(x_ref[...], g_ref[...]).astype(BF16)
    o_ref[...] = _dot(xn, w_ref[...])


def _norm_matmul_small(x, g, w, n_col_tiles):
    n, k = x.shape[0], w.shape[1]
    tn = k // n_col_tiles
    return pl.pallas_call(
        _norm_matmul_small_kernel,
        grid=(n_col_tiles,),
        in_specs=[_resident((n, D_MODEL)), _resident((1, D_MODEL)), pl.BlockSpec((D_MODEL, tn), lambda j: (0, j))],
        out_specs=pl.BlockSpec((n, tn), lambda j: (0, j)),
        out_shape=jax.ShapeDtypeStruct((n, k), F32),
        compiler_params=_cparams("parallel"),
        name="norm_matmul_small",
    )(x, g.reshape(1, D_MODEL), w)


_CMP_PAGES_PER_STEP = 8
_CMP_BLK_PER_PAGE = PAGE_SIZE // L_CMP


def _nsa_compress_sample_kernel(pt_ref, *refs, n_pages, n_blk, n_out):
    pages = refs[:_CMP_PAGES_PER_STEP]
    new_ref, bd_ref, pe_ref, o_ref, rows_scr = refs[_CMP_PAGES_PER_STEP:]
    b = pl.program_id(0)
    pg = pl.program_id(1)
    past_blk = n_pages * _CMP_BLK_PER_PAGE
    pair_rows = 2 * PAGE_SIZE
    out_row = _iota((pair_rows, pair_rows), 0)
    src_row = _iota((pair_rows, pair_rows), 1)
    blk_per_pair = 2 * _CMP_BLK_PER_PAGE
    perm = jnp.where(src_row == (out_row % blk_per_pair) * L_CMP + out_row // blk_per_pair, 1.0, 0.0).astype(BF16)
    for r in range(0, _CMP_PAGES_PER_STEP, 2):
        c0 = pl.multiple_of((pg * _CMP_PAGES_PER_STEP + r) * _CMP_BLK_PER_PAGE, blk_per_pair)
        for kv in range(2):
            both = jnp.concatenate(
                [pages[r + i][kv].reshape(NSA_KVW, PAGE_SIZE) for i in range(2)], axis=1)
            moved = _dot_nt(perm, both.astype(BF16))
            for l in range(L_CMP):
                rows_scr[kv, l, pl.ds(c0, blk_per_pair), :] = moved[l * blk_per_pair:(l + 1) * blk_per_pair]

    @pl.when(pg == pl.num_programs(1) - 1)
    def _():
        new = new_ref[pl.ds(b, 1), :]
        n_tail = n_blk - past_blk
        first = _iota((n_tail, NSA_KVW), 0) == 0
        for kv in range(2):
            for l in range(L_CMP):
                tail = jnp.where(first, new[:, kv * NSA_KVW:(kv + 1) * NSA_KVW], 0.0) if l == 0 else (
                    jnp.zeros((n_tail, NSA_KVW), F32))
                rows_scr[kv, l, past_blk:n_blk, :] = tail
        o_ref[...] = jnp.zeros(o_ref.shape, F32)

        def load_rows(kv, l):
            return rows_scr[kv, l]

        _compress_planes(load_rows, bd_ref, pe_ref, o_ref.at[:, :, 0:n_blk, :], n_blk)


def _rows_last(cache):
    nd = cache.ndim
    return jnp.transpose(cache, tuple(range(nd - 4)) + (nd - 3, nd - 2, nd - 1, nd - 4))


def _nsa_compress_sample(cache_t, layer, page_table, new_rows, bd, pe2):
    n_b, n_pages = page_table.shape
    past = n_pages * PAGE_SIZE
    n_blk = -(-(past + 1) // L_SEL) * L_SEL // L_CMP
    n_blk = -(-n_blk // SUBLANES) * SUBLANES
    n_out = -(-n_blk // LANES) * LANES
    cache4 = cache_t

    def page_spec(r):
        return pl.BlockSpec((None, None, 2, NSA_KV, NSA_DH, PAGE_SIZE),
                            lambda b, pg, pt: (layer, pt[b, pg * _CMP_PAGES_PER_STEP + r], 0, 0, 0, 0))

    grid_spec = pltpu.PrefetchScalarGridSpec(
        num_scalar_prefetch=1,
        grid=(n_b, n_pages // _CMP_PAGES_PER_STEP),
        in_specs=[page_spec(r) for r in range(_CMP_PAGES_PER_STEP)]
        + [pl.BlockSpec(new_rows.shape, lambda b, pg, pt: (0, 0)),
           pl.BlockSpec(bd.shape, lambda b, pg, pt: (0, 0, 0, 0)),
           pl.BlockSpec(pe2.shape, lambda b, pg, pt: (0, 0, 0))],
        out_specs=pl.BlockSpec((None, 2, NSA_KV, n_out, NSA_DH), lambda b, pg, pt: (b, 0, 0, 0, 0)),
        scratch_shapes=[pltpu.VMEM((2, L_CMP, n_blk, NSA_KVW), F32)])
    return pl.pallas_call(
        functools.partial(_nsa_compress_sample_kernel, n_pages=n_pages, n_blk=n_blk, n_out=n_out),
        grid_spec=grid_spec,
        out_shape=jax.ShapeDtypeStruct((n_b, 2, NSA_KV, n_out, NSA_DH), F32),
        compiler_params=_cparams("parallel", "arbitrary"),
        name="nsa_compress_sample",
    )(page_table, *([cache4] * _CMP_PAGES_PER_STEP), new_rows, bd, pe2)


def _pad_rows8(x):
    return jnp.concatenate([x, jnp.zeros((SUBLANES - x.shape[0], x.shape[1]), x.dtype)], axis=0)


_TOPK_ROWS_PER_STEP = 8


def _nsa_cmp_topk_sample_kernel(q_ref, kvc_ref, oc_ref, idx_ref, *, qpos, n_sel):
    nb = q_ref.shape[0]
    n_cmp = kvc_ref.shape[3]
    n_lane = 2 * LANES
    c_end = (_iota((SUBLANES, n_cmp), 1) + 1) * L_CMP - 1
    valid = c_end <= qpos
    any_valid = 1.0 if qpos >= L_CMP - 1 else 0.0
    pair_mat = jnp.where((_iota((n_cmp, n_lane), 0) >> 1) == _iota((n_cmp, n_lane), 1), 1.0, 0.0)
    imps = []
    for i in range(nb):
        for g in range(NSA_KV):
            q8 = _pad_rows8(q_ref[i, g * NSA_HPG:(g + 1) * NSA_HPG, :])
            s = _dot_nt(q8, kvc_ref[i, 0, g].astype(BF16))
            p = _softmax_rows(jnp.where(valid, s, NEG)) * any_valid
            o_c = _dot(p.astype(BF16), kvc_ref[i, 1, g].astype(BF16))
            oc_ref[i, g * NSA_HPG:(g + 1) * NSA_HPG, :] = o_c[0:NSA_HPG]
            imps.append(p[0:1] + p[1:2] + p[2:3] + p[3:4])
        imps.append(jnp.zeros((SUBLANES - NSA_KV, n_cmp), F32))
    rows = nb * SUBLANES
    imp = jnp.concatenate(imps, axis=0)
    pair = _dot_f32(imp, pair_mat)
    blk = _iota((rows, n_lane), 1)
    forced = (blk == 0) | (blk == qpos // L_SEL)
    avail = blk * L_SEL <= qpos
    val = jnp.where(forced, FORCE, jnp.where(avail, pair, -1.0))
    val = jnp.where(blk < n_sel, val, -2.0)
    out = jnp.zeros((rows, LANES), I32)
    out_lane = _iota((rows, LANES), 1)
    for r in range(N_SEL):
        m = jnp.max(val, axis=-1, keepdims=True)
        first = jnp.min(jnp.where(val == m, blk, n_lane), axis=-1, keepdims=True)
        out = jnp.where(out_lane == r, first, out)
        val = jnp.where(blk == first, -3.0, val)
    idx_ref[...] = out.reshape(nb, SUBLANES, LANES)


def _nsa_cmp_topk_sample(q3, kvc, qpos, n_sel):
    n_b = q3.shape[0]
    n_cmp = kvc.shape[3]
    nb = _TOPK_ROWS_PER_STEP if n_b % _TOPK_ROWS_PER_STEP == 0 else 1
    return pl.pallas_call(
        functools.partial(_nsa_cmp_topk_sample_kernel, qpos=qpos, n_sel=n_sel),
        grid=(n_b // nb,),
        in_specs=[pl.BlockSpec((nb, NSA_HEADS, NSA_DH), lambda b: (b, 0, 0)),
                  pl.BlockSpec((nb, 2, NSA_KV, n_cmp, NSA_DH), lambda b: (b, 0, 0, 0, 0))],
        out_specs=[pl.BlockSpec((nb, NSA_HEADS, NSA_DH), lambda b: (b, 0, 0)),
                   pl.BlockSpec((nb, SUBLANES, LANES), lambda b: (b, 0, 0))],
        out_shape=[jax.ShapeDtypeStruct((n_b, NSA_HEADS, NSA_DH), F32),
                   jax.ShapeDtypeStruct((n_b, SUBLANES, LANES), I32)],
        compiler_params=_cparams("parallel"),
        name="nsa_cmp_topk_sample",
    )(q3, kvc)


def _nsa_gather_sel_kernel(pt_ref, idx_ref, *refs):
    blocks = refs[:NSA_KV * N_SEL]
    o_ref = refs[NSA_KV * N_SEL]
    for g in range(NSA_KV):
        for kv in range(2):
            pages = [blocks[g * N_SEL + r][kv] for r in range(N_SEL)]
            o_ref[g, kv] = jnp.concatenate(pages, axis=1).astype(BF16)


def _nsa_gather_sel(cache_t, layer, page_table, idx):
    n_b, n_pages = page_table.shape
    past = n_pages * PAGE_SIZE
    halves = PAGE_SIZE // L_SEL
    last_blk = past // L_SEL - 1

    def blk_spec(n):
        g = n // N_SEL

        def index(b, pt, ix):
            s = jnp.clip(ix[b, n], 0, last_blk)
            return (layer, pt[b, s // halves], 0, g, 0, 0)
        return pl.BlockSpec((None, None, 2, None, NSA_DH, PAGE_SIZE), index)

    grid_spec = pltpu.PrefetchScalarGridSpec(
        num_scalar_prefetch=2,
        grid=(n_b,),
        in_specs=[blk_spec(n) for n in range(NSA_KV * N_SEL)],
        out_specs=pl.BlockSpec((None, NSA_KV, 2, NSA_DH, N_SEL * PAGE_SIZE), lambda b, pt, ix: (b, 0, 0, 0, 0)))
    return pl.pallas_call(
        _nsa_gather_sel_kernel,
        grid_spec=grid_spec,
        out_shape=jax.ShapeDtypeStruct((n_b, NSA_KV, 2, NSA_DH, N_SEL * PAGE_SIZE), BF16),
        name="nsa_gather_sel",
    )(page_table, idx, *([cache_t] * (NSA_KV * N_SEL)))


def _attend_with_new(q8, k_t, v_t, live, k_new, v_new, new_live):
    s = jnp.where(live > 0.5, _dot(q8, k_t), NEG)
    k_new16 = k_new.astype(BF16).astype(F32)
    s_new = jnp.sum(q8.astype(F32) * k_new16, axis=-1, keepdims=True)
    s_new = jnp.where(new_live > 0.5, s_new, NEG)
    m = jnp.maximum(jnp.max(s, axis=-1, keepdims=True), s_new)
    e = jnp.exp(s - m)
    e_new = jnp.exp(s_new - m)
    inv = 1.0 / (jnp.sum(e, axis=-1, keepdims=True) + e_new)
    p16 = (e * inv).astype(BF16)
    p_new16 = (e_new * inv).astype(BF16).astype(F32)
    return _dot_nt(p16, v_t) + p_new16 * v_new.astype(BF16).astype(F32)


def _row_to_col(row):
    return jnp.broadcast_to(row, (SUBLANES, row.shape[1])).T[:, 0:1]


def _nsa_sel_win_sample_kernel(idx_ref, qr_ref, oc_ref, gate_ref, selnew_ref, winnew_ref, win_ref, kv_ref,
                               o_ref, wout_ref, *, qpos, past):
    gates = gate_ref[...]
    glane = _iota((1, LANES), 1)
    n_key = N_SEL * PAGE_SIZE
    lane_k = _iota((SUBLANES, n_key), 1)
    log_page = PAGE_SIZE.bit_length() - 1
    wb = win_ref.shape[3]
    wlane = _iota((SUBLANES, wb), 1)
    kwpos = past - wb + wlane
    dist = qpos - kwpos
    wlive = jnp.where((dist >= 0) & (dist <= WINDOW) & (kwpos >= 0), 1.0, 0.0)
    new_in_window = jnp.full((1, 1), 1.0 if 0 <= qpos - past <= WINDOW else 0.0, F32)
    expand = jnp.where(_iota((LANES, n_key), 0) == (_iota((LANES, n_key), 1) >> log_page), 1.0, 0.0)
    idx_f = idx_ref[...].astype(F32)
    blk_of_lane = _dot_f32(jnp.where(_iota((SUBLANES, LANES), 1) < N_SEL, idx_f, 0.0), expand)
    sel_new = selnew_ref[...]
    win_new = winnew_ref[...]
    last_lane = _iota((NSA_DH, wb), 1) == wb - 1
    for g in range(NSA_KV):
        kc = slice(g * NSA_DH, (g + 1) * NSA_DH)
        vc = slice(NSA_KVW + g * NSA_DH, NSA_KVW + (g + 1) * NSA_DH)
        q8 = _pad_rows8(qr_ref[g * NSA_HPG:(g + 1) * NSA_HPG, :])
        blk_g = blk_of_lane[g:g + 1, :]
        new_blk = float(qpos // L_SEL)
        new_selected = jnp.max(jnp.where(blk_g == new_blk, 1.0, 0.0), axis=-1, keepdims=True)
        halves = PAGE_SIZE // L_SEL
        page_g = jnp.floor(jnp.minimum(blk_g, float(past // L_SEL - 1)) * (1.0 / halves))
        kpos = page_g * float(PAGE_SIZE) + (lane_k & (PAGE_SIZE - 1)).astype(F32)
        in_block = jnp.floor(kpos * (1.0 / L_SEL)) == blk_g
        live = jnp.where(in_block & (kpos < float(past)) & (kpos <= float(qpos)), 1.0, 0.0)
        o_s = _attend_with_new(q8, kv_ref[g, 0], kv_ref[g, 1], live, sel_new[:, kc], sel_new[:, vc], new_selected)
        kw_t = win_ref[0, g]
        vw_t = win_ref[1, g]
        o_w = _attend_with_new(q8, kw_t.astype(BF16), vw_t.astype(BF16), wlive, win_new[:, kc], win_new[:, vc],
                               new_in_window)
        wout_ref[0, g] = jnp.where(last_lane, _row_to_col(win_new[:, kc]), pltpu.roll(kw_t, wb - 1, 1))
        wout_ref[1, g] = jnp.where(last_lane, _row_to_col(win_new[:, vc]), pltpu.roll(vw_t, wb - 1, 1))
        o_c = oc_ref[g * NSA_HPG:(g + 1) * NSA_HPG, :]
        rows = []
        for j in range(NSA_HPG):
            acc = jnp.zeros((1, NSA_DH), F32)
            for br, ob in enumerate((o_c, o_s, o_w)):
                col = br * NSA_HEADS + g * NSA_HPG + j
                gcol = jnp.sum(jnp.where(glane == col, gates, 0.0), axis=-1, keepdims=True)
                acc = acc + gcol * ob[j:j + 1, :]
            rows.append(acc)
        o_ref[g * NSA_HPG:(g + 1) * NSA_HPG, :] = jnp.concatenate(rows, axis=0)


def _nsa_sel_win_sample(kv_sel, idx8, qr3, oc3, gates, sel_new, win_new, win_t, qpos, past):
    n_b = qr3.shape[0]
    wb = win_t.shape[-1]
    b3 = lambda b: (b, 0, 0)
    b5 = lambda b: (b, 0, 0, 0, 0)
    wspec = pl.BlockSpec((None, 2, NSA_KV, NSA_DH, wb), b5)
    return pl.pallas_call(
        functools.partial(_nsa_sel_win_sample_kernel, qpos=qpos, past=past),
        grid=(n_b,),
        in_specs=[pl.BlockSpec((None, SUBLANES, LANES), b3),
                  pl.BlockSpec((None, NSA_HEADS, NSA_DH), b3), pl.BlockSpec((None, NSA_HEADS, NSA_DH), b3),
                  pl.BlockSpec((None, 1, LANES), b3), pl.BlockSpec((None, 1, 2 * NSA_KVW), b3),
                  pl.BlockSpec((None, 1, 2 * NSA_KVW), b3), wspec,
                  pl.BlockSpec((None, NSA_KV, 2, NSA_DH, N_SEL * PAGE_SIZE), b5)],
        out_specs=[pl.BlockSpec((None, NSA_HEADS, NSA_DH), b3), wspec],
        out_shape=[jax.ShapeDtypeStruct((n_b, NSA_HEADS, NSA_DH), F32), jax.ShapeDtypeStruct(win_t.shape, F32)],
        compiler_params=_cparams("parallel"),
        name="nsa_sel_win_sample",
    )(idx8, qr3, oc3, gates.reshape(n_b, 1, LANES), sel_new.reshape(n_b, 1, -1),
      win_new.reshape(n_b, 1, -1), win_t, kv_sel)


def _nsa_sample_layer(x, layer, norm_g, w_in, cmp_pe, cmp_w, w_out, cache_cmp, cache_sel, win_state, page_table,
                      tables):
    n_b = x.shape[0]
    past = page_table.shape[1] * PAGE_SIZE
    qpos = past
    q, qr, cmp_rows, sel_rows, win_rows, gates, _ = _nsa_proj(
        x, norm_g.reshape(1, D_MODEL), _nsa_w_in_padded(w_in), tables, 1, n_b)
    bd, pe2 = _cmp_weights(cmp_w, cmp_pe)
    kvc = _nsa_compress_sample(cache_cmp, layer, page_table, cmp_rows, bd, pe2)
    n_sel = -(-(past + 1) // L_SEL)
    oc3, idx8 = _nsa_cmp_topk_sample(q.reshape(n_b, NSA_HEADS, NSA_DH), kvc, qpos, n_sel)
    idx = idx8[:, :NSA_KV, :N_SEL].reshape(n_b, NSA_KV * N_SEL)
    kv_sel = _nsa_gather_sel(cache_sel, layer, page_table, idx)
    o3, win_out = _nsa_sel_win_sample(kv_sel, idx8, qr.reshape(n_b, NSA_HEADS, NSA_DH), oc3, gates,
                                      sel_rows, win_rows, win_state, qpos, past)
    y = _matmul_residual(o3.reshape(n_b, NSA_Q), w_out.astype(BF16), x, n_b)
    return y, cmp_rows, sel_rows, win_out


def _gdn_sample_pre_kernel(z_ref, b0_ref, b1_ref, b2_ref, cw_ref, alog_ref, dtb_ref, q_ref, k_ref, v_ref, gb_ref):
    outs = (q_ref, k_ref, v_ref)
    for c in range(GDN_CONV_CH // _GDN_CW):
        c0 = c * _GDN_CW
        cols = slice(c0, c0 + _GDN_CW)
        w = cw_ref[:, cols]
        y = b0_ref[:, cols] * w[0:1] + b1_ref[:, cols] * w[1:2] + b2_ref[:, cols] * w[2:3] + z_ref[:, cols] * w[3:4]
        o_ref = outs[c0 // GDN_QK]
        o_ref[:, c0 % GDN_QK:c0 % GDN_QK + _GDN_CW] = _gdn_post_conv(y, c0)
    gb_ref[...] = _gdn_gates(z_ref[:, GDN_CONV_CH + GDN_VW:_GDN_W_COLS], alog_ref[...], dtb_ref[...])


def _gdn_sample_pre(z, bufs, conv_w, alog, dtb):
    n = z.shape[0]
    f = jax.ShapeDtypeStruct((n, GDN_QK), F32)
    return pl.pallas_call(
        _gdn_sample_pre_kernel,
        out_shape=[f, f, f, jax.ShapeDtypeStruct((n, LANES), F32)],
        compiler_params=pltpu.CompilerParams(vmem_limit_bytes=VMEM_LIMIT),
        name="gdn_sample_pre",
    )(z, *bufs, conv_w, alog, dtb)


def _gdn_sample_state_kernel(s0_ref, qt_ref, kt_ref, v_ref, gb_ref, gate_ref, ng_ref, s_ref, o_ref):
    gb = gb_ref[...]
    lane = _iota((1, LANES), 1)
    ng = ng_ref[...]
    for h in range(GDN_HEADS):
        cols = slice(h * GDN_DV, (h + 1) * GDN_DV)
        g_h = jnp.sum(jnp.where(lane == h, gb, 0.0), axis=-1, keepdims=True)
        beta = jnp.sum(jnp.where(lane == GDN_HEADS + h, gb, 0.0), axis=-1, keepdims=True)
        eg = jnp.exp(g_h)
        s_prev = s0_ref[h]
        k_col = kt_ref[:, h:h + 1]
        q_col = qt_ref[:, h:h + 1]
        v_row = v_ref[:, cols]
        u = v_row * beta - jnp.sum(s_prev * (k_col * beta * eg), axis=0, keepdims=True)
        attn = jnp.sum(q_col * k_col, axis=0, keepdims=True)
        o = jnp.sum(s_prev * (q_col * eg), axis=0, keepdims=True) + attn * u
        s_ref[h] = s_prev * eg + k_col * u
        o_ref[:, cols] = (_rmsnorm(o, ng) * jax.nn.silu(gate_ref[:, cols])).astype(BF16)


def _gdn_sample_state(s0, q_t, k_t, v, gb, gate, head_norm_g):
    n_b = s0.shape[0]
    st = pl.BlockSpec((None, GDN_HEADS, GDN_DK, GDN_DV), lambda b: (b, 0, 0, 0))
    col = pl.BlockSpec((None, GDN_DK, GDN_HEADS), lambda b: (b, 0, 0))
    wide = pl.BlockSpec((None, 1, GDN_VW), lambda b: (b, 0, 0))
    return pl.pallas_call(
        _gdn_sample_state_kernel,
        grid=(n_b,),
        in_specs=[st, col, col, wide, pl.BlockSpec((None, 1, LANES), lambda b: (b, 0, 0)), wide,
                  _resident((1, GDN_DV))],
        out_specs=[st, wide],
        out_shape=[jax.ShapeDtypeStruct(s0.shape, F32), jax.ShapeDtypeStruct((n_b, 1, GDN_VW), BF16)],
        compiler_params=_cparams("parallel"),
        name="gdn_sample_state",
    )(s0, q_t, k_t, v.reshape(n_b, 1, GDN_VW), gb.reshape(n_b, 1, LANES), gate.reshape(n_b, 1, GDN_VW),
      head_norm_g.reshape(1, GDN_DV))


def _gdn_sample_layer(x, norm_g, s0, conv_buf, w_in, conv_w, a_log, dt_bias, head_norm_g, w_out):
    n_b = x.shape[0]
    z = _norm_matmul_small(x, norm_g, _gdn_w_in_padded(w_in), 3)
    bufs = [conv_buf[:, j, :] for j in range(GDN_CONV - 1)]
    q, k, v, gb = _gdn_sample_pre(z, bufs, conv_w, _lane_row(a_log), _lane_row(dt_bias))
    to_cols = lambda t: t.reshape(n_b, GDN_HEADS, GDN_DK).transpose(0, 2, 1)
    gate = z[:, GDN_CONV_CH:GDN_CONV_CH + GDN_VW]
    s_new, o = _gdn_sample_state(s0, to_cols(q), to_cols(k), v, gb, gate, head_norm_g)
    y = _matmul_residual(o.reshape(n_b, GDN_VW), w_out.astype(BF16), x, n_b)
    new_buf = jnp.concatenate([conv_buf[:, 1:, :], z[:, None, :GDN_CONV_CH]], axis=1)
    return y, s_new, new_buf


def _lru_sample_kernel(z_ref, b0_ref, b1_ref, b2_ref, cw_ref, cb_ref, wga_ref, bga_ref, wgx_ref, bgx_ref, lam_ref,
                       h0_ref, h_ref, y_ref):
    for c in range(D_RNN // _LRU_CW):
        c0 = c * _LRU_CW
        cols = slice(c0, c0 + _LRU_CW)
        w = cw_ref[:, cols]
        xc = (b0_ref[:, cols] * w[0:1] + b1_ref[:, cols] * w[1:2] + b2_ref[:, cols] * w[2:3]
              + z_ref[:, D_RNN + c0:D_RNN + c0 + _LRU_CW] * w[3:4] + cb_ref[:, cols])
        a, b = _lru_gates(xc, wga_ref, bga_ref[:, cols], wgx_ref, bgx_ref[:, cols], lam_ref[:, cols], c0 // LRU_BW)
        h = a * h0_ref[:, cols] + b
        h_ref[:, cols] = h
        y_ref[:, cols] = (h * jax.nn.gelu(z_ref[:, cols])).astype(BF16)


def _lru_sample_layer(x, norm_g, h0, conv_buf, w_in, conv_w, conv_b, w_ga, b_ga, w_gx, b_gx, lam, w_out):
    n_b = x.shape[0]
    z = _norm_matmul_small(x, norm_g, w_in.astype(BF16), 2)
    r1 = lambda v: v.reshape(1, D_RNN)
    bufs = [conv_buf[:, j, :] for j in range(LRU_CONV - 1)]
    h, y = pl.pallas_call(
        _lru_sample_kernel,
        out_shape=[jax.ShapeDtypeStruct((n_b, D_RNN), F32), jax.ShapeDtypeStruct((n_b, D_RNN), BF16)],
        compiler_params=pltpu.CompilerParams(vmem_limit_bytes=VMEM_LIMIT),
        name="lru_sample",
    )(z, *bufs, conv_w, r1(conv_b), w_ga.astype(BF16), r1(b_ga), w_gx.astype(BF16), r1(b_gx), r1(lam), h0)
    y = _matmul_residual(y, w_out.astype(BF16), x, n_b)
    new_buf = jnp.concatenate([conv_buf[:, 1:, :], z[:, None, D_RNN:]], axis=1)
    return y, h, new_buf


def _ffn_sample_kernel(x_ref, g_ref, wv_ref, wg_ref, cwv_ref, cwg_ref, b0v_ref, b0g_ref, b1v_ref, b1g_ref, wout_ref,
                       y_ref, uv_ref, ug_ref, xn_scr):
    c = pl.program_id(0)

    @pl.when(c == 0)
    def _():
        x = x_ref[...]
        xn_scr[...] = _rmsnorm(x, g_ref[...]).astype(BF16)
        y_ref[...] = x

    xn = xn_scr[...]
    uv = _dot(xn, wv_ref[...])
    ug = _dot(xn, wg_ref[...])
    uv_ref[...] = uv
    ug_ref[...] = ug
    cwv, cwg = cwv_ref[...], cwg_ref[...]
    val = b0v_ref[...] * cwv[0:1] + b1v_ref[...] * cwv[1:2] + uv * cwv[2:3]
    gt = b0g_ref[...] * cwg[0:1] + b1g_ref[...] * cwg[1:2] + ug * cwg[2:3]
    y_ref[...] += _dot((val * jax.nn.silu(gt)).astype(BF16), wout_ref[...])


def _ffn_sample(x, norm_g, w_in, conv_w, w_out, buf):
    n_b = x.shape[0]
    n_c = D_FF // _FFN_CW
    b0, b1 = buf[:, 0, :], buf[:, 1, :]
    val = lambda shape0: pl.BlockSpec((shape0, _FFN_CW), lambda c: (0, c))
    gat = lambda shape0: pl.BlockSpec((shape0, _FFN_CW), lambda c: (0, n_c + c))
    y, uv, ug = pl.pallas_call(
        _ffn_sample_kernel,
        grid=(n_c,),
        in_specs=[_resident((n_b, D_MODEL)), _resident((1, D_MODEL)), val(D_MODEL), gat(D_MODEL),
                  val(FFN_CONV), gat(FFN_CONV), val(n_b), gat(n_b), val(n_b), gat(n_b),
                  pl.BlockSpec((_FFN_CW, D_MODEL), lambda c: (c, 0))],
        out_specs=[pl.BlockSpec((n_b, D_MODEL), lambda c: (0, 0)), val(n_b), val(n_b)],
        out_shape=[jax.ShapeDtypeStruct((n_b, D_MODEL), F32), jax.ShapeDtypeStruct((n_b, D_FF), F32),
                   jax.ShapeDtypeStruct((n_b, D_FF), F32)],
        scratch_shapes=[pltpu.VMEM((n_b, D_MODEL), BF16)],
        compiler_params=_cparams("arbitrary"),
        name="ffn_sample",
    )(x, norm_g.reshape(1, D_MODEL), w_in, w_in, conv_w, conv_w, b0, b0, b1, b1, w_out)
    new_buf = jnp.stack([b1, jnp.concatenate([uv, ug], axis=-1)], axis=1)
    return y, new_buf


def kernel(x_prompt, x_sample, cache_nsa_cmp, cache_nsa_sel, state_nsa_win, state_gdn_S, state_gdn_conv,
           state_lru_h, state_lru_conv, state_ffn_conv, page_table,
           norm_mix_g, norm_ffn_g, norm_final_g,
           nsa_w_in, nsa_cmp_pe, nsa_cmp_w, nsa_w_out,
           gdn_w_in, gdn_conv_w, gdn_A_log, gdn_dt_bias, gdn_norm_g, gdn_w_out,
           lru_w_in, lru_conv_w, lru_conv_b, lru_w_ga, lru_b_ga, lru_w_gx, lru_b_gx, lru_lambda, lru_w_out,
           ffn_w_in, ffn_conv_w, ffn_w_out):
    n_p, t_len, _ = x_prompt.shape
    n_s = x_sample.shape[0]
    past = page_table.shape[1] * PAGE_SIZE
    wl = min(WINDOW, t_len)
    xp = x_prompt.reshape(n_p * t_len, D_MODEL)
    xs = x_sample.reshape(n_s, D_MODEL)
    tab_p = _rope_tables(jnp.arange(t_len, dtype=I32))
    tab_s = _rope_tables(jnp.full((n_s,), past, I32))
    cache_cmp_t = _rows_last(cache_nsa_cmp)
    cache_sel_t = _rows_last(cache_nsa_sel)
    win_t = _rows_last(state_nsa_win)
    ffn_w_in16 = ffn_w_in.astype(BF16)
    ffn_w_out16 = ffn_w_out.astype(BF16)
    zeros = lambda *shape: jnp.zeros(shape, F32)
    kv5 = lambda rows, nb: rows.reshape(nb, -1, 2, NSA_KV, NSA_DH)

    p_nsa, s_nsa, p_gdn, s_gdn, p_lru, s_lru, p_ffn, s_ffn = [], [], [], [], [], [], [], []
    for li in range(DEPTH):
        j = li // N_MIXERS
        kind = li % N_MIXERS
        if kind == 0:
            xp, c, s, w = _nsa_prompt_layer(xp, n_p, norm_mix_g[li], nsa_w_in[j], nsa_cmp_pe[j], nsa_cmp_w[j],
                                            nsa_w_out[j], tab_p)
            rows_first = lambda a: jnp.transpose(a, (0, 4, 1, 2, 3))
            p_nsa.append((rows_first(c), rows_first(s), rows_first(w[..., t_len - wl:])))
            xs, c, s, w = _nsa_sample_layer(xs, j, norm_mix_g[li], nsa_w_in[j], nsa_cmp_pe[j], nsa_cmp_w[j],
                                            nsa_w_out[j], cache_cmp_t, cache_sel_t, win_t[j], page_table, tab_s)
            s_nsa.append((kv5(c, n_s), kv5(s, n_s), jnp.transpose(w, (0, 4, 1, 2, 3))))
        elif kind == 1:
            args = (gdn_w_in[j], gdn_conv_w[j], gdn_A_log[j], gdn_dt_bias[j], gdn_norm_g[j], gdn_w_out[j])
            xp, s_new, buf = _gdn_prompt_layer(xp, n_p, norm_mix_g[li], zeros(n_p, GDN_HEADS, GDN_DK, GDN_DV),
                                               zeros(n_p, GDN_CONV - 1, GDN_CONV_CH), *args)
            p_gdn.append((s_new, buf))
            xs, s_new, buf = _gdn_sample_layer(xs, norm_mix_g[li], state_gdn_S[j], state_gdn_conv[j], *args)
            s_gdn.append((s_new, buf))
        else:
            args = (lru_w_in[j], lru_conv_w[j], lru_conv_b[j], lru_w_ga[j], lru_b_ga[j], lru_w_gx[j], lru_b_gx[j],
                    lru_lambda[j], lru_w_out[j])
            xp, h, buf = _lru_prompt_layer(xp, n_p, norm_mix_g[li], zeros(n_p, D_RNN), zeros(n_p, LRU_CONV - 1, D_RNN),
                                           *args)
            p_lru.append((h, buf))
            xs, h, buf = _lru_sample_layer(xs, norm_mix_g[li], state_lru_h[j], state_lru_conv[j], *args)
            s_lru.append((h, buf))
        xp, buf = _ffn_prompt(xp, n_p, norm_ffn_g[li], ffn_w_in16[li], ffn_conv_w[li], ffn_w_out16[li],
                              zeros(n_p, FFN_CONV - 1, 2 * D_FF))
        p_ffn.append(buf)
        xs, buf = _ffn_sample(xs, norm_ffn_g[li], ffn_w_in16[li], ffn_conv_w[li], ffn_w_out16[li], state_ffn_conv[li])
        s_ffn.append(buf)

    y_prompt = _final_norm(xp, norm_final_g, 512).reshape(n_p, t_len, D_MODEL)
    y_sample = _final_norm(xs, norm_final_g, n_s).reshape(n_s, 1, D_MODEL)
    stack = lambda entries, k: jnp.stack([e[k] for e in entries])
    return (y_prompt, y_sample,
            stack(p_nsa, 0), stack(s_nsa, 0), stack(p_nsa, 1), stack(s_nsa, 1), stack(p_nsa, 2), stack(s_nsa, 2),
            stack(p_gdn, 0), stack(s_gdn, 0), stack(p_gdn, 1), stack(s_gdn, 1),
            stack(p_lru, 0), stack(s_lru, 0), stack(p_lru, 1), stack(s_lru, 1),
            jnp.stack(p_ffn), jnp.stack(s_ffn))
```

```python
import functools

import jax
import jax.numpy as jnp
from jax import lax
from jax.experimental import pallas as pl
from jax.experimental.pallas import tpu as pltpu

F32 = jnp.float32
BF16 = jnp.bfloat16
I32 = jnp.int32

D_MODEL = 1024
DEPTH = 4
PAST_LEN = 8192
PAGE_SIZE = 128
N_MIXERS = 3

NSA_DH = 64
NSA_HEADS = 16
NSA_HPG = 4
NSA_KV = 4
NSA_Q = NSA_HEADS * NSA_DH
NSA_KVW = NSA_KV * NSA_DH
ROT_DIM = NSA_DH // 4
ROPE_THETA = 500000.0
L_CMP = 32
L_SEL = 64
N_SEL = 16
WINDOW = 512
_LOG_SEL = 6
NSA_SCALE = NSA_DH ** -0.5

GDN_DK = 128
GDN_DV = 128
GDN_HEADS = 8
GDN_QK = GDN_HEADS * GDN_DK
GDN_VW = GDN_HEADS * GDN_DV
GDN_CONV = 4
GDN_CONV_CH = 2 * GDN_QK + GDN_VW
GDN_CHUNK = 64

D_RNN = D_MODEL
LRU_BLOCKS = 8
LRU_BW = D_RNN // LRU_BLOCKS
LRU_CONV = 4
LRU_C = 8.0

D_FF = 2816
FFN_CONV = 3

EPS = 1e-6
NEG = -1e30
FORCE = 1e4

LANES = 128
SUBLANES = 8
MXU_DIM = 256
VMEM_LIMIT = 56 * 1024 * 1024


def _cparams(*sem):
    return pltpu.CompilerParams(dimension_semantics=sem, vmem_limit_bytes=VMEM_LIMIT)


def _resident(shape):
    nd = len(shape)
    return pl.BlockSpec(shape, lambda *_: (0,) * nd, pipeline_mode=pl.Buffered(1))


def _rmsnorm(x, g):
    return x * lax.rsqrt(jnp.mean(x * x, axis=-1, keepdims=True) + EPS) * g


def _dot(a, b):
    return jnp.dot(a, b, preferred_element_type=F32)


def _dot_nt(a, b):
    return lax.dot_general(a, b, (((1,), (1,)), ((), ())), preferred_element_type=F32)


def _dot_tn(a, b):
    return lax.dot_general(a, b, (((0,), (0,)), ((), ())), preferred_element_type=F32)


def _dot_f32(a, b):
    return jnp.dot(a, b, precision=lax.Precision.HIGHEST, preferred_element_type=F32)


def _iota(shape, axis):
    return lax.broadcasted_iota(I32, shape, axis)


def _rope_tables(pos):
    half = ROT_DIM // 2
    inv = ROPE_THETA ** (-jnp.arange(half, dtype=F32) / half)
    ang = pos.astype(F32)[:, None] * inv[None, :]
    lane = jnp.arange(LANES) % NSA_DH
    c = jnp.cos(ang)[:, lane % half]
    s = jnp.sin(ang)[:, lane % half]
    cos = jnp.where(lane < ROT_DIM, c, 1.0)
    sin_a = jnp.where((lane >= half) & (lane < ROT_DIM), s, 0.0)
    sin_b = jnp.where(lane < half, -s, 0.0)
    return cos.astype(F32), sin_a.astype(F32), sin_b.astype(F32)


def _rope(x, cos, sin_a, sin_b):
    half = ROT_DIM // 2
    outs = []
    for c in range(x.shape[1] // LANES):
        xc = x[:, c * LANES:(c + 1) * LANES]
        outs.append(xc * cos + pltpu.roll(xc, half, 1) * sin_a + pltpu.roll(xc, LANES - half, 1) * sin_b)
    return outs[0] if len(outs) == 1 else jnp.concatenate(outs, axis=1)


_NSA_W_COLS = NSA_Q + 6 * NSA_KVW + LANES


def _store_seq_last(o_ref, rows):
    for c in range(2 * NSA_KVW // LANES):
        t = rows[:, c * LANES:(c + 1) * LANES].T
        kv, g0 = c // 2, 2 * (c % 2)
        o_ref[kv, g0] = t[:NSA_DH]
        o_ref[kv, g0 + 1] = t[NSA_DH:]


def _nsa_proj_kernel(x_ref, g_ref, w_ref, cos_ref, sa_ref, sb_ref,
                     q_ref, qr_ref, cmp_ref, sel_ref, win_ref, gate_ref, kvb_ref, *seq_last_refs):
    xn = _rmsnorm(x_ref[...], g_ref[...]).astype(BF16)
    cos, sa, sb = cos_ref[...], sa_ref[...], sb_ref[...]

    def mm(c0, c1):
        return _dot(xn, w_ref[:, c0:c1])

    zq = mm(0, NSA_Q)
    q_ref[...] = (zq * NSA_SCALE).astype(BF16)
    qr_ref[...] = (_rope(zq, cos, sa, sb) * NSA_SCALE).astype(BF16)
    c0 = NSA_Q
    zc = mm(c0, c0 + 2 * NSA_KVW)
    cmp_ref[...] = zc
    if seq_last_refs:
        _store_seq_last(seq_last_refs[0], zc)
    c0 += 2 * NSA_KVW
    for kind, o_ref in ((0, sel_ref), (1, win_ref)):
        z = mm(c0, c0 + 2 * NSA_KVW)
        kr = _rope(z[:, :NSA_KVW], cos, sa, sb)
        v = z[:, NSA_KVW:]
        o_ref[:, :NSA_KVW] = kr
        o_ref[:, NSA_KVW:] = v
        if seq_last_refs:
            _store_seq_last(seq_last_refs[1 + kind], o_ref[...])
        for g in range(NSA_KV):
            kvb_ref[2 * kind, g] = kr[:, g * NSA_DH:(g + 1) * NSA_DH].astype(BF16)
            kvb_ref[2 * kind + 1, g] = v[:, g * NSA_DH:(g + 1) * NSA_DH].astype(BF16)
        c0 += 2 * NSA_KVW
    gate_ref[...] = jax.nn.sigmoid(mm(c0, c0 + LANES))


def _nsa_proj(x, g, w, tables, n_batch, tm, seq_last=False):
    n = x.shape[0]
    t_len = n // n_batch
    tpb = t_len // tm
    row = lambda i: (i, 0)
    tab = pl.BlockSpec((tm, LANES), lambda i: (i % tpb, 0))
    out_specs = [pl.BlockSpec((tm, NSA_Q), row), pl.BlockSpec((tm, NSA_Q), row),
                 pl.BlockSpec((tm, 2 * NSA_KVW), row), pl.BlockSpec((tm, 2 * NSA_KVW), row),
                 pl.BlockSpec((tm, 2 * NSA_KVW), row), pl.BlockSpec((tm, LANES), row),
                 pl.BlockSpec((None, 4, NSA_KV, tm, NSA_DH), lambda i: (i // tpb, 0, 0, i % tpb, 0))]
    out_shape = [jax.ShapeDtypeStruct((n, NSA_Q), BF16), jax.ShapeDtypeStruct((n, NSA_Q), BF16),
                 jax.ShapeDtypeStruct((n, 2 * NSA_KVW), F32), jax.ShapeDtypeStruct((n, 2 * NSA_KVW), F32),
                 jax.ShapeDtypeStruct((n, 2 * NSA_KVW), F32), jax.ShapeDtypeStruct((n, LANES), F32),
                 jax.ShapeDtypeStruct((n_batch, 4, NSA_KV, t_len, NSA_DH), BF16)]
    if seq_last:
        out_specs += [pl.BlockSpec((None, 2, NSA_KV, NSA_DH, tm), lambda i: (i // tpb, 0, 0, 0, i % tpb))] * 3
        out_shape += [jax.ShapeDtypeStruct((n_batch, 2, NSA_KV, NSA_DH, t_len), F32)] * 3
    return pl.pallas_call(
        _nsa_proj_kernel,
        grid=(n // tm,),
        in_specs=[pl.BlockSpec((tm, D_MODEL), row), _resident((1, D_MODEL)), _resident((D_MODEL, _NSA_W_COLS)),
                  tab, tab, tab],
        out_specs=out_specs,
        out_shape=out_shape,
        compiler_params=_cparams("parallel"),
        name="nsa_proj",
    )(x, g, w, *tables)


def _nsa_w_in_padded(w_in):
    pad = _NSA_W_COLS - w_in.shape[1]
    return jnp.pad(w_in, ((0, 0), (0, pad))).astype(BF16)


def _cmp_weights(cmp_w, cmp_pe):
    eye = jnp.eye(NSA_KV, dtype=cmp_w.dtype)
    bd = jnp.einsum("gh,kldx->klgdhx", eye, cmp_w).reshape(2, L_CMP, NSA_KVW, NSA_KVW)
    return bd.astype(BF16), jnp.tile(cmp_pe, (1, 1, NSA_KV))


def _compress_planes(load_rows, bd_ref, pe_ref, o_ref, n_rows_out):
    for kv in range(2):
        acc = jnp.zeros((n_rows_out, NSA_KVW), F32)
        bias = jnp.zeros((SUBLANES, NSA_KVW), F32)
        for l in range(L_CMP):
            w = bd_ref[kv, l]
            acc = acc + _dot(load_rows(kv, l).astype(BF16), w)
            pe = jnp.broadcast_to(pe_ref[kv, l:l + 1, :], (SUBLANES, NSA_KVW)).astype(BF16)
            bias = bias + _dot(pe, w)
        out = acc + bias[0:1, :]
        for g in range(NSA_KV):
            o_ref[kv, g] = out[:, g * NSA_DH:(g + 1) * NSA_DH]


def _nsa_compress_prompt_kernel(x0_ref, x1_ref, x2_ref, x3_ref, bd_ref, pe_ref, o_ref, *, n_blk, n_pad):
    planes = (x0_ref, x1_ref, x2_ref, x3_ref)
    if n_pad > n_blk:
        o_ref[...] = jnp.zeros(o_ref.shape, F32)

    def load_rows(kv, l):
        return jnp.concatenate([planes[2 * kv + i][pl.ds(l, n_blk, stride=L_CMP), :] for i in range(2)], axis=1)

    if n_pad == n_blk:
        _compress_planes(load_rows, bd_ref, pe_ref, o_ref, n_blk)
    else:
        _compress_planes(load_rows, bd_ref, pe_ref, o_ref.at[:, :, 0:n_blk, :], n_blk)


def _nsa_compress_prompt(cmp_rows, bd, pe2, n_batch):
    n = cmp_rows.shape[0]
    t_len = n // n_batch
    n_blk = t_len // L_CMP
    n_pad = max(n_blk, LANES)
    planes = [pl.BlockSpec((t_len, LANES), functools.partial(lambda b, c: (b, c), c=c)) for c in range(4)]
    return pl.pallas_call(
        functools.partial(_nsa_compress_prompt_kernel, n_blk=n_blk, n_pad=n_pad),
        grid=(n_batch,),
        in_specs=planes + [_resident(bd.shape), _resident(pe2.shape)],
        out_specs=pl.BlockSpec((None, 2, NSA_KV, n_pad, NSA_DH), lambda b: (b, 0, 0, 0, 0)),
        out_shape=jax.ShapeDtypeStruct((n_batch, 2, NSA_KV, n_pad, NSA_DH), F32),
        compiler_params=_cparams("parallel"),
        name="nsa_compress_prompt",
    )(cmp_rows, cmp_rows, cmp_rows, cmp_rows, bd, pe2)


def _stack_heads(x):
    return jnp.concatenate([x[:, j * NSA_DH:(j + 1) * NSA_DH] for j in range(NSA_HPG)], axis=0)


def _softmax_rows(s):
    m = jnp.max(s, axis=-1, keepdims=True)
    e = jnp.exp(s - m)
    return e / jnp.sum(e, axis=-1, keepdims=True)


def _select_blocks(imp_t, t0, rank_scr):
    n_sel_pad = LANES // 2
    tq = imp_t.shape[1]
    rank_scr[...] = imp_t
    pair = rank_scr[pl.ds(0, n_sel_pad, stride=2), :] + rank_scr[pl.ds(1, n_sel_pad, stride=2), :]
    blk = _iota((n_sel_pad, tq), 0)
    tpos = t0 + _iota((n_sel_pad, tq), 1)
    forced = (blk == 0) | (blk == (tpos >> _LOG_SEL))
    avail = blk * L_SEL <= tpos
    val = jnp.where(forced, FORCE, jnp.where(avail, pair, -1.0))
    rank_scr[0:n_sel_pad, :] = val
    cnt = jnp.zeros((n_sel_pad, tq), F32)
    for s in range(n_sel_pad):
        row = jnp.broadcast_to(rank_scr[pl.ds(s, 1), :], (n_sel_pad, tq))
        ahead = (row > val) | ((row == val) & (blk > s))
        cnt = cnt + jnp.where(ahead, 1.0, 0.0)
    sel = jnp.where(cnt < float(N_SEL), 1.0, 0.0)
    return jnp.concatenate([sel, jnp.zeros((LANES - n_sel_pad, tq), F32)], axis=0)


_ATT_RB = 128


def _softmax_blocks(sc_scr, bias_scr, p_scr, width, rows, tq, post=None):
    for r0 in range(0, rows, _ATT_RB):
        rq = r0 % tq
        s = sc_scr[r0:r0 + _ATT_RB, 0:width] + bias_scr[rq:rq + _ATT_RB, 0:width]
        e = jnp.exp(s - jnp.max(s, axis=-1, keepdims=True))
        p = e / jnp.sum(e, axis=-1, keepdims=True)
        if post is not None:
            p = post(p, r0)
        p_scr[r0:r0 + _ATT_RB, 0:width] = p.astype(BF16)


def _nsa_attn_kernel(q_ref, qr_ref, gate_ref, kvc_ref, kvb_ref, o_ref,
                     rank_scr, sc_scr, bias_scr, p_scr, m_scr, l_scr, acc_scr, imp_scr, *, tq, tk):
    g = pl.program_id(1)
    t0 = pl.program_id(2) * tq
    rows = NSA_HPG * tq
    q4 = _stack_heads(q_ref[...])
    qr4 = _stack_heads(qr_ref[...])

    n_cmp = kvc_ref.shape[1]
    sc_scr[:, 0:n_cmp] = _dot_nt(q4, kvc_ref[0].astype(BF16))
    tpos_c = t0 + _iota((tq, n_cmp), 0)
    c_end = (_iota((tq, n_cmp), 1) + 1) * L_CMP - 1
    bias_scr[:, 0:n_cmp] = jnp.where(c_end <= tpos_c, 0.0, NEG)
    imp_scr[...] = jnp.zeros(imp_scr.shape, F32)

    def cmp_post(p, r0):
        rq = r0 % tq
        p = p * jnp.where(t0 + rq + _iota((_ATT_RB, 1), 0) >= L_CMP - 1, 1.0, 0.0)
        imp_scr[rq:rq + _ATT_RB, :] += p
        return p

    _softmax_blocks(sc_scr, bias_scr, p_scr, n_cmp, rows, tq, post=cmp_post)
    o_c = _dot(p_scr[:, 0:n_cmp], kvc_ref[1].astype(BF16))

    sel = jnp.concatenate(
        [_select_blocks(imp_scr[h0:h0 + LANES, :].T, t0 + h0, rank_scr).T for h0 in range(0, tq, LANES)], axis=0)
    sel = jnp.where(sel > 0.5, 1.0, 0.0).astype(BF16)

    m_scr[...] = jnp.full(m_scr.shape, NEG, F32)
    l_scr[...] = jnp.zeros(l_scr.shape, F32)
    acc_scr[...] = jnp.zeros(acc_scr.shape, F32)

    def sel_tile(k0, tw):
        sc_scr[:, 0:tw] = _dot_nt(qr4, kvb_ref[0, pl.ds(k0, tw), :])
        expand = jnp.where(_iota((LANES, tw), 0) == ((k0 + _iota((LANES, tw), 1)) >> _LOG_SEL), 1.0, 0.0).astype(BF16)
        picked = _dot(sel, expand)
        live = (picked > 0.5) & (k0 + _iota((tq, tw), 1) <= t0 + _iota((tq, tw), 0))
        bias_scr[:, 0:tw] = jnp.where(live, 0.0, NEG)
        for r0 in range(0, rows, _ATT_RB):
            rq = r0 % tq
            rs = slice(r0, r0 + _ATT_RB)
            s = sc_scr[rs, 0:tw] + bias_scr[rq:rq + _ATT_RB, 0:tw]
            m_old = m_scr[rs, :]
            m_new = jnp.maximum(m_old, jnp.max(s, axis=-1, keepdims=True))
            alpha = jnp.exp(m_old - m_new)
            pe = jnp.exp(s - m_new)
            l_scr[rs, :] = alpha * l_scr[rs, :] + jnp.sum(pe, axis=-1, keepdims=True)
            m_scr[rs, :] = m_new
            acc_scr[rs, :] = alpha * acc_scr[rs, :]
            p_scr[rs, 0:tw] = pe.astype(BF16)
        acc_scr[...] += _dot(p_scr[:, 0:tw], kvb_ref[1, pl.ds(k0, tw), :])

    def full_tile(kt, carry):
        sel_tile(pl.multiple_of(kt * tk, tk), tk)
        return carry

    def tail_tile(kt, carry):
        sel_tile(pl.multiple_of(n_whole * tk, tk), tail)
        return carry

    tail = tk // 2
    n_whole = (t0 + tq) // tk
    rem = t0 + tq - n_whole * tk
    n_full = n_whole + jnp.where(rem > tail, 1, 0)
    n_tail = jnp.where((rem > 0) & (rem <= tail), 1, 0)
    lax.fori_loop(0, n_full, full_tile, 0)
    lax.fori_loop(0, n_tail, tail_tile, 0)
    o_s = acc_scr[...] / l_scr[...]

    wk = WINDOW + tq
    w0 = pl.multiple_of(jnp.maximum(t0 - WINDOW, 0), tq)
    sc_scr[:, 0:wk] = _dot_nt(qr4, kvb_ref[2, pl.ds(w0, wk), :])
    dist = (t0 + _iota((tq, wk), 0)) - (w0 + _iota((tq, wk), 1))
    bias_scr[:, 0:wk] = jnp.where((dist >= 0) & (dist <= WINDOW), 0.0, NEG)
    _softmax_blocks(sc_scr, bias_scr, p_scr, wk, rows, tq)
    o_w = _dot(p_scr[:, 0:wk], kvb_ref[3, pl.ds(w0, wk), :])

    gates = gate_ref[...]
    lane = _iota((tq, LANES), 1)
    outs = []
    for j in range(NSA_HPG):
        o = jnp.zeros((tq, NSA_DH), F32)
        for br, ob in enumerate((o_c, o_s, o_w)):
            col = br * NSA_HEADS + g * NSA_HPG + j
            gcol = jnp.sum(jnp.where(lane == col, gates, 0.0), axis=-1, keepdims=True)
            o = o + gcol * ob[j * tq:(j + 1) * tq]
        outs.append(o)
    o_ref[...] = jnp.concatenate(outs, axis=1).astype(BF16)


def _nsa_attn_prompt(q, qr, gates, kvc, kvb, n_batch, tq=256, tk=1024):
    n = q.shape[0]
    t_len = n // n_batch
    nq = t_len // tq
    tk = min(tk, t_len)
    assert kvc.shape[3] == LANES and t_len >= WINDOW + tq and t_len % tk == 0
    rows = NSA_HPG * tq
    width = max(tk, WINDOW + tq, LANES)
    qspec =pl.BlockSpec((tq, NSA_HPG * NSA_DH), lambda b, g, i: (b * nq + i, g))
    return pl.pallas_call(
        functools.partial(_nsa_attn_kernel, tq=tq, tk=tk),
        grid=(n_batch, NSA_KV, nq),
        in_specs=[qspec, qspec,
                  pl.BlockSpec((tq, LANES), lambda b, g, i: (b * nq + i, 0)),
                  pl.BlockSpec((None, 2, None, LANES, NSA_DH), lambda b, g, i: (b, 0, g, 0, 0)),
                  pl.BlockSpec((None, 4, None, t_len, NSA_DH), lambda b, g, i: (b, 0, g, 0, 0))],
        out_specs=qspec,
        out_shape=jax.ShapeDtypeStruct((n, NSA_Q), BF16),
        scratch_shapes=[pltpu.VMEM((LANES, LANES), F32),
                        pltpu.VMEM((rows, width), F32),
                        pltpu.VMEM((tq, width), F32),
                        pltpu.VMEM((rows, width), BF16),
                        pltpu.VMEM((rows, 1), F32), pltpu.VMEM((rows, 1), F32),
                        pltpu.VMEM((rows, NSA_DH), F32),
                        pltpu.VMEM((tq, LANES), F32)],
        compiler_params=_cparams("parallel", "parallel", "arbitrary"),
        name="nsa_attn_prompt",
    )(q, qr, gates, kvc, kvb)


def _matmul_residual_kernel(a_ref, w_ref, x_ref, o_ref):
    o_ref[...] = x_ref[...] + _dot(a_ref[...].astype(BF16), w_ref[...])


def _matmul_residual(a, w, x, tm):
    n, k = a.shape
    row = lambda i: (i, 0)
    return pl.pallas_call(
        _matmul_residual_kernel,
        grid=(n // tm,),
        in_specs=[pl.BlockSpec((tm, k), row), _resident(w.shape), pl.BlockSpec((tm, D_MODEL), row)],
        out_specs=pl.BlockSpec((tm, D_MODEL), row),
        out_shape=jax.ShapeDtypeStruct((n, D_MODEL), F32),
        compiler_params=_cparams("parallel"),
        name="matmul_residual",
    )(a, w, x)


def _nsa_prompt_layer(x, n_batch, norm_g, w_in, cmp_pe, cmp_w, w_out, tables):
    tm = min(512, x.shape[0] // n_batch)
    q, qr, cmp_rows, _, _, gates, kvb, cmp_t, sel_t, win_t = _nsa_proj(
        x, norm_g.reshape(1, D_MODEL), _nsa_w_in_padded(w_in), tables, n_batch, tm, seq_last=True)
    bd, pe2 = _cmp_weights(cmp_w, cmp_pe)
    kvc = _nsa_compress_prompt(cmp_rows, bd, pe2, n_batch)
    o = _nsa_attn_prompt(q, qr, gates, kvc, kvb, n_batch)
    y = _matmul_residual(o, w_out.astype(BF16), x, tm)
    return y, cmp_t, sel_t, win_t


_FFN_CW = MXU_DIM


def _ffn_kernel(x_ref, g_ref, win_ref, cw_ref, wout_ref, buf_ref, y_ref, st_ref, carry_scr, uv_scr, ug_scr,
                *, tm, tpb):
    ti = pl.program_id(0) % tpb
    x = x_ref[...]
    xn = _rmsnorm(x, g_ref[...]).astype(BF16)

    @pl.when(ti == 0)
    def _():
        carry_scr[SUBLANES - (FFN_CONV - 1):SUBLANES, :] = buf_ref[...]

    acc = jnp.zeros((tm, D_MODEL), F32)
    for c in range(D_FF // _FFN_CW):
        conv = []
        for part, scr in ((0, uv_scr), (1, ug_scr)):
            c0 = part * D_FF + c * _FFN_CW
            u = _dot(xn, win_ref[:, c0:c0 + _FFN_CW])
            scr[0:SUBLANES, :] = carry_scr[:, c0:c0 + _FFN_CW]
            scr[SUBLANES:SUBLANES + tm, :] = u
            carry_scr[:, c0:c0 + _FFN_CW] = u[tm - SUBLANES:tm, :]
            w = cw_ref[:, c0:c0 + _FFN_CW]
            conv.append(scr[SUBLANES - 2:SUBLANES - 2 + tm, :] * w[0:1] + scr[SUBLANES - 1:SUBLANES - 1 + tm, :] * w[1:2]
                        + u * w[2:3])
        act = conv[0] * jax.nn.silu(conv[1])
        acc = acc + _dot(act.astype(BF16), wout_ref[c * _FFN_CW:(c + 1) * _FFN_CW, :])
    y_ref[...] = x + acc

    @pl.when(ti == tpb - 1)
    def _():
        st_ref[...] = carry_scr[SUBLANES - (FFN_CONV - 1):SUBLANES, :]


def _ffn_prompt(x, n_batch, norm_g, w_in, conv_w, w_out, buf, tm=512):
    n = x.shape[0]
    tpb = n // n_batch // tm
    row = lambda i: (i, 0)
    st = pl.BlockSpec((None, FFN_CONV - 1, 2 * D_FF), lambda i: (i // tpb, 0, 0))
    return pl.pallas_call(
        functools.partial(_ffn_kernel, tm=tm, tpb=tpb),
        grid=(n // tm,),
        in_specs=[pl.BlockSpec((tm, D_MODEL), row), _resident((1, D_MODEL)), _resident(w_in.shape),
                  _resident(conv_w.shape), _resident(w_out.shape), st],
        out_specs=[pl.BlockSpec((tm, D_MODEL), row), st],
        out_shape=[jax.ShapeDtypeStruct((n, D_MODEL), F32),
                   jax.ShapeDtypeStruct((n_batch, FFN_CONV - 1, 2 * D_FF), F32)],
        scratch_shapes=[pltpu.VMEM((SUBLANES, 2 * D_FF), F32), pltpu.VMEM((SUBLANES + tm, _FFN_CW), F32),
                        pltpu.VMEM((SUBLANES + tm, _FFN_CW), F32)],
        compiler_params=_cparams("arbitrary"),
        name="ffn_prompt",
    )(x, norm_g.reshape(1, D_MODEL), w_in, conv_w, w_out, buf)


_GDN_W_COLS = GDN_CONV_CH + GDN_VW + LANES
_GDN_CW = MXU_DIM


def _softplus(x):
    return jnp.maximum(x, 0.0) + jnp.log1p(jnp.exp(-jnp.abs(x)))


def _gdn_w_in_padded(w_in):
    return jnp.pad(w_in, ((0, 0), (0, _GDN_W_COLS - w_in.shape[1]))).astype(BF16)


def _lane_row(v):
    return jnp.pad(v.astype(F32), (0, LANES - v.shape[0])).reshape(1, LANES)


def _gdn_post_conv(y, c0):
    y = jax.nn.silu(y)
    kind = c0 // GDN_QK
    if kind == 2:
        return y
    outs = []
    for hh in range(_GDN_CW // GDN_DK):
        seg = y[:, hh * GDN_DK:(hh + 1) * GDN_DK]
        seg = seg * lax.rsqrt(jnp.sum(seg * seg, axis=-1, keepdims=True) + EPS)
        outs.append(seg * GDN_DK ** -0.5 if kind == 0 else seg)
    return jnp.concatenate(outs, axis=1)


def _gdn_gates(ab, alog, dtb):
    lane = _iota(ab.shape, 1)
    gval = -jnp.exp(alog) * _softplus(ab + dtb)
    return jnp.where(lane < GDN_HEADS, gval, jax.nn.sigmoid(ab))


def _gdn_proj_kernel(x_ref, g_ref, w_ref, cw_ref, alog_ref, dtb_ref, buf_ref,
                     q_ref, k_ref, v_ref, gate_ref, gb_ref, st_ref, carry_scr, u_scr, *, tm, tpb):
    ti = pl.program_id(0) % tpb
    xn = _rmsnorm(x_ref[...], g_ref[...]).astype(BF16)
    keep = GDN_CONV - 1

    @pl.when(ti == 0)
    def _():
        carry_scr[SUBLANES - keep:SUBLANES, :] = buf_ref[...]

    outs = (q_ref, k_ref, v_ref)
    for c in range(GDN_CONV_CH // _GDN_CW):
        c0 = c * _GDN_CW
        u = _dot(xn, w_ref[:, c0:c0 + _GDN_CW])
        u_scr[0:SUBLANES, :] = carry_scr[:, c0:c0 + _GDN_CW]
        u_scr[SUBLANES:SUBLANES + tm, :] = u
        carry_scr[:, c0:c0 + _GDN_CW] = u[tm - SUBLANES:tm, :]
        w = cw_ref[:, c0:c0 + _GDN_CW]
        y = u_scr[SUBLANES - 3:SUBLANES - 3 + tm, :] * w[0:1]
        y = y + u_scr[SUBLANES - 2:SUBLANES - 2 + tm, :] * w[1:2]
        y = y + u_scr[SUBLANES - 1:SUBLANES - 1 + tm, :] * w[2:3]
        y = y + u * w[3:4]
        o_ref = outs[c0 // GDN_QK]
        o_ref[:, c0 % GDN_QK:c0 % GDN_QK + _GDN_CW] = _gdn_post_conv(y, c0)
    gate_ref[...] = _dot(xn, w_ref[:, GDN_CONV_CH:GDN_CONV_CH + GDN_VW])
    ab = _dot(xn, w_ref[:, GDN_CONV_CH + GDN_VW:_GDN_W_COLS])
    gb_ref[...] = _gdn_gates(ab, alog_ref[...], dtb_ref[...])

    @pl.when(ti == tpb - 1)
    def _():
        st_ref[...] = carry_scr[SUBLANES - keep:SUBLANES, :]


def _gdn_proj(x, n_batch, norm_g, w, conv_w, alog, dtb, buf, tm=256):
    n = x.shape[0]
    tpb = n // n_batch // tm
    row = lambda i: (i, 0)
    wide = pl.BlockSpec((tm, GDN_QK), row)
    st = pl.BlockSpec((None, GDN_CONV - 1, GDN_CONV_CH), lambda i: (i // tpb, 0, 0))
    f = jax.ShapeDtypeStruct((n, GDN_QK), F32)
    return pl.pallas_call(
        functools.partial(_gdn_proj_kernel, tm=tm, tpb=tpb),
        grid=(n // tm,),
        in_specs=[pl.BlockSpec((tm, D_MODEL), row), _resident((1, D_MODEL)), _resident(w.shape),
                  _resident(conv_w.shape), _resident((1, LANES)), _resident((1, LANES)), st],
        out_specs=[wide, wide, wide, wide, pl.BlockSpec((tm, LANES), row), st],
        out_shape=[f, f, f, f, jax.ShapeDtypeStruct((n, LANES), F32),
                   jax.ShapeDtypeStruct((n_batch, GDN_CONV - 1, GDN_CONV_CH), F32)],
        scratch_shapes=[pltpu.VMEM((SUBLANES, GDN_CONV_CH), F32), pltpu.VMEM((SUBLANES + tm, _GDN_CW), F32)],
        compiler_params=_cparams("arbitrary"),
        name="gdn_proj",
    )(x, norm_g.reshape(1, D_MODEL), w, conv_w, alog, dtb, buf)


_GDN_HG = MXU_DIM // GDN_CHUNK


def _gdn_block_diag(x):
    head = _iota(x.shape, 0) // GDN_CHUNK
    return jnp.concatenate([jnp.where(head == h, x, 0.0) for h in range(_GDN_HG)], axis=1)


def _gdn_group_prep(q, k, v, beta, gcum_col, gcum_row, g_last):
    n = q.shape[0]
    ii = _iota((n, n), 0)
    jj = _iota((n, n), 1)
    same = (ii // GDN_CHUNK) == (jj // GDN_CHUNK)
    decay = jnp.exp(jnp.where(same & (ii >= jj), gcum_col - gcum_row, NEG))
    kb = k * beta
    k16 = k.astype(BF16)
    low = _dot_nt(kb.astype(BF16), k16) * decay * jnp.where(ii > jj, 1.0, 0.0)
    attn = _dot_nt(q.astype(BF16), k16) * decay
    x = jnp.where(ii == jj, 1.0, 0.0) - low
    p = low
    for _ in range(GDN_CHUNK.bit_length() - 2):
        p16 = p.astype(BF16)
        p = _dot(p16, p16)
        x = x + _dot(x.astype(BF16), p.astype(BF16))
    rhs = jnp.concatenate([v * beta, kb * jnp.exp(gcum_col)], axis=1)
    sol = _dot(x.astype(BF16), rhs.astype(BF16))
    u_base, w_dec = sol[:, :GDN_DV], sol[:, GDN_DV:]
    return (u_base, _gdn_block_diag(w_dec).astype(BF16), _gdn_block_diag(q * jnp.exp(gcum_col)).astype(BF16),
            attn.astype(BF16), _gdn_block_diag(k * jnp.exp(g_last - gcum_col)).astype(BF16))


def _gdn_group_step(prep, s_prev, s_decay):
    u_base, w_dec_bd, q_bd, attn, k_dec_bd = prep
    s16 = s_prev.astype(BF16)
    u = u_base - _dot(w_dec_bd, s16)
    u16 = u.astype(BF16)
    o = _dot(q_bd, s16) + _dot(attn, u16)
    s_new = s_prev * s_decay + _dot_tn(k_dec_bd, u16)
    return o, s_new


def _gdn_chunk_kernel(q_ref, k_ref, v_ref, gb_ref, gate_ref, x_ref, s0_ref, ng_ref, wout_ref,
                      y_ref, s_ref, s_scr, o_scr, *, rt, tpb):
    ti = pl.program_id(1)
    n_groups = GDN_HEADS // _GDN_HG

    @pl.when(ti == 0)
    def _():
        s_scr[...] = s0_ref[...].reshape(n_groups, _GDN_HG * GDN_DK, GDN_DV)

    c = GDN_CHUNK
    tri = jnp.where(_iota((c, c), 0) >= _iota((c, c), 1), 1.0, 0.0)
    stack_rows = lambda ref, r0, h0: jnp.concatenate(
        [ref[r0:r0 + c, (h0 + h) * GDN_DK:(h0 + h + 1) * GDN_DK] for h in range(_GDN_HG)], axis=0)

    preps, decays = [], []
    for ci in range(rt // c):
        r0 = ci * c
        gb = gb_ref[r0:r0 + c, :]
        gcum = _dot_f32(tri, gb)
        gcum_t = gcum.T
        for gi in range(n_groups):
            h0 = gi * _GDN_HG
            heads = range(h0, h0 + _GDN_HG)
            col = lambda a: jnp.concatenate([a[:, h:h + 1] for h in heads], axis=0)
            g_last = jnp.concatenate([jnp.broadcast_to(gcum[c - 1:c, h:h + 1], (c, 1)) for h in heads], axis=0)
            gcum_row = jnp.concatenate([gcum_t[h:h + 1, :] for h in heads], axis=1)
            beta = jnp.concatenate([gb[:, GDN_HEADS + h:GDN_HEADS + h + 1] for h in heads], axis=0)
            preps.append(_gdn_group_prep(stack_rows(q_ref, r0, h0), stack_rows(k_ref, r0, h0),
                                         stack_rows(v_ref, r0, h0), beta, col(gcum), gcum_row, g_last))
            decays.append(jnp.concatenate(
                [jnp.broadcast_to(jnp.exp(gcum[c - 1:c, h:h + 1]), (GDN_DK, 1)) for h in heads], axis=0))

    for gi in range(n_groups):
        s = s_scr[gi]
        for ci in range(rt // c):
            o, s = _gdn_group_step(preps[ci * n_groups + gi], s, decays[ci * n_groups + gi])
            for h in range(_GDN_HG):
                o_scr[ci * c:(ci + 1) * c, (gi * _GDN_HG + h) * GDN_DV:(gi * _GDN_HG + h + 1) * GDN_DV] = (
                    o[h * c:(h + 1) * c])
        s_scr[gi] = s

    ng = ng_ref[...]
    outs = []
    for h in range(GDN_HEADS):
        cols = slice(h * GDN_DV, (h + 1) * GDN_DV)
        outs.append((_rmsnorm(o_scr[:, cols], ng) * jax.nn.silu(gate_ref[:, cols])).astype(BF16))
    y_ref[...] = x_ref[...] + _dot(jnp.concatenate(outs, axis=1), wout_ref[...])

    @pl.when(ti == tpb - 1)
    def _():
        s_ref[...] = s_scr[...].reshape(GDN_HEADS, GDN_DK, GDN_DV)


def _gdn_chunked(q, k, v, gb, gate, x, s0, norm_g, w_out, n_batch, rt=256):
    n = x.shape[0]
    tpb = n // n_batch // rt
    row = lambda b, i: (b * tpb + i, 0)
    wide = pl.BlockSpec((rt, GDN_QK), row)
    st = pl.BlockSpec((None, GDN_HEADS, GDN_DK, GDN_DV), lambda b, i: (b, 0, 0, 0))
    return pl.pallas_call(
        functools.partial(_gdn_chunk_kernel, rt=rt, tpb=tpb),
        grid=(n_batch, tpb),
        in_specs=[wide, wide, wide, pl.BlockSpec((rt, LANES), row), wide, wide, st,
                  _resident((1, GDN_DV)), _resident(w_out.shape)],
        out_specs=[wide, st],
        out_shape=[jax.ShapeDtypeStruct((n, D_MODEL), F32),
                   jax.ShapeDtypeStruct((n_batch, GDN_HEADS, GDN_DK, GDN_DV), F32)],
        scratch_shapes=[pltpu.VMEM((GDN_HEADS // _GDN_HG, _GDN_HG * GDN_DK, GDN_DV), F32),
                        pltpu.VMEM((rt, GDN_VW), F32)],
        compiler_params=_cparams("parallel", "arbitrary"),
        name="gdn_chunked",
    )(q, k, v, gb, gate, x, s0, norm_g.reshape(1, GDN_DV), w_out)


def _gdn_prompt_layer(x, n_batch, norm_g, s0, conv_buf, w_in, conv_w, a_log, dt_bias, head_norm_g, w_out):
    q, k, v, gate, gb, new_buf = _gdn_proj(x, n_batch, norm_g, _gdn_w_in_padded(w_in), conv_w,
                                           _lane_row(a_log), _lane_row(dt_bias), conv_buf)
    y, s_new = _gdn_chunked(q, k, v, gb, gate, x, s0, head_norm_g, w_out.astype(BF16), n_batch)
    return y, s_new, new_buf


_LRU_CW = MXU_DIM


def _lru_gates(xc, wga_ref, bga, wgx_ref, bgx, lam, n0):
    r_parts, i_parts = [], []
    for j in range(xc.shape[1] // LRU_BW):
        blk = xc[:, j * LRU_BW:(j + 1) * LRU_BW].astype(BF16)
        r_parts.append(_dot(blk, wga_ref[n0 + j]))
        i_parts.append(_dot(blk, wgx_ref[n0 + j]))
    cat = lambda ps: ps[0] if len(ps) == 1 else jnp.concatenate(ps, axis=1)
    r = jax.nn.sigmoid(cat(r_parts) + bga)
    i = jax.nn.sigmoid(cat(i_parts) + bgx)
    log_a = -LRU_C * r * _softplus(-lam)
    a = jnp.exp(log_a)
    b = jnp.sqrt(-jnp.tanh(log_a) * (a * a + 1.0)) * i * xc
    return a, b


def _lru_proj_kernel(x_ref, g_ref, w_ref, cw_ref, cb_ref, wga_ref, bga_ref, wgx_ref, bgx_ref, lam_ref, buf_ref,
                     gate_ref, a_ref, b_ref, st_ref, carry_scr, u_scr, *, tm, tpb):
    ti = pl.program_id(0) % tpb
    xn = _rmsnorm(x_ref[...], g_ref[...]).astype(BF16)
    keep = LRU_CONV - 1

    @pl.when(ti == 0)
    def _():
        carry_scr[SUBLANES - keep:SUBLANES, :] = buf_ref[...]

    gate_ref[...] = _dot(xn, w_ref[:, 0:D_RNN])
    for c in range(D_RNN // _LRU_CW):
        c0 = c * _LRU_CW
        cols = slice(c0, c0 + _LRU_CW)
        u = _dot(xn, w_ref[:, D_RNN + c0:D_RNN + c0 + _LRU_CW])
        u_scr[0:SUBLANES, :] = carry_scr[:, cols]
        u_scr[SUBLANES:SUBLANES + tm, :] = u
        carry_scr[:, cols] = u[tm - SUBLANES:tm, :]
        w = cw_ref[:, cols]
        xc = u_scr[SUBLANES - 3:SUBLANES - 3 + tm, :] * w[0:1]
        xc = xc + u_scr[SUBLANES - 2:SUBLANES - 2 + tm, :] * w[1:2]
        xc = xc + u_scr[SUBLANES - 1:SUBLANES - 1 + tm, :] * w[2:3]
        xc = xc + u * w[3:4] + cb_ref[:, cols]
        a, b = _lru_gates(xc, wga_ref, bga_ref[:, cols], wgx_ref, bgx_ref[:, cols], lam_ref[:, cols],
                          c0 // LRU_BW)
        a_ref[:, cols] = a
        b_ref[:, cols] = b

    @pl.when(ti == tpb - 1)
    def _():
        st_ref[...] = carry_scr[SUBLANES - keep:SUBLANES, :]


def _lru_proj(x, n_batch, norm_g, w, conv_w, conv_b, w_ga, b_ga, w_gx, b_gx, lam, buf, tm=512):
    n = x.shape[0]
    tpb = n // n_batch // tm
    row = lambda i: (i, 0)
    wide = pl.BlockSpec((tm, D_RNN), row)
    vec = _resident((1, D_RNN))
    st = pl.BlockSpec((None, LRU_CONV - 1, D_RNN), lambda i: (i // tpb, 0, 0))
    f = jax.ShapeDtypeStruct((n, D_RNN), F32)
    r1 = lambda v: v.reshape(1, D_RNN)
    return pl.pallas_call(
        functools.partial(_lru_proj_kernel, tm=tm, tpb=tpb),
        grid=(n // tm,),
        in_specs=[pl.BlockSpec((tm, D_MODEL), row), _resident((1, D_MODEL)), _resident(w.shape),
                  _resident(conv_w.shape), vec, _resident(w_ga.shape), vec, _resident(w_gx.shape), vec, vec, st],
        out_specs=[wide, wide, wide, st],
        out_shape=[f, f, f, jax.ShapeDtypeStruct((n_batch, LRU_CONV - 1, D_RNN), F32)],
        scratch_shapes=[pltpu.VMEM((SUBLANES, D_RNN), F32), pltpu.VMEM((SUBLANES + tm, _LRU_CW), F32)],
        compiler_params=_cparams("arbitrary"),
        name="lru_proj",
    )(x, norm_g.reshape(1, D_MODEL), w, conv_w, r1(conv_b), w_ga, r1(b_ga), w_gx, r1(b_gx), r1(lam), buf)


def _lru_scan_kernel(a_ref, b_ref, gate_ref, x_ref, h0_ref, wout_ref, y_ref, hl_ref, h_scr, hs_scr, *, rt, tpb):
    ti = pl.program_id(1)

    @pl.when(ti == 0)
    def _():
        h_scr[...] = h0_ref[...]

    def step(t, h):
        h = a_ref[pl.ds(t, 1), :] * h + b_ref[pl.ds(t, 1), :]
        hs_scr[pl.ds(t, 1), :] = h
        return h

    h = lax.fori_loop(0, rt, step, h_scr[...], unroll=8)
    h_scr[...] = h
    y = (hs_scr[...] * jax.nn.gelu(gate_ref[...])).astype(BF16)
    y_ref[...] = x_ref[...] + _dot(y, wout_ref[...])

    @pl.when(ti == tpb - 1)
    def _():
        hl_ref[...] = h


def _lru_scan(a, b, gate, x, h0, w_out, n_batch, rt=256):
    n = x.shape[0]
    tpb = n // n_batch // rt
    wide = pl.BlockSpec((rt, D_RNN), lambda bb, i: (bb * tpb + i, 0))
    st = pl.BlockSpec((None, 1, D_RNN), lambda bb, i: (bb, 0, 0))
    y, hl = pl.pallas_call(
        functools.partial(_lru_scan_kernel, rt=rt, tpb=tpb),
        grid=(n_batch, tpb),
        in_specs=[wide, wide, wide, wide, st, _resident(w_out.shape)],
        out_specs=[wide, st],
        out_shape=[jax.ShapeDtypeStruct((n, D_MODEL), F32), jax.ShapeDtypeStruct((n_batch, 1, D_RNN), F32)],
        scratch_shapes=[pltpu.VMEM((1, D_RNN), F32), pltpu.VMEM((rt, D_RNN), F32)],
        compiler_params=_cparams("parallel", "arbitrary"),
        name="lru_scan",
    )(a, b, gate, x, h0.reshape(n_batch, 1, D_RNN), w_out)
    return y, hl.reshape(n_batch, D_RNN)


def _lru_prompt_layer(x, n_batch, norm_g, h0, conv_buf, w_in, conv_w, conv_b, w_ga, b_ga, w_gx, b_gx, lam, w_out):
    gate, a, b, new_buf = _lru_proj(x, n_batch, norm_g, w_in.astype(BF16), conv_w, conv_b, w_ga.astype(BF16), b_ga,
                                    w_gx.astype(BF16), b_gx, lam, conv_buf)
    y, h_last = _lru_scan(a, b, gate, x, h0, w_out.astype(BF16), n_batch)
    return y, h_last, new_buf


def _final_norm_kernel(x_ref, g_ref, o_ref):
    o_ref[...] = _rmsnorm(x_ref[...], g_ref[...])


def _final_norm(x, g, tm):
    n = x.shape[0]
    row = lambda i: (i, 0)
    return pl.pallas_call(
        _final_norm_kernel,
        grid=(n // tm,),
        in_specs=[pl.BlockSpec((tm, D_MODEL), row), _resident((1, D_MODEL))],
        out_specs=pl.BlockSpec((tm, D_MODEL), row),
        out_shape=jax.ShapeDtypeStruct((n, D_MODEL), F32),
        compiler_params=_cparams("parallel"),
        name="final_norm",
    )(x, g.reshape(1, D_MODEL))


def _norm_matmul_small_kernel(x_ref, g_ref, w_ref, o_ref):
    xn = _rmsnorm(x_ref[...], g_ref[...]).astype(BF16)
    o_ref[...] = _dot(xn, w_ref[...])


def _norm_matmul_small(x, g, w, n_col_tiles):
    n, k = x.shape[0], w.shape[1]
    tn = k // n_col_tiles
    return pl.pallas_call(
        _norm_matmul_small_kernel,
        grid=(n_col_tiles,),
        in_specs=[_resident((n, D_MODEL)), _resident((1, D_MODEL)), pl.BlockSpec((D_MODEL, tn), lambda j: (0, j))],
        out_specs=pl.BlockSpec((n, tn), lambda j: (0, j)),
        out_shape=jax.ShapeDtypeStruct((n, k), F32),
        compiler_params=_cparams("parallel"),
        name="norm_matmul_small",
    )(x, g.reshape(1, D_MODEL), w)


_CMP_PAGES_PER_STEP = 16
_CMP_BLK_PER_PAGE = PAGE_SIZE // L_CMP


def _nsa_compress_sample_kernel(pt_ref, *refs, n_pages, n_blk, n_out):
    pages = refs[:_CMP_PAGES_PER_STEP]
    new_ref, bd_ref, pe_ref, o_ref, rows_scr = refs[_CMP_PAGES_PER_STEP:]
    b = pl.program_id(0)
    pg = pl.program_id(1)
    past_blk = n_pages * _CMP_BLK_PER_PAGE
    pair_rows = 2 * PAGE_SIZE
    out_row = _iota((pair_rows, pair_rows), 0)
    src_row = _iota((pair_rows, pair_rows), 1)
    blk_per_pair = 2 * _CMP_BLK_PER_PAGE
    perm = jnp.where(src_row == (out_row % blk_per_pair) * L_CMP + out_row // blk_per_pair, 1.0, 0.0).astype(BF16)
    for r in range(0, _CMP_PAGES_PER_STEP, 2):
        c0 = pl.multiple_of((pg * _CMP_PAGES_PER_STEP + r) * _CMP_BLK_PER_PAGE, blk_per_pair)
        for kv in range(2):
            both = jnp.concatenate(
                [pages[r + i][kv].reshape(NSA_KVW, PAGE_SIZE) for i in range(2)], axis=1)
            moved = _dot_nt(perm, both.astype(BF16))
            for l in range(L_CMP):
                rows_scr[kv, l, pl.ds(c0, blk_per_pair), :] = moved[l * blk_per_pair:(l + 1) * blk_per_pair]

    @pl.when(pg == pl.num_programs(1) - 1)
    def _():
        new = new_ref[pl.ds(b, 1), :]
        n_tail = n_blk - past_blk
        first = _iota((n_tail, NSA_KVW), 0) == 0
        for kv in range(2):
            for l in range(L_CMP):
                tail = jnp.where(first, new[:, kv * NSA_KVW:(kv + 1) * NSA_KVW], 0.0) if l == 0 else (
                    jnp.zeros((n_tail, NSA_KVW), F32))
                rows_scr[kv, l, past_blk:n_blk, :] = tail
        o_ref[...] = jnp.zeros(o_ref.shape, F32)

        def load_rows(kv, l):
            return rows_scr[kv, l]

        _compress_planes(load_rows, bd_ref, pe_ref, o_ref.at[:, :, 0:n_blk, :], n_blk)


def _rows_last(cache):
    nd = cache.ndim
    return jnp.transpose(cache, tuple(range(nd - 4)) + (nd - 3, nd - 2, nd - 1, nd - 4))


def _nsa_compress_sample(cache_t, layer, page_table, new_rows, bd, pe2):
    n_b, n_pages = page_table.shape
    past = n_pages * PAGE_SIZE
    n_blk = -(-(past + 1) // L_SEL) * L_SEL // L_CMP
    n_blk = -(-n_blk // SUBLANES) * SUBLANES
    n_out = -(-n_blk // LANES) * LANES
    cache4 = cache_t

    def page_spec(r):
        return pl.BlockSpec((None, None, 2, NSA_KV, NSA_DH, PAGE_SIZE),
                            lambda b, pg, pt: (layer, pt[b, pg * _CMP_PAGES_PER_STEP + r], 0, 0, 0, 0))

    grid_spec = pltpu.PrefetchScalarGridSpec(
        num_scalar_prefetch=1,
        grid=(n_b, n_pages // _CMP_PAGES_PER_STEP),
        in_specs=[page_spec(r) for r in range(_CMP_PAGES_PER_STEP)]
        + [pl.BlockSpec(new_rows.shape, lambda b, pg, pt: (0, 0)),
           pl.BlockSpec(bd.shape, lambda b, pg, pt: (0, 0, 0, 0)),
           pl.BlockSpec(pe2.shape, lambda b, pg, pt: (0, 0, 0))],
        out_specs=pl.BlockSpec((None, 2, NSA_KV, n_out, NSA_DH), lambda b, pg, pt: (b, 0, 0, 0, 0)),
        scratch_shapes=[pltpu.VMEM((2, L_CMP, n_blk, NSA_KVW), F32)])
    return pl.pallas_call(
        functools.partial(_nsa_compress_sample_kernel, n_pages=n_pages, n_blk=n_blk, n_out=n_out),
        grid_spec=grid_spec,
        out_shape=jax.ShapeDtypeStruct((n_b, 2, NSA_KV, n_out, NSA_DH), F32),
        compiler_params=_cparams("parallel", "arbitrary"),
        name="nsa_compress_sample",
    )(page_table, *([cache4] * _CMP_PAGES_PER_STEP), new_rows, bd, pe2)


def _pad_rows8(x):
    return jnp.concatenate([x, jnp.zeros((SUBLANES - x.shape[0], x.shape[1]), x.dtype)], axis=0)


_TOPK_ROWS_PER_STEP = 8


def _nsa_cmp_topk_sample_kernel(q_ref, kvc_ref, oc_ref, idx_ref, *, qpos, n_sel):
    nb = q_ref.shape[0]
    n_cmp = kvc_ref.shape[3]
    n_lane = 2 * LANES
    c_end = (_iota((SUBLANES, n_cmp), 1) + 1) * L_CMP - 1
    valid = c_end <= qpos
    any_valid = 1.0 if qpos >= L_CMP - 1 else 0.0
    pair_mat = jnp.where((_iota((n_cmp, n_lane), 0) >> 1) == _iota((n_cmp, n_lane), 1), 1.0, 0.0)
    imps = []
    for i in range(nb):
        for g in range(NSA_KV):
            q8 = _pad_rows8(q_ref[i, g * NSA_HPG:(g + 1) * NSA_HPG, :])
            s = _dot_nt(q8, kvc_ref[i, 0, g].astype(BF16))
            p = _softmax_rows(jnp.where(valid, s, NEG)) * any_valid
            o_c = _dot(p.astype(BF16), kvc_ref[i, 1, g].astype(BF16))
            oc_ref[i, g * NSA_HPG:(g + 1) * NSA_HPG, :] = o_c[0:NSA_HPG]
            imps.append(p[0:1] + p[1:2] + p[2:3] + p[3:4])
        imps.append(jnp.zeros((SUBLANES - NSA_KV, n_cmp), F32))
    rows = nb * SUBLANES
    imp = jnp.concatenate(imps, axis=0)
    pair = _dot_f32(imp, pair_mat)
    blk = _iota((rows, n_lane), 1)
    forced = (blk == 0) | (blk == qpos // L_SEL)
    avail = blk * L_SEL <= qpos
    val = jnp.where(forced, FORCE, jnp.where(avail, pair, -1.0))
    val = jnp.where(blk < n_sel, val, -2.0)
    out = jnp.zeros((rows, LANES), I32)
    out_lane = _iota((rows, LANES), 1)
    for r in range(N_SEL):
        m = jnp.max(val, axis=-1, keepdims=True)
        first = jnp.min(jnp.where(val == m, blk, n_lane), axis=-1, keepdims=True)
        out = jnp.where(out_lane == r, first, out)
        val = jnp.where(blk == first, -3.0, val)
    idx_ref[...] = out.reshape(nb, SUBLANES, LANES)


def _nsa_cmp_topk_sample(q3, kvc, qpos, n_sel):
    n_b = q3.shape[0]
    n_cmp = kvc.shape[3]
    nb = _TOPK_ROWS_PER_STEP if n_b % _TOPK_ROWS_PER_STEP == 0 else 1
    return pl.pallas_call(
        functools.partial(_nsa_cmp_topk_sample_kernel, qpos=qpos, n_sel=n_sel),
        grid=(n_b // nb,),
        in_specs=[pl.BlockSpec((nb, NSA_HEADS, NSA_DH), lambda b: (b, 0, 0)),
                  pl.BlockSpec((nb, 2, NSA_KV, n_cmp, NSA_DH), lambda b: (b, 0, 0, 0, 0))],
        out_specs=[pl.BlockSpec((nb, NSA_HEADS, NSA_DH), lambda b: (b, 0, 0)),
                   pl.BlockSpec((nb, SUBLANES, LANES), lambda b: (b, 0, 0))],
        out_shape=[jax.ShapeDtypeStruct((n_b, NSA_HEADS, NSA_DH), F32),
                   jax.ShapeDtypeStruct((n_b, SUBLANES, LANES), I32)],
        compiler_params=_cparams("parallel"),
        name="nsa_cmp_topk_sample",
    )(q3, kvc)


def _nsa_gather_sel_kernel(pt_ref, idx_ref, *refs):
    blocks = refs[:NSA_KV * N_SEL]
    o_ref = refs[NSA_KV * N_SEL]
    for g in range(NSA_KV):
        for kv in range(2):
            pages = [blocks[g * N_SEL + r][kv] for r in range(N_SEL)]
            o_ref[g, kv] = jnp.concatenate(pages, axis=1).astype(BF16)


def _nsa_gather_sel(cache_t, layer, page_table, idx):
    n_b, n_pages = page_table.shape
    past = n_pages * PAGE_SIZE
    halves = PAGE_SIZE // L_SEL
    last_blk = past // L_SEL - 1

    def blk_spec(n):
        g = n // N_SEL

        def index(b, pt, ix):
            s = jnp.clip(ix[b, n], 0, last_blk)
            return (layer, pt[b, s // halves], 0, g, 0, 0)
        return pl.BlockSpec((None, None, 2, None, NSA_DH, PAGE_SIZE), index)

    grid_spec = pltpu.PrefetchScalarGridSpec(
        num_scalar_prefetch=2,
        grid=(n_b,),
        in_specs=[blk_spec(n) for n in range(NSA_KV * N_SEL)],
        out_specs=pl.BlockSpec((None, NSA_KV, 2, NSA_DH, N_SEL * PAGE_SIZE), lambda b, pt, ix: (b, 0, 0, 0, 0)))
    return pl.pallas_call(
        _nsa_gather_sel_kernel,
        grid_spec=grid_spec,
        out_shape=jax.ShapeDtypeStruct((n_b, NSA_KV, 2, NSA_DH, N_SEL * PAGE_SIZE), BF16),
        name="nsa_gather_sel",
    )(page_table, idx, *([cache_t] * (NSA_KV * N_SEL)))


def _attend_with_new(q8, k_t, v_t, live, k_new, v_new, new_live):
    s = jnp.where(live > 0.5, _dot(q8, k_t), NEG)
    k_new16 = k_new.astype(BF16).astype(F32)
    s_new = jnp.sum(q8.astype(F32) * k_new16, axis=-1, keepdims=True)
    s_new = jnp.where(new_live > 0.5, s_new, NEG)
    m = jnp.maximum(jnp.max(s, axis=-1, keepdims=True), s_new)
    e = jnp.exp(s - m)
    e_new = jnp.exp(s_new - m)
    inv = 1.0 / (jnp.sum(e, axis=-1, keepdims=True) + e_new)
    p16 = (e * inv).astype(BF16)
    p_new16 = (e_new * inv).astype(BF16).astype(F32)
    return _dot_nt(p16, v_t) + p_new16 * v_new.astype(BF16).astype(F32)


def _row_to_col(row):
    return jnp.broadcast_to(row, (SUBLANES, row.shape[1])).T[:, 0:1]


def _nsa_sel_win_sample_kernel(idx_ref, qr_ref, oc_ref, gate_ref, selnew_ref, winnew_ref, win_ref, kv_ref,
                               o_ref, wout_ref, *, qpos, past):
    gates = gate_ref[...]
    glane = _iota((1, LANES), 1)
    n_key = N_SEL * PAGE_SIZE
    lane_k = _iota((SUBLANES, n_key), 1)
    log_page = PAGE_SIZE.bit_length() - 1
    wb = win_ref.shape[3]
    wlane = _iota((SUBLANES, wb), 1)
    kwpos = past - wb + wlane
    dist = qpos - kwpos
    wlive = jnp.where((dist >= 0) & (dist <= WINDOW) & (kwpos >= 0), 1.0, 0.0)
    new_in_window = jnp.full((1, 1), 1.0 if 0 <= qpos - past <= WINDOW else 0.0, F32)
    expand = jnp.where(_iota((LANES, n_key), 0) == (_iota((LANES, n_key), 1) >> log_page), 1.0, 0.0)
    idx_f = idx_ref[...].astype(F32)
    blk_of_lane = _dot_f32(jnp.where(_iota((SUBLANES, LANES), 1) < N_SEL, idx_f, 0.0), expand)
    sel_new = selnew_ref[...]
    win_new = winnew_ref[...]
    last_lane = _iota((NSA_DH, wb), 1) == wb - 1
    for g in range(NSA_KV):
        kc = slice(g * NSA_DH, (g + 1) * NSA_DH)
        vc = slice(NSA_KVW + g * NSA_DH, NSA_KVW + (g + 1) * NSA_DH)
        q8 = _pad_rows8(qr_ref[g * NSA_HPG:(g + 1) * NSA_HPG, :])
        blk_g = blk_of_lane[g:g + 1, :]
        new_blk = float(qpos // L_SEL)
        new_selected = jnp.max(jnp.where(blk_g == new_blk, 1.0, 0.0), axis=-1, keepdims=True)
        halves = PAGE_SIZE // L_SEL
        page_g = jnp.floor(jnp.minimum(blk_g, float(past // L_SEL - 1)) * (1.0 / halves))
        kpos = page_g * float(PAGE_SIZE) + (lane_k & (PAGE_SIZE - 1)).astype(F32)
        in_block = jnp.floor(kpos * (1.0 / L_SEL)) == blk_g
        live = jnp.where(in_block & (kpos < float(past)) & (kpos <= float(qpos)), 1.0, 0.0)
        o_s = _attend_with_new(q8, kv_ref[g, 0], kv_ref[g, 1], live, sel_new[:, kc], sel_new[:, vc], new_selected)
        kw_t = win_ref[0, g]
        vw_t = win_ref[1, g]
        o_w = _attend_with_new(q8, kw_t.astype(BF16), vw_t.astype(BF16), wlive, win_new[:, kc], win_new[:, vc],
                               new_in_window)
        wout_ref[0, g] = jnp.where(last_lane, _row_to_col(win_new[:, kc]), pltpu.roll(kw_t, wb - 1, 1))
        wout_ref[1, g] = jnp.where(last_lane, _row_to_col(win_new[:, vc]), pltpu.roll(vw_t, wb - 1, 1))
        o_c = oc_ref[g * NSA_HPG:(g + 1) * NSA_HPG, :]
        rows = []
        for j in range(NSA_HPG):
            acc = jnp.zeros((1, NSA_DH), F32)
            for br, ob in enumerate((o_c, o_s, o_w)):
                col = br * NSA_HEADS + g * NSA_HPG + j
                gcol = jnp.sum(jnp.where(glane == col, gates, 0.0), axis=-1, keepdims=True)
                acc = acc + gcol * ob[j:j + 1, :]
            rows.append(acc)
        o_ref[g * NSA_HPG:(g + 1) * NSA_HPG, :] = jnp.concatenate(rows, axis=0)


def _nsa_sel_win_sample(kv_sel, idx8, qr3, oc3, gates, sel_new, win_new, win_t, qpos, past):
    n_b = qr3.shape[0]
    wb = win_t.shape[-1]
    b3 = lambda b: (b, 0, 0)
    b5 = lambda b: (b, 0, 0, 0, 0)
    wspec = pl.BlockSpec((None, 2, NSA_KV, NSA_DH, wb), b5)
    return pl.pallas_call(
        functools.partial(_nsa_sel_win_sample_kernel, qpos=qpos, past=past),
        grid=(n_b,),
        in_specs=[pl.BlockSpec((None, SUBLANES, LANES), b3),
                  pl.BlockSpec((None, NSA_HEADS, NSA_DH), b3), pl.BlockSpec((None, NSA_HEADS, NSA_DH), b3),
                  pl.BlockSpec((None, 1, LANES), b3), pl.BlockSpec((None, 1, 2 * NSA_KVW), b3),
                  pl.BlockSpec((None, 1, 2 * NSA_KVW), b3), wspec,
                  pl.BlockSpec((None, NSA_KV, 2, NSA_DH, N_SEL * PAGE_SIZE), b5)],
        out_specs=[pl.BlockSpec((None, NSA_HEADS, NSA_DH), b3), wspec],
        out_shape=[jax.ShapeDtypeStruct((n_b, NSA_HEADS, NSA_DH), F32), jax.ShapeDtypeStruct(win_t.shape, F32)],
        compiler_params=_cparams("parallel"),
        name="nsa_sel_win_sample",
    )(idx8, qr3, oc3, gates.reshape(n_b, 1, LANES), sel_new.reshape(n_b, 1, -1),
      win_new.reshape(n_b, 1, -1), win_t, kv_sel)


def _nsa_sample_layer(x, layer, norm_g, w_in, cmp_pe, cmp_w, w_out, cache_cmp, cache_sel, win_state, page_table,
                      tables):
    n_b = x.shape[0]
    past = page_table.shape[1] * PAGE_SIZE
    qpos = past
    q, qr, cmp_rows, sel_rows, win_rows, gates, _ = _nsa_proj(
        x, norm_g.reshape(1, D_MODEL), _nsa_w_in_padded(w_in), tables, 1, n_b)
    bd, pe2 = _cmp_weights(cmp_w, cmp_pe)
    kvc = _nsa_compress_sample(cache_cmp, layer, page_table, cmp_rows, bd, pe2)
    n_sel = -(-(past + 1) // L_SEL)
    oc3, idx8 = _nsa_cmp_topk_sample(q.reshape(n_b, NSA_HEADS, NSA_DH), kvc, qpos, n_sel)
    idx = idx8[:, :NSA_KV, :N_SEL].reshape(n_b, NSA_KV * N_SEL)
    kv_sel = _nsa_gather_sel(cache_sel, layer, page_table, idx)
    o3, win_out = _nsa_sel_win_sample(kv_sel, idx8, qr.reshape(n_b, NSA_HEADS, NSA_DH), oc3, gates,
                                      sel_rows, win_rows, win_state, qpos, past)
    y = _matmul_residual(o3.reshape(n_b, NSA_Q), w_out.astype(BF16), x, n_b)
    return y, cmp_rows, sel_rows, win_out


def _gdn_sample_pre_kernel(z_ref, b0_ref, b1_ref, b2_ref, cw_ref, alog_ref, dtb_ref, q_ref, k_ref, v_ref, gb_ref):
    outs = (q_ref, k_ref, v_ref)
    for c in range(GDN_CONV_CH // _GDN_CW):
        c0 = c * _GDN_CW
        cols = slice(c0, c0 + _GDN_CW)
        w = cw_ref[:, cols]
        y = b0_ref[:, cols] * w[0:1] + b1_ref[:, cols] * w[1:2] + b2_ref[:, cols] * w[2:3] + z_ref[:, cols] * w[3:4]
        o_ref = outs[c0 // GDN_QK]
        o_ref[:, c0 % GDN_QK:c0 % GDN_QK + _GDN_CW] = _gdn_post_conv(y, c0)
    gb_ref[...] = _gdn_gates(z_ref[:, GDN_CONV_CH + GDN_VW:_GDN_W_COLS], alog_ref[...], dtb_ref[...])


def _gdn_sample_pre(z, bufs, conv_w, alog, dtb):
    n = z.shape[0]
    f = jax.ShapeDtypeStruct((n, GDN_QK), F32)
    return pl.pallas_call(
        _gdn_sample_pre_kernel,
        out_shape=[f, f, f, jax.ShapeDtypeStruct((n, LANES), F32)],
        compiler_params=pltpu.CompilerParams(vmem_limit_bytes=VMEM_LIMIT),
        name="gdn_sample_pre",
    )(z, *bufs, conv_w, alog, dtb)


def _gdn_sample_state_kernel(s0_ref, qt_ref, kt_ref, v_ref, gb_ref, gate_ref, ng_ref, s_ref, o_ref):
    gb = gb_ref[...]
    lane = _iota((1, LANES), 1)
    ng = ng_ref[...]
    for h in range(GDN_HEADS):
        cols = slice(h * GDN_DV, (h + 1) * GDN_DV)
        g_h = jnp.sum(jnp.where(lane == h, gb, 0.0), axis=-1, keepdims=True)
        beta = jnp.sum(jnp.where(lane == GDN_HEADS + h, gb, 0.0), axis=-1, keepdims=True)
        eg = jnp.exp(g_h)
        s_prev = s0_ref[h]
        k_col = kt_ref[:, h:h + 1]
        q_col = qt_ref[:, h:h + 1]
        v_row = v_ref[:, cols]
        u = v_row * beta - jnp.sum(s_prev * (k_col * beta * eg), axis=0, keepdims=True)
        attn = jnp.sum(q_col * k_col, axis=0, keepdims=True)
        o = jnp.sum(s_prev * (q_col * eg), axis=0, keepdims=True) + attn * u
        s_ref[h] = s_prev * eg + k_col * u
        o_ref[:, cols] = (_rmsnorm(o, ng) * jax.nn.silu(gate_ref[:, cols])).astype(BF16)


def _gdn_sample_state(s0, q_t, k_t, v, gb, gate, head_norm_g):
    n_b = s0.shape[0]
    st = pl.BlockSpec((None, GDN_HEADS, GDN_DK, GDN_DV), lambda b: (b, 0, 0, 0))
    col = pl.BlockSpec((None, GDN_DK, GDN_HEADS), lambda b: (b, 0, 0))
    wide = pl.BlockSpec((None, 1, GDN_VW), lambda b: (b, 0, 0))
    return pl.pallas_call(
        _gdn_sample_state_kernel,
        grid=(n_b,),
        in_specs=[st, col, col, wide, pl.BlockSpec((None, 1, LANES), lambda b: (b, 0, 0)), wide,
                  _resident((1, GDN_DV))],
        out_specs=[st, wide],
        out_shape=[jax.ShapeDtypeStruct(s0.shape, F32), jax.ShapeDtypeStruct((n_b, 1, GDN_VW), BF16)],
        compiler_params=_cparams("parallel"),
        name="gdn_sample_state",
    )(s0, q_t, k_t, v.reshape(n_b, 1, GDN_VW), gb.reshape(n_b, 1, LANES), gate.reshape(n_b, 1, GDN_VW),
      head_norm_g.reshape(1, GDN_DV))


def _gdn_sample_layer(x, norm_g, s0, conv_buf, w_in, conv_w, a_log, dt_bias, head_norm_g, w_out):
    n_b = x.shape[0]
    z = _norm_matmul_small(x, norm_g, _gdn_w_in_padded(w_in), 3)
    bufs = [conv_buf[:, j, :] for j in range(GDN_CONV - 1)]
    q, k, v, gb = _gdn_sample_pre(z, bufs, conv_w, _lane_row(a_log), _lane_row(dt_bias))
    to_cols = lambda t: t.reshape(n_b, GDN_HEADS, GDN_DK).transpose(0, 2, 1)
    gate = z[:, GDN_CONV_CH:GDN_CONV_CH + GDN_VW]
    s_new, o = _gdn_sample_state(s0, to_cols(q), to_cols(k), v, gb, gate, head_norm_g)
    y = _matmul_residual(o.reshape(n_b, GDN_VW), w_out.astype(BF16), x, n_b)
    new_buf = jnp.concatenate([conv_buf[:, 1:, :], z[:, None, :GDN_CONV_CH]], axis=1)
    return y, s_new, new_buf


def _lru_sample_kernel(z_ref, b0_ref, b1_ref, b2_ref, cw_ref, cb_ref, wga_ref, bga_ref, wgx_ref, bgx_ref, lam_ref,
                       h0_ref, h_ref, y_ref):
    for c in range(D_RNN // _LRU_CW):
        c0 = c * _LRU_CW
        cols = slice(c0, c0 + _LRU_CW)
        w = cw_ref[:, cols]
        xc = (b0_ref[:, cols] * w[0:1] + b1_ref[:, cols] * w[1:2] + b2_ref[:, cols] * w[2:3]
              + z_ref[:, D_RNN + c0:D_RNN + c0 + _LRU_CW] * w[3:4] + cb_ref[:, cols])
        a, b = _lru_gates(xc, wga_ref, bga_ref[:, cols], wgx_ref, bgx_ref[:, cols], lam_ref[:, cols], c0 // LRU_BW)
        h = a * h0_ref[:, cols] + b
        h_ref[:, cols] = h
        y_ref[:, cols] = (h * jax.nn.gelu(z_ref[:, cols])).astype(BF16)


def _lru_sample_layer(x, norm_g, h0, conv_buf, w_in, conv_w, conv_b, w_ga, b_ga, w_gx, b_gx, lam, w_out):
    n_b = x.shape[0]
    z = _norm_matmul_small(x, norm_g, w_in.astype(BF16), 2)
    r1 = lambda v: v.reshape(1, D_RNN)
    bufs = [conv_buf[:, j, :] for j in range(LRU_CONV - 1)]
    h, y = pl.pallas_call(
        _lru_sample_kernel,
        out_shape=[jax.ShapeDtypeStruct((n_b, D_RNN), F32), jax.ShapeDtypeStruct((n_b, D_RNN), BF16)],
        compiler_params=pltpu.CompilerParams(vmem_limit_bytes=VMEM_LIMIT),
        name="lru_sample",
    )(z, *bufs, conv_w, r1(conv_b), w_ga.astype(BF16), r1(b_ga), w_gx.astype(BF16), r1(b_gx), r1(lam), h0)
    y = _matmul_residual(y, w_out.astype(BF16), x, n_b)
    new_buf = jnp.concatenate([conv_buf[:, 1:, :], z[:, None, D_RNN:]], axis=1)
    return y, h, new_buf


def _ffn_sample_kernel(x_ref, g_ref, wv_ref, wg_ref, cwv_ref, cwg_ref, b0v_ref, b0g_ref, b1v_ref, b1g_ref, wout_ref,
                       y_ref, uv_ref, ug_ref, xn_scr):
    c = pl.program_id(0)

    @pl.when(c == 0)
    def _():
        x = x_ref[...]
        xn_scr[...] = _rmsnorm(x, g_ref[...]).astype(BF16)
        y_ref[...] = x

    xn = xn_scr[...]
    uv = _dot(xn, wv_ref[...])
    ug = _dot(xn, wg_ref[...])
    uv_ref[...] = uv
    ug_ref[...] = ug
    cwv, cwg = cwv_ref[...], cwg_ref[...]
    val = b0v_ref[...] * cwv[0:1] + b1v_ref[...] * cwv[1:2] + uv * cwv[2:3]
    gt = b0g_ref[...] * cwg[0:1] + b1g_ref[...] * cwg[1:2] + ug * cwg[2:3]
    y_ref[...] += _dot((val * jax.nn.silu(gt)).astype(BF16), wout_ref[...])


def _ffn_sample(x, norm_g, w_in, conv_w, w_out, buf):
    n_b = x.shape[0]
    n_c = D_FF // _FFN_CW
    b0, b1 = buf[:, 0, :], buf[:, 1, :]
    val = lambda shape0: pl.BlockSpec((shape0, _FFN_CW), lambda c: (0, c))
    gat = lambda shape0: pl.BlockSpec((shape0, _FFN_CW), lambda c: (0, n_c + c))
    y, uv, ug = pl.pallas_call(
        _ffn_sample_kernel,
        grid=(n_c,),
        in_specs=[_resident((n_b, D_MODEL)), _resident((1, D_MODEL)), val(D_MODEL), gat(D_MODEL),
                  val(FFN_CONV), gat(FFN_CONV), val(n_b), gat(n_b), val(n_b), gat(n_b),
                  pl.BlockSpec((_FFN_CW, D_MODEL), lambda c: (c, 0))],
        out_specs=[pl.BlockSpec((n_b, D_MODEL), lambda c: (0, 0)), val(n_b), val(n_b)],
        out_shape=[jax.ShapeDtypeStruct((n_b, D_MODEL), F32), jax.ShapeDtypeStruct((n_b, D_FF), F32),
                   jax.ShapeDtypeStruct((n_b, D_FF), F32)],
        scratch_shapes=[pltpu.VMEM((n_b, D_MODEL), BF16)],
        compiler_params=_cparams("arbitrary"),
        name="ffn_sample",
    )(x, norm_g.reshape(1, D_MODEL), w_in, w_in, conv_w, conv_w, b0, b0, b1, b1, w_out)
    new_buf = jnp.stack([b1, jnp.concatenate([uv, ug], axis=-1)], axis=1)
    return y, new_buf


def kernel(x_prompt, x_sample, cache_nsa_cmp, cache_nsa_sel, state_nsa_win, state_gdn_S, state_gdn_conv,
           state_lru_h, state_lru_conv, state_ffn_conv, page_table,
           norm_mix_g, norm_ffn_g, norm_final_g,
           nsa_w_in, nsa_cmp_pe, nsa_cmp_w, nsa_w_out,
           gdn_w_in, gdn_conv_w, gdn_A_log, gdn_dt_bias, gdn_norm_g, gdn_w_out,
           lru_w_in, lru_conv_w, lru_conv_b, lru_w_ga, lru_b_ga, lru_w_gx, lru_b_gx, lru_lambda, lru_w_out,
           ffn_w_in, ffn_conv_w, ffn_w_out):
    n_p, t_len, _ = x_prompt.shape
    n_s = x_sample.shape[0]
    past = page_table.shape[1] * PAGE_SIZE
    wl = min(WINDOW, t_len)
    xp = x_prompt.reshape(n_p * t_len, D_MODEL)
    xs = x_sample.reshape(n_s, D_MODEL)
    tab_p = _rope_tables(jnp.arange(t_len, dtype=I32))
    tab_s = _rope_tables(jnp.full((n_s,), past, I32))
    cache_cmp_t = _rows_last(cache_nsa_cmp)
    cache_sel_t = _rows_last(cache_nsa_sel)
    win_t = _rows_last(state_nsa_win)
    ffn_w_in16 = ffn_w_in.astype(BF16)
    ffn_w_out16 = ffn_w_out.astype(BF16)
    zeros = lambda *shape: jnp.zeros(shape, F32)
    kv5 = lambda rows, nb: rows.reshape(nb, -1, 2, NSA_KV, NSA_DH)

    p_nsa, s_nsa, p_gdn, s_gdn, p_lru, s_lru, p_ffn, s_ffn = [], [], [], [], [], [], [], []
    for li in range(DEPTH):
        j = li // N_MIXERS
        kind = li % N_MIXERS
        if kind == 0:
            xp, c, s, w = _nsa_prompt_layer(xp, n_p, norm_mix_g[li], nsa_w_in[j], nsa_cmp_pe[j], nsa_cmp_w[j],
                                            nsa_w_out[j], tab_p)
            rows_first = lambda a: jnp.transpose(a, (0, 4, 1, 2, 3))
            p_nsa.append((rows_first(c), rows_first(s), rows_first(w[..., t_len - wl:])))
            xs, c, s, w = _nsa_sample_layer(xs, j, norm_mix_g[li], nsa_w_in[j], nsa_cmp_pe[j], nsa_cmp_w[j],
                                            nsa_w_out[j], cache_cmp_t, cache_sel_t, win_t[j], page_table, tab_s)
            s_nsa.append((kv5(c, n_s), kv5(s, n_s), jnp.transpose(w, (0, 4, 1, 2, 3))))
        elif kind == 1:
            args = (gdn_w_in[j], gdn_conv_w[j], gdn_A_log[j], gdn_dt_bias[j], gdn_norm_g[j], gdn_w_out[j])
            xp, s_new, buf = _gdn_prompt_layer(xp, n_p, norm_mix_g[li], zeros(n_p, GDN_HEADS, GDN_DK, GDN_DV),
                                               zeros(n_p, GDN_CONV - 1, GDN_CONV_CH), *args)
            p_gdn.append((s_new, buf))
            xs, s_new, buf = _gdn_sample_layer(xs, norm_mix_g[li], state_gdn_S[j], state_gdn_conv[j], *args)
            s_gdn.append((s_new, buf))
        else:
            args = (lru_w_in[j], lru_conv_w[j], lru_conv_b[j], lru_w_ga[j], lru_b_ga[j], lru_w_gx[j], lru_b_gx[j],
                    lru_lambda[j], lru_w_out[j])
            xp, h, buf = _lru_prompt_layer(xp, n_p, norm_mix_g[li], zeros(n_p, D_RNN), zeros(n_p, LRU_CONV - 1, D_RNN),
                                           *args)
            p_lru.append((h, buf))
            xs, h, buf = _lru_sample_layer(xs, norm_mix_g[li], state_lru_h[j], state_lru_conv[j], *args)
            s_lru.append((h, buf))
        xp, buf = _ffn_prompt(xp, n_p, norm_ffn_g[li], ffn_w_in16[li], ffn_conv_w[li], ffn_w_out16[li],
                              zeros(n_p, FFN_CONV - 1, 2 * D_FF))
        p_ffn.append(buf)
        xs, buf = _ffn_sample(xs, norm_ffn_g[li], ffn_w_in16[li], ffn_conv_w[li], ffn_w_out16[li], state_ffn_conv[li])
        s_ffn.append(buf)

    y_prompt = _final_norm(xp, norm_final_g, 512).reshape(n_p, t_len, D_MODEL)
    y_sample = _final_norm(xs, norm_final_g, n_s).reshape(n_s, 1, D_MODEL)
    stack = lambda entries, k: jnp.stack([e[k] for e in entries])
    return (y_prompt, y_sample,
            stack(p_nsa, 0), stack(s_nsa, 0), stack(p_nsa, 1), stack(s_nsa, 1), stack(p_nsa, 2), stack(s_nsa, 2),
            stack(p_gdn, 0), stack(s_gdn, 0), stack(p_gdn, 1), stack(s_gdn, 1),
            stack(p_lru, 0), stack(s_lru, 0), stack(p_lru, 1), stack(s_lru, 1),
            jnp.stack(p_ffn), jnp.stack(s_ffn))
```
